```python
import math, functools
import jax, jax.numpy as jnp
from jax import lax
import numpy as np

D_MODEL = 1024
BATCH = 2
SEQ = 8192
DEPTH = 2
DEC_BATCH = 128
DEC_SEQ = 8
PAST_LEN = 2048
PAGE_SIZE = 128

N_HEADS = 16
HEAD_DIM = D_MODEL // N_HEADS
N_KV_HEADS = 4
GROUP = N_HEADS // N_KV_HEADS
L_CMP = 32
D_CMP = 16
L_SEL = 64
TOP_N = 16
WINDOW = 512
Q_BLOCK = 128
SSM_CH = 16
N_GROUPS = D_MODEL // SSM_CH
STATE_P = 64
D_FF = 2816
CONV_W = 3
PLE_DIM = 256

N_MIXERS = 2
N_ATTN = (DEPTH + 1) // 2
N_SSM = DEPTH // 2
Q_COLS = N_HEADS * HEAD_DIM
KV_COLS = N_KV_HEADS * HEAD_DIM
ATTN_IN_COLS = Q_COLS + 6 * KV_COLS + 3 * N_HEADS
NORM_EPS = 1e-6
NEG_INF = -1e30
FORCE_BONUS = 1e4

kernel_name = "nsa_s5_hybrid_decode_step"


def rms_norm(x, gain):
    xf = x.astype(jnp.float32)
    y = xf * lax.rsqrt(jnp.mean(xf * xf, axis=-1, keepdims=True) + NORM_EPS)
    return (y * gain.astype(jnp.float32)).astype(x.dtype)


def alibi_slopes():
    return jnp.exp2(-8.0 * jnp.arange(1, N_HEADS + 1, dtype=jnp.float32) / N_HEADS)


def masked_softmax(s, mask):
    p = jax.nn.softmax(jnp.where(mask, s, NEG_INF), axis=-1)
    return p * mask


def compress_rows(rows, pe, w1, w2):
    nc = (rows.shape[0] - L_CMP) // D_CMP + 1
    idx = jnp.arange(nc)[:, None] * D_CMP + jnp.arange(L_CMP)[None, :]
    blk = rows[idx] + pe[None, :, None, :]
    h = jax.nn.gelu(jnp.einsum('nlgd,lde->nge', blk, w1))
    return jnp.einsum('nge,ef->ngf', h, w2)


def n_sel_blocks(length):
    return max(-(-length // L_SEL), TOP_N)


def selection_blocks(rows, ns_pad):
    rows = jnp.pad(rows, ((0, ns_pad * L_SEL - rows.shape[0]), (0, 0), (0, 0)))
    return rows.reshape(ns_pad, L_SEL, N_KV_HEADS, HEAD_DIM).transpose(2, 0, 1, 3)


def nsa_attend(q, q_pos, gates, kc, vc, ksb, vsb, kw, vw, w_pos):
    f32 = jnp.float32
    tq = q.shape[0]
    scale = HEAD_DIM ** -0.5
    qg = q.reshape(tq, N_KV_HEADS, GROUP, HEAD_DIM)
    slopes = alibi_slopes().reshape(1, N_KV_HEADS, GROUP, 1)
    qp = q_pos.astype(f32)

    nc = kc.shape[0]
    c_start = jnp.arange(nc) * D_CMP
    c_center = c_start.astype(f32) + (L_CMP - 1) / 2
    s = jnp.einsum('tgrd,ngd->tgrn', qg, kc).astype(f32) * scale
    s = s - slopes * (qp[:, None, None, None] - c_center)
    c_mask = (c_start + L_CMP - 1 <= q_pos[:, None])[:, None, None, :]
    p_cmp = masked_softmax(s, c_mask)
    o_cmp = jnp.einsum('tgrn,ngd->tgrd', p_cmp, vc.astype(f32))

    ns = ksb.shape[1]
    ratio = L_SEL // D_CMP
    imp = jnp.pad(p_cmp.sum(axis=2), ((0, 0), (0, 0), (0, ns * ratio - nc)))
    imp = imp.reshape(tq, N_KV_HEADS, ns, ratio).sum(-1)
    blk = jnp.arange(ns)
    cur = (q_pos // L_SEL)[:, None]
    forced = (blk == 0) | (blk == cur) | (blk == cur - 1)
    started = blk * L_SEL <= q_pos[:, None]
    score = jnp.where(started[:, None, :], imp + FORCE_BONUS * forced[:, None, :], NEG_INF)
    _, idx = lax.top_k(score, TOP_N)
    g_ix = jnp.arange(N_KV_HEADS)[None, :, None]
    ks = ksb[g_ix, idx]
    vs = vsb[g_ix, idx]
    tok = idx[..., None] * L_SEL + jnp.arange(L_SEL)
    dist = (q_pos[:, None, None, None] - tok).reshape(tq, N_KV_HEADS, 1, TOP_N * L_SEL)
    s = jnp.einsum('tgrd,tgkld->tgrkl', qg, ks).astype(f32).reshape(tq, N_KV_HEADS, GROUP, TOP_N * L_SEL)
    s = s * scale - slopes * dist.astype(f32)
    p_sel = masked_softmax(s, dist >= 0)
    o_sel = jnp.einsum('tgrkl,tgkld->tgrd',
                       p_sel.reshape(tq, N_KV_HEADS, GROUP, TOP_N, L_SEL), vs.astype(f32))

    wd = q_pos[:, None] - w_pos[None, :]
    w_mask = ((wd >= 0) & (wd < WINDOW) & (w_pos >= 0)[None, :])[:, None, None, :]
    s = jnp.einsum('tgrd,lgd->tgrl', qg, kw).astype(f32) * scale - slopes * wd.astype(f32)[:, None, None, :]
    p_win = masked_softmax(s, w_mask)
    o_win = jnp.einsum('tgrl,lgd->tgrd', p_win, vw.astype(f32))

    g = gates.astype(f32).reshape(tq, N_KV_HEADS, GROUP, 3)
    o = g[..., 0:1] * o_cmp + g[..., 1:2] * o_sel + g[..., 2:3] * o_win
    return o.reshape(tq, N_HEADS, HEAD_DIM).astype(q.dtype)


def nsa_project(u, w_in):
    b, t, _ = u.shape
    z = u @ w_in
    q = z[..., :Q_COLS].reshape(b, t, N_HEADS, HEAD_DIM)
    kv = z[..., Q_COLS:Q_COLS + 4 * KV_COLS].reshape(b, t, 4, N_KV_HEADS, HEAD_DIM)
    win = z[..., Q_COLS + 4 * KV_COLS:Q_COLS + 6 * KV_COLS].reshape(b, t, 2, N_KV_HEADS, HEAD_DIM)
    gates = jax.nn.sigmoid(z[..., Q_COLS + 6 * KV_COLS:]).reshape(b, t, N_HEADS, 3)
    return q, kv, win, gates


def nsa_prompt_seq(q, gates, kv, win, cmp_pe, cmp_w1, cmp_w2):
    t = q.shape[0]
    kc = compress_rows(kv[:, 0], cmp_pe[0], cmp_w1[0], cmp_w2[0])
    vc = compress_rows(kv[:, 1], cmp_pe[1], cmp_w1[1], cmp_w2[1])
    ns = n_sel_blocks(t)
    ksb = selection_blocks(kv[:, 2], ns)
    vsb = selection_blocks(kv[:, 3], ns)
    win_pad = jnp.pad(win, ((WINDOW, 0), (0, 0), (0, 0), (0, 0)))
    band = WINDOW + Q_BLOCK

    def block(b):
        q0 = b * Q_BLOCK
        qb = lax.dynamic_slice_in_dim(q, q0, Q_BLOCK, 0)
        gb = lax.dynamic_slice_in_dim(gates, q0, Q_BLOCK, 0)
        wb = lax.dynamic_slice_in_dim(win_pad, q0, band, 0)
        q_pos = q0 + jnp.arange(Q_BLOCK)
        w_pos = q0 - WINDOW + jnp.arange(band)
        return nsa_attend(qb, q_pos, gb, kc, vc, ksb, vsb, wb[:, 0], wb[:, 1], w_pos)

    o = lax.map(block, jnp.arange(t // Q_BLOCK))
    return o.reshape(t, N_HEADS, HEAD_DIM)


def nsa_layer_prompt(u, w_in, w_out, cmp_pe, cmp_w1, cmp_w2):
    b, t, _ = u.shape
    q, kv, win, gates = nsa_project(u, w_in)
    o = jax.vmap(nsa_prompt_seq, in_axes=(0, 0, 0, 0, None, None, None))(q, gates, kv, win, cmp_pe, cmp_w1, cmp_w2)
    keep = min(WINDOW, t)
    return o.reshape(b, t, Q_COLS) @ w_out, kv, win[:, t - keep:]


def nsa_layer_sample(u, cache_kv, layer, cache_win_l, page_table, w_in, w_out, cmp_pe, cmp_w1, cmp_w2):
    b, t, _ = u.shape
    q, kv, win, gates = nsa_project(u, w_in)
    past_len = page_table.shape[1] * cache_kv.shape[2]
    wbuf = cache_win_l.shape[1]
    ns = n_sel_blocks(past_len + t)
    q_pos = past_len + jnp.arange(t)
    w_pos = past_len - wbuf + jnp.arange(wbuf + t)

    def one(args):
        qs, gs, kv_new, win_new, win_past, prow = args
        past = cache_kv[layer, prow].reshape(past_len, 4, N_KV_HEADS, HEAD_DIM)
        rows = jnp.concatenate([past, kv_new], axis=0)
        kc = compress_rows(rows[:, 0], cmp_pe[0], cmp_w1[0], cmp_w2[0])
        vc = compress_rows(rows[:, 1], cmp_pe[1], cmp_w1[1], cmp_w2[1])
        ksb = selection_blocks(rows[:, 2], ns)
        vsb = selection_blocks(rows[:, 3], ns)
        band = jnp.concatenate([win_past, win_new], axis=0)
        o = nsa_attend(qs, q_pos, gs, kc, vc, ksb, vsb, band[:, 0], band[:, 1], w_pos)
        return o, band[t:]

    o, new_win = lax.map(one, (q, gates, kv, win, cache_win_l, page_table))
    return o.reshape(b, t, Q_COLS) @ w_out, kv, new_win


def ssm_combine(left, right):
    a_l, b_l = left
    a_r, b_r = right
    return a_l * a_r, a_r * b_l + b_r


def s5_layer(u, h0, a_re, a_im, log_dt, b_re, b_im, c_re, c_im, d_skip, w_glu):
    f32 = jnp.float32
    bsz, t, _ = u.shape
    lam = lax.complex(a_re.astype(f32), a_im.astype(f32))
    dt = jnp.exp(log_dt.astype(f32))[:, None]
    a_bar = jnp.exp(lam * dt)
    b_bar = ((a_bar - 1.0) / lam)[:, :, None] * lax.complex(b_re.astype(f32), b_im.astype(f32))
    c_mat = lax.complex(c_re.astype(f32), c_im.astype(f32))
    uf = u.astype(f32)
    ug = uf.reshape(bsz, t, N_GROUPS, SSM_CH).astype(jnp.complex64)
    bu = jnp.einsum('gpc,btgc->btgp', b_bar, ug)
    bu = bu.at[:, 0].add(a_bar * h0)
    a_seq = jnp.broadcast_to(a_bar, bu.shape)
    _, h = lax.associative_scan(ssm_combine, (a_seq, bu), axis=1)
    y = jnp.einsum('gcp,btgp->btgc', c_mat, h).real.reshape(bsz, t, D_MODEL) + d_skip.astype(f32) * uf
    z = jax.nn.gelu(y)
    ab = z @ w_glu.astype(f32)
    out = ab[..., :D_MODEL] * jax.nn.sigmoid(ab[..., D_MODEL:])
    return out.astype(u.dtype), h[:, -1]


def conv_ffn(u, buf, w_up, conv_w, conv_b, w_down):
    t = u.shape[1]
    hg = u @ w_up
    h, g = hg[..., :D_FF], hg[..., D_FF:]
    hc = jnp.concatenate([buf.astype(h.dtype), h], axis=1)
    conv = conv_b + sum(conv_w[j] * hc[:, j:j + t] for j in range(CONV_W))
    return (jax.nn.gelu(conv) * g) @ w_down, hc[:, t:]


def per_layer_embed(x, p, gain, w_proj, w_gate):
    gate = jax.nn.sigmoid(rms_norm(x, gain) @ w_gate)
    return x + (p @ w_proj) * gate


def setup_inputs(seed: int = 0) -> dict:
    key = jax.random.key(seed)
    keys = iter(jax.random.split(key, 40))
    f32 = jnp.float32

    def nrm(shape, scale=1.0):
        return jax.random.normal(next(keys), shape, f32) * scale

    n_pages = PAST_LEN // PAGE_SIZE
    n_pool = (DEC_BATCH * n_pages * 5) // 4
    win_buf = min(WINDOW, PAST_LEN)
    inp = {}
    inp["x_prompt"] = nrm((BATCH, SEQ, D_MODEL))
    inp["x_sample"] = nrm((DEC_BATCH, DEC_SEQ, D_MODEL))
    inp["cache_kv"] = nrm((N_ATTN, n_pool, PAGE_SIZE, 4, N_KV_HEADS, HEAD_DIM))
    inp["cache_win"] = nrm((N_ATTN, DEC_BATCH, win_buf, 2, N_KV_HEADS, HEAD_DIM))
    inp["state_ssm_re"] = nrm((N_SSM, DEC_BATCH, N_GROUPS, STATE_P), 0.1)
    inp["state_ssm_im"] = nrm((N_SSM, DEC_BATCH, N_GROUPS, STATE_P), 0.1)
    inp["state_conv"] = nrm((DEPTH, DEC_BATCH, CONV_W - 1, D_FF))
    perm = jax.random.permutation(next(keys), n_pool)
    inp["page_table"] = perm[:DEC_BATCH * n_pages].reshape(DEC_BATCH, n_pages).astype(jnp.int32)
    inp["p_prompt"] = nrm((DEPTH, BATCH, SEQ, PLE_DIM))
    inp["p_sample"] = nrm((DEPTH, DEC_BATCH, DEC_SEQ, PLE_DIM))
    inp["norm_mix"] = 1.0 + nrm((DEPTH, D_MODEL), 0.01)
    inp["norm_ffn"] = 1.0 + nrm((DEPTH, D_MODEL), 0.01)
    inp["norm_ple"] = 1.0 + nrm((DEPTH, D_MODEL), 0.01)
    inp["norm_final"] = 1.0 + nrm((D_MODEL,), 0.01)
    inp["w_attn_in"] = nrm((N_ATTN, D_MODEL, ATTN_IN_COLS), D_MODEL ** -0.5)
    inp["w_attn_out"] = nrm((N_ATTN, Q_COLS, D_MODEL), Q_COLS ** -0.5)
    inp["cmp_pe"] = nrm((N_ATTN, 2, L_CMP, HEAD_DIM), 0.02)
    inp["cmp_w1"] = nrm((N_ATTN, 2, L_CMP, HEAD_DIM, HEAD_DIM), (L_CMP * HEAD_DIM) ** -0.5)
    inp["cmp_w2"] = nrm((N_ATTN, 2, HEAD_DIM, HEAD_DIM), HEAD_DIM ** -0.5)
    inp["ssm_a_re"] = -0.5 + nrm((N_SSM, N_GROUPS, STATE_P), 0.01)
    inp["ssm_a_im"] = jnp.pi * jnp.arange(STATE_P, dtype=f32) + nrm((N_SSM, N_GROUPS, STATE_P), 0.01)
    inp["ssm_log_dt"] = jax.random.uniform(next(keys), (N_SSM, N_GROUPS), f32, math.log(1e-3), math.log(1e-1))
    inp["ssm_b_re"] = nrm((N_SSM, N_GROUPS, STATE_P, SSM_CH), (2 * SSM_CH) ** -0.5)
    inp["ssm_b_im"] = nrm((N_SSM, N_GROUPS, STATE_P, SSM_CH), (2 * SSM_CH) ** -0.5)
    inp["ssm_c_re"] = nrm((N_SSM, N_GROUPS, SSM_CH, STATE_P), STATE_P ** -0.5)
    inp["ssm_c_im"] = nrm((N_SSM, N_GROUPS, SSM_CH, STATE_P), STATE_P ** -0.5)
    inp["ssm_d"] = nrm((N_SSM, D_MODEL), 0.5)
    inp["w_glu"] = nrm((N_SSM, D_MODEL, 2 * D_MODEL), D_MODEL ** -0.5)
    inp["w_ffn_up"] = nrm((DEPTH, D_MODEL, 2 * D_FF), D_MODEL ** -0.5)
    inp["ffn_conv_w"] = nrm((DEPTH, CONV_W, D_FF), CONV_W ** -0.5)
    inp["ffn_conv_b"] = nrm((DEPTH, D_FF), 0.01)
    inp["w_ffn_down"] = nrm((DEPTH, D_FF, D_MODEL), D_FF ** -0.5)
    inp["w_ple_proj"] = nrm((DEPTH, PLE_DIM, D_MODEL), PLE_DIM ** -0.5)
    inp["w_ple_gate"] = nrm((DEPTH, D_MODEL, D_MODEL), D_MODEL ** -0.5)
    return inp


def reference(x_prompt, x_sample, cache_kv, cache_win, state_ssm_re, state_ssm_im, state_conv, page_table,
              p_prompt, p_sample, norm_mix, norm_ffn, norm_ple, norm_final, w_attn_in, w_attn_out,
              cmp_pe, cmp_w1, cmp_w2, ssm_a_re, ssm_a_im, ssm_log_dt, ssm_b_re, ssm_b_im, ssm_c_re, ssm_c_im,
              ssm_d, w_glu, w_ffn_up, ffn_conv_w, ffn_conv_b, w_ffn_down, w_ple_proj, w_ple_gate):
    f32 = jnp.float32
    bsz = x_prompt.shape[0]
    xp, xs = x_prompt, x_sample
    kv_p, kv_s, win_p, win_s = [], [], [], []
    sre_p, sim_p, sre_s, sim_s = [], [], [], []
    cb_p, cb_s = [], []
    for i in range(DEPTH):
        j = i // N_MIXERS
        up = rms_norm(xp, norm_mix[i])
        us = rms_norm(xs, norm_mix[i])
        if i % N_MIXERS == 0:
            mp, kvp, wp = nsa_layer_prompt(up, w_attn_in[j], w_attn_out[j], cmp_pe[j], cmp_w1[j], cmp_w2[j])
            ms, kvs, ws = nsa_layer_sample(us, cache_kv, j, cache_win[j], page_table,
                                           w_attn_in[j], w_attn_out[j], cmp_pe[j], cmp_w1[j], cmp_w2[j])
            kv_p.append(kvp)
            kv_s.append(kvs)
            win_p.append(wp)
            win_s.append(ws)
        else:
            h0p = jnp.zeros((bsz, N_GROUPS, STATE_P), jnp.complex64)
            h0s = lax.complex(state_ssm_re[j].astype(f32), state_ssm_im[j].astype(f32))
            mp, hp = s5_layer(up, h0p, ssm_a_re[j], ssm_a_im[j], ssm_log_dt[j], ssm_b_re[j], ssm_b_im[j],
                              ssm_c_re[j], ssm_c_im[j], ssm_d[j], w_glu[j])
            ms, hs = s5_layer(us, h0s, ssm_a_re[j], ssm_a_im[j], ssm_log_dt[j], ssm_b_re[j], ssm_b_im[j],
                              ssm_c_re[j], ssm_c_im[j], ssm_d[j], w_glu[j])
            sre_p.append(hp.real)
            sim_p.append(hp.imag)
            sre_s.append(hs.real)
            sim_s.append(hs.imag)
        xp = xp + mp
        xs = xs + ms
        fp, bp = conv_ffn(rms_norm(xp, norm_ffn[i]), jnp.zeros((bsz, CONV_W - 1, D_FF), xp.dtype),
                          w_ffn_up[i], ffn_conv_w[i], ffn_conv_b[i], w_ffn_down[i])
        fs, bs = conv_ffn(rms_norm(xs, norm_ffn[i]), state_conv[i],
                          w_ffn_up[i], ffn_conv_w[i], ffn_conv_b[i], w_ffn_down[i])
        cb_p.append(bp)
        cb_s.append(bs)
        xp = per_layer_embed(xp + fp, p_prompt[i], norm_ple[i], w_ple_proj[i], w_ple_gate[i])
        xs = per_layer_embed(xs + fs, p_sample[i], norm_ple[i], w_ple_proj[i], w_ple_gate[i])
    y_prompt = rms_norm(xp, norm_final)
    y_sample = rms_norm(xs, norm_final)
    new_kv_prompt = jnp.stack(kv_p)
    new_kv_sample = jnp.stack(kv_s)
    new_win_prompt = jnp.stack(win_p)
    new_win_sample = jnp.stack(win_s)
    new_ssm_re_prompt = jnp.stack(sre_p)
    new_ssm_im_prompt = jnp.stack(sim_p)
    new_ssm_re_sample = jnp.stack(sre_s)
    new_ssm_im_sample = jnp.stack(sim_s)
    new_conv_prompt = jnp.stack(cb_p)
    new_conv_sample = jnp.stack(cb_s)
    return (y_prompt, y_sample, new_kv_prompt, new_kv_sample, new_win_prompt, new_win_sample,
            new_ssm_re_prompt, new_ssm_im_prompt, new_ssm_re_sample, new_ssm_im_sample,
            new_conv_prompt, new_conv_sample)
```

```python
import functools
import math

import numpy as np
import jax
import jax.numpy as jnp
from jax import lax
from jax.experimental import pallas as pl
from jax.experimental.pallas import tpu as pltpu

F32 = jnp.float32
BF16 = jnp.bfloat16

N_HEADS = 16
HEAD_DIM = 64
N_KV_HEADS = 4
GROUP = N_HEADS // N_KV_HEADS
KV_COLS = N_KV_HEADS * HEAD_DIM
L_CMP = 32
D_CMP = 16
L_SEL = 64
TOP_N = 16
WINDOW = 512
SSM_CH = 16
STATE_P = 64
CONV_W = 3
NORM_EPS = 1e-6
NEG_INF = -1e30
FORCE_BONUS = 1e4
SEL_PER_CMP = L_SEL // D_CMP

LANES = 128
SUBLANES = 8
VMEM_LIMIT = 56 * 1024 * 1024

Q_TILE = 128
SEL_CHUNK = 512
SSM_LANE_CHUNK = 512
SSM_GROUP_BLOCK = 16


def _cparams(sem):
    return pltpu.CompilerParams(dimension_semantics=sem, vmem_limit_bytes=VMEM_LIMIT)


def _const_spec(shape):
    nd = len(shape)
    return pl.BlockSpec(shape, lambda *_: (0,) * nd, pipeline_mode=pl.Buffered(1))


def _rms(x, gain):
    y = x * lax.rsqrt(jnp.mean(x * x, axis=-1, keepdims=True) + NORM_EPS)
    return y * gain


def _gelu(x):
    c = math.sqrt(2.0 / math.pi)
    return x * (0.5 * (1.0 + jnp.tanh(c * (x + 0.044715 * (x * x * x)))))


def _sigmoid(x):
    return 1.0 / (1.0 + jnp.exp(-x))


def _dot(a, b):
    return jnp.dot(a, b, preferred_element_type=F32)


def _dot_nt(a, b):
    return lax.dot_general(a, b, (((1,), (1,)), ((), ())), preferred_element_type=F32)


def _masked_softmax(s, mask):
    sm = jnp.where(mask, s, NEG_INF)
    m = jnp.max(sm, axis=-1, keepdims=True)
    e = jnp.where(mask, jnp.exp(sm - m), 0.0)
    den = jnp.sum(e, axis=-1, keepdims=True)
    return e / jnp.where(den > 0.0, den, 1.0)


def _exact_pool(x, pool_bf):
    hi = x.astype(BF16)
    r1 = x - hi.astype(F32)
    mid = r1.astype(BF16)
    lo = (r1 - mid.astype(F32)).astype(BF16)
    return _dot(hi, pool_bf) + _dot(mid, pool_bf) + _dot(lo, pool_bf)


def _topk_mask(score, k):
    lane = lax.broadcasted_iota(jnp.int32, score.shape, 1).astype(F32)
    sel = jnp.zeros(score.shape, F32)
    s = score
    for _ in range(k):
        m = jnp.max(s, axis=-1, keepdims=True)
        idx = jnp.min(jnp.where(s == m, lane, float(LANES)), axis=-1, keepdims=True)
        hit = lane == idx
        sel = jnp.where(hit, 1.0, sel)
        s = jnp.where(hit, -jnp.inf, s)
    return sel


def _slot_ids():
    return lax.broadcasted_iota(jnp.int32, (1, KV_COLS), 1) // HEAD_DIM


def _place_slot(x, src_slot, dst_slot):
    shift = (HEAD_DIM * (dst_slot - src_slot)) % KV_COLS
    y = pltpu.roll(x, shift, 1) if shift else x
    return jnp.where(_slot_ids() == dst_slot, y, 0.0)


def _selection_scores(imp, tpos):
    blk = lax.broadcasted_iota(jnp.int32, imp.shape, 1)
    cur = tpos // L_SEL
    forced = (blk == 0) | (blk == cur) | (blk == cur - 1)
    started = blk * L_SEL <= tpos
    return jnp.where(started, imp + jnp.where(forced, FORCE_BONUS, 0.0), NEG_INF)


def _attn_in_kernel(x_ref, gain_ref, wm_ref, wg_ref,
                    q_ref, kv_ref, win_ref, ksel_ref, vsel_ref, kwin_ref, vwin_ref, gates_ref):
    d = x_ref.shape[1]
    u = _rms(x_ref[...], gain_ref[...]).astype(BF16)
    z = _dot(u, wm_ref[...])
    q_ref[...] = z[:, :d]
    kv = z[:, d:d + 4 * KV_COLS]
    kv_ref[...] = kv
    win = z[:, d + 4 * KV_COLS:d + 6 * KV_COLS]
    win_ref[...] = win
    ksel_ref[...] = kv[:, 2 * KV_COLS:3 * KV_COLS].astype(BF16)
    vsel_ref[...] = kv[:, 3 * KV_COLS:4 * KV_COLS].astype(BF16)
    kwin_ref[...] = win[:, :KV_COLS].astype(BF16)
    vwin_ref[...] = win[:, KV_COLS:].astype(BF16)
    gates_ref[...] = _sigmoid(_dot(u, wg_ref[...]))


def _attn_in_proj(x2, gain, w_main, w_gate):
    rows, d = x2.shape
    tm = min(512, rows)
    nmain = w_main.shape[1]
    row = lambda n: pl.BlockSpec((tm, n), lambda i: (i, 0))
    outs = [(d, F32), (4 * KV_COLS, F32), (2 * KV_COLS, F32), (KV_COLS, BF16), (KV_COLS, BF16),
            (KV_COLS, BF16), (KV_COLS, BF16), (LANES, F32)]
    return pl.pallas_call(
        _attn_in_kernel,
        grid=(rows // tm,),
        in_specs=[row(d), _const_spec((1, d)), _const_spec((d, nmain)), _const_spec((d, LANES))],
        out_specs=[row(n) for n, _ in outs],
        out_shape=[jax.ShapeDtypeStruct((rows, n), dt) for n, dt in outs],
        compiler_params=_cparams(("arbitrary",)),
        name="attn_in_proj",
    )(x2, gain, w_main, w_gate)


def _matmul_res_kernel(a_ref, w_ref, x_ref, o_ref):
    o_ref[...] = x_ref[...] + _dot(a_ref[...].astype(BF16), w_ref[...])


def _matmul_res(a, w, x):
    rows, k = a.shape
    n = w.shape[1]
    tm = min(512, rows)
    return pl.pallas_call(
        _matmul_res_kernel,
        grid=(rows // tm,),
        in_specs=[pl.BlockSpec((tm, k), lambda i: (i, 0)), _const_spec((k, n)),
                  pl.BlockSpec((tm, n), lambda i: (i, 0))],
        out_specs=pl.BlockSpec((tm, n), lambda i: (i, 0)),
        out_shape=jax.ShapeDtypeStruct((rows, n), F32),
        compiler_params=_cparams(("arbitrary",)),
        name="matmul_res",
    )(a, w, x)


def _finish_compress(a, b, w2):
    nseg = a.shape[0]
    pre = a + pltpu.roll(b, nseg - 1, 0)
    out = _dot(_gelu(pre).astype(BF16), w2)
    rowi = lax.broadcasted_iota(jnp.int32, (nseg, 1), 0)
    return jnp.where(rowi < nseg - 1, out, 0.0)


def _segment_sums_kernel(finish, x_ref, pea_ref, peb_ref, w1a_ref, w1b_ref, *refs):
    if finish:
        w2_ref, o_ref, acc_a, acc_b = refs
    else:
        a_ref, b_ref, acc_a, acc_b = refs
    l = pl.program_id(2)

    @pl.when(l == 0)
    def _():
        acc_a[...] = jnp.zeros(acc_a.shape, F32)
        acc_b[...] = jnp.zeros(acc_b.shape, F32)

    xs = x_ref[0]
    acc_a[...] += _dot((xs + pea_ref[0, 0]).astype(BF16), w1a_ref[0, 0])
    acc_b[...] += _dot((xs + peb_ref[0, 0]).astype(BF16), w1b_ref[0, 0])

    @pl.when(l == D_CMP - 1)
    def _():
        if finish:
            o_ref[0, 0] = _finish_compress(acc_a[...], acc_b[...], w2_ref[0])
        else:
            a_ref[0] = acc_a[...]
            b_ref[0] = acc_b[...]


def _segment_sums(rows3, pe_t, w1_bd, w2_bd):
    tiles, nseg, _ = rows3.shape
    finish = w2_bd is not None
    pe4 = pe_t.reshape(2, L_CMP, 1, KV_COLS)
    in_specs = [pl.BlockSpec((1, nseg, KV_COLS), lambda i, k, l: (i, 0, l * 4 + k)),
                pl.BlockSpec((1, 1, 1, KV_COLS), lambda i, k, l: (k, l, 0, 0)),
                pl.BlockSpec((1, 1, 1, KV_COLS), lambda i, k, l: (k, D_CMP + l, 0, 0)),
                pl.BlockSpec((1, 1, KV_COLS, KV_COLS), lambda i, k, l: (k, l, 0, 0)),
                pl.BlockSpec((1, 1, KV_COLS, KV_COLS), lambda i, k, l: (k, D_CMP + l, 0, 0))]
    args = [rows3, pe4, pe4, w1_bd, w1_bd]
    if finish:
        in_specs.append(pl.BlockSpec((1, KV_COLS, KV_COLS), lambda i, k, l: (k, 0, 0)))
        args.append(w2_bd)
        out_specs = pl.BlockSpec((1, 1, nseg, KV_COLS), lambda i, k, l: (i, k, 0, 0))
        out_shape = jax.ShapeDtypeStruct((tiles, 2, nseg, KV_COLS), F32)
    else:
        out_specs = [pl.BlockSpec((1, nseg, KV_COLS), lambda i, k, l: (i, 0, k))] * 2
        out_shape = [jax.ShapeDtypeStruct((tiles, nseg, 2 * KV_COLS), F32)] * 2
    return pl.pallas_call(
        functools.partial(_segment_sums_kernel, finish),
        grid=(tiles, 2, D_CMP),
        in_specs=in_specs,
        out_specs=out_specs,
        out_shape=out_shape,
        scratch_shapes=[pltpu.VMEM((nseg, KV_COLS), F32), pltpu.VMEM((nseg, KV_COLS), F32)],
        compiler_params=_cparams(("arbitrary", "arbitrary", "arbitrary")),
        name="segment_sums",
    )(*args)


def _nsa_prompt_kernel(nc, q_ref, gates_ref, cmp_ref, ksel_ref, vsel_ref, kwin_ref, vwin_ref,
                       expand_ref, pool_ref, slope_ref, o_ref):
    tq = Q_TILE
    rows = GROUP * tq
    t_total = ksel_ref.shape[1]
    ncp = cmp_ref.shape[2]
    qb = pl.program_id(1)
    q0 = qb * tq
    qf = q_ref[0] * (HEAD_DIM ** -0.5)
    gates = gates_ref[0]
    tpos1 = q0 + lax.broadcasted_iota(jnp.int32, (tq, 1), 0)
    tpos = jnp.concatenate([tpos1] * GROUP, axis=0)
    tposf = tpos.astype(F32)
    win_len = WINDOW + tq
    w0 = jnp.maximum(q0 - WINDOW, 0)
    n_sel_steps = (q0 + tq + SEL_CHUNK - 1) // SEL_CHUNK

    for g in range(N_KV_HEADS):
        piece = qf[:, g * KV_COLS:(g + 1) * KV_COLS]
        qpad = jnp.concatenate([_place_slot(piece, r, g) for r in range(GROUP)], axis=0).astype(BF16)
        slope = slope_ref[g]

        kc = cmp_ref[0, 0].astype(BF16)
        vc = cmp_ref[0, 1].astype(BF16)
        n_i = lax.broadcasted_iota(jnp.int32, (1, ncp), 1)
        c_end = jnp.where(n_i < nc, n_i * D_CMP + (L_CMP - 1), jnp.int32(2 ** 30))
        center = (n_i * D_CMP).astype(F32) + (L_CMP - 1) / 2
        s = _dot_nt(qpad, kc) - slope * (tposf - center)
        p_cmp = _masked_softmax(s, c_end <= tpos)
        o_cmp = _dot(p_cmp.astype(BF16), vc)

        imp = p_cmp[0:tq]
        for r in range(1, GROUP):
            imp = imp + p_cmp[r * tq:(r + 1) * tq]
        score = _selection_scores(_exact_pool(imp, pool_ref[...]), tpos1)
        sel = _topk_mask(score, TOP_N).astype(BF16)

        def sel_step(c, carry):
            m, l, acc = carry
            k0 = pl.multiple_of(c * SEL_CHUNK, SEL_CHUNK)
            k = ksel_ref[0, pl.ds(k0, SEL_CHUNK), :]
            v = vsel_ref[0, pl.ds(k0, SEL_CHUNK), :]
            kpos = k0 + lax.broadcasted_iota(jnp.int32, (1, SEL_CHUNK), 1)
            picked = _dot(sel, expand_ref[c])
            valid = (jnp.concatenate([picked] * GROUP, axis=0) > 0.5) & (kpos <= tpos)
            s = _dot_nt(qpad, k) - slope * (tposf - kpos.astype(F32))
            s = jnp.where(valid, s, NEG_INF)
            m_new = jnp.maximum(m, jnp.max(s, axis=-1, keepdims=True))
            alpha = jnp.exp(m - m_new)
            p = jnp.where(valid, jnp.exp(s - m_new), 0.0)
            l = alpha * l + jnp.sum(p, axis=-1, keepdims=True)
            acc = alpha * acc + _dot(p.astype(BF16), v)
            return m_new, l, acc

        m0 = jnp.full((rows, 1), NEG_INF, F32)
        l0 = jnp.zeros((rows, 1), F32)
        a0 = jnp.zeros((rows, KV_COLS), F32)
        _, l_sel, acc = lax.fori_loop(0, n_sel_steps, sel_step, (m0, l0, a0))
        o_sel = acc / jnp.where(l_sel > 0.0, l_sel, 1.0)

        w0a = pl.multiple_of(w0, tq)
        kw = kwin_ref[0, pl.ds(w0a, win_len), :]
        vw = vwin_ref[0, pl.ds(w0a, win_len), :]
        wd = tpos - (w0 + lax.broadcasted_iota(jnp.int32, (1, win_len), 1))
        s = _dot_nt(qpad, kw) - slope * wd.astype(F32)
        p_win = _masked_softmax(s, (wd >= 0) & (wd < WINDOW))
        o_win = _dot(p_win.astype(BF16), vw)

        chunk = jnp.zeros((tq, KV_COLS), F32)
        for r in range(GROUP):
            c0 = 3 * (g * GROUP + r)
            rs = slice(r * tq, (r + 1) * tq)
            o_r = (gates[:, c0:c0 + 1] * o_cmp[rs] + gates[:, c0 + 1:c0 + 2] * o_sel[rs]
                   + gates[:, c0 + 2:c0 + 3] * o_win[rs])
            chunk = chunk + _place_slot(o_r, g, r)
        o_ref[0, :, g * KV_COLS:(g + 1) * KV_COLS] = chunk.astype(o_ref.dtype)
    del t_total


def _nsa_prompt_attend(q, gates, cmp, ksel, vsel, kwin, vwin, expand, pool, slope_rows):
    bsz, t, d = q.shape
    nc = (t - L_CMP) // D_CMP + 1
    ncp = cmp.shape[2]
    seq = lambda n: pl.BlockSpec((1, t, n), lambda b, i: (b, 0, 0), pipeline_mode=pl.Buffered(1))
    return pl.pallas_call(
        functools.partial(_nsa_prompt_kernel, nc),
        grid=(bsz, t // Q_TILE),
        in_specs=[pl.BlockSpec((1, Q_TILE, d), lambda b, i: (b, i, 0)),
                  pl.BlockSpec((1, Q_TILE, LANES), lambda b, i: (b, i, 0)),
                  pl.BlockSpec((1, 2, ncp, KV_COLS), lambda b, i: (b, 0, 0, 0), pipeline_mode=pl.Buffered(1)),
                  seq(KV_COLS), seq(KV_COLS), seq(KV_COLS), seq(KV_COLS),
                  _const_spec(expand.shape), _const_spec(pool.shape), _const_spec(slope_rows.shape)],
        out_specs=pl.BlockSpec((1, Q_TILE, d), lambda b, i: (b, i, 0)),
        out_shape=jax.ShapeDtypeStruct((bsz, t, d), BF16),
        compiler_params=_cparams(("arbitrary", "arbitrary")),
        name="nsa_prompt_attend",
    )(q, gates, cmp, ksel, vsel, kwin, vwin, expand, pool, slope_rows)


def _sample_qpad(qf, tq):
    parts = []
    for g in range(N_KV_HEADS):
        piece = qf[:, g * KV_COLS:(g + 1) * KV_COLS]
        for r in range(GROUP):
            parts.append(_place_slot(piece, r, g))
    return jnp.concatenate(parts, axis=0).astype(BF16)


def _sample_rows_to_tokens(o, tq):
    chunks = []
    for g in range(N_KV_HEADS):
        chunk = jnp.zeros((tq, KV_COLS), F32)
        for r in range(GROUP):
            i0 = (g * GROUP + r) * tq
            chunk = chunk + _place_slot(o[i0:i0 + tq], g, r)
        chunks.append(chunk)
    return jnp.concatenate(chunks, axis=1)


def _nsa_sample_cmp_kernel(n_pages, past_len, tq, pt_ref, q_ref, w2_ref, pool_ref, slope_ref, *refs):
    a_refs = refs[:n_pages]
    b_refs = refs[n_pages:2 * n_pages]
    ocmp_ref, sel_ref = refs[2 * n_pages:]
    del pt_ref
    rows = N_HEADS * tq
    a = jnp.concatenate([r[0] for r in a_refs], axis=0)
    b = jnp.concatenate([r[0] for r in b_refs], axis=0)
    nseg = a.shape[0]
    kc = _finish_compress(a[:, :KV_COLS], b[:, :KV_COLS], w2_ref[0]).astype(BF16)
    vc = _finish_compress(a[:, KV_COLS:], b[:, KV_COLS:], w2_ref[1]).astype(BF16)

    qpad = _sample_qpad(q_ref[0] * (HEAD_DIM ** -0.5), tq)
    slope = slope_ref[...]
    rowi = lax.broadcasted_iota(jnp.int32, (rows, 1), 0)
    tpos = past_len + rowi % tq
    tposf = tpos.astype(F32)
    nc = (past_len + tq - L_CMP) // D_CMP + 1
    n_i = lax.broadcasted_iota(jnp.int32, (1, nseg), 1)
    c_end = jnp.where(n_i < nc, n_i * D_CMP + (L_CMP - 1), jnp.int32(2 ** 30))
    center = (n_i * D_CMP).astype(F32) + (L_CMP - 1) / 2
    s = _dot_nt(qpad, kc) - slope * (tposf - center)
    p_cmp = _masked_softmax(s, c_end <= tpos)
    ocmp_ref[0] = _dot(p_cmp.astype(BF16), vc)

    sels = []
    tpos1 = past_len + lax.broadcasted_iota(jnp.int32, (tq, 1), 0)
    for g in range(N_KV_HEADS):
        i0 = g * GROUP * tq
        imp = p_cmp[i0:i0 + tq]
        for r in range(1, GROUP):
            imp = imp + p_cmp[i0 + r * tq:i0 + (r + 1) * tq]
        score = _selection_scores(_exact_pool(imp, pool_ref[...]), tpos1)
        sel = _topk_mask(score, TOP_N)
        sels.extend([sel] * GROUP)
    sel_ref[0] = jnp.concatenate(sels, axis=0).astype(BF16)


def _nsa_sample_cmp(page_table, q, a_all, b_all, w2_bd, pool, slope_rows, page, past_len):
    bs, tq, d = q.shape
    n_pages = page_table.shape[1]
    seg_pp = page // D_CMP
    rows = N_HEADS * tq
    a3 = a_all.reshape(-1, seg_pp, 2 * KV_COLS)
    b3 = b_all.reshape(-1, seg_pp, 2 * KV_COLS)

    def page_spec(p):
        return pl.BlockSpec((1, seg_pp, 2 * KV_COLS), lambda s, pt: (pt[s, p], 0, 0))

    cst = lambda shape: pl.BlockSpec(shape, lambda s, pt: (0,) * len(shape))
    grid_spec = pltpu.PrefetchScalarGridSpec(
        num_scalar_prefetch=1,
        grid=(bs,),
        in_specs=[pl.BlockSpec((1, tq, d), lambda s, pt: (s, 0, 0)),
                  cst(w2_bd.shape), cst(pool.shape), cst(slope_rows.shape)]
                 + [page_spec(p) for p in range(n_pages)] * 2,
        out_specs=[pl.BlockSpec((1, rows, KV_COLS), lambda s, pt: (s, 0, 0)),
                   pl.BlockSpec((1, rows, LANES), lambda s, pt: (s, 0, 0))],
    )
    return pl.pallas_call(
        functools.partial(_nsa_sample_cmp_kernel, n_pages, past_len, tq),
        grid_spec=grid_spec,
        out_shape=[jax.ShapeDtypeStruct((bs, rows, KV_COLS), F32),
                   jax.ShapeDtypeStruct((bs, rows, LANES), BF16)],
        compiler_params=_cparams(("arbitrary",)),
        name="nsa_sample_cmp",
    )(page_table, q, w2_bd, pool, slope_rows, *([a3] * n_pages), *([b3] * n_pages))


def _nsa_sample_attend_kernel(n_pages, page, tq, pt_ref, q_ref, gates_ref, kvn_ref, winn_ref, winp_ref,
                              ocmp_ref, sel_ref, expand_ref, slope_ref, *refs):
    page_refs = refs[:n_pages]
    o_ref, newwin_ref = refs[n_pages:]
    del pt_ref
    rows = N_HEADS * tq
    past_len = n_pages * page
    wbuf = winp_ref.shape[1]
    qpad = _sample_qpad(q_ref[0] * (HEAD_DIM ** -0.5), tq)
    slope = slope_ref[...]
    rowi = lax.broadcasted_iota(jnp.int32, (rows, 1), 0)
    tpos = past_len + rowi % tq
    tposf = tpos.astype(F32)
    newpos = past_len + lax.broadcasted_iota(jnp.int32, (1, LANES), 1)
    pad_rows = lambda a: jnp.concatenate([a, jnp.zeros((LANES - tq, a.shape[1]), F32)], axis=0)

    kvn = pad_rows(kvn_ref[0])
    k_new = kvn[:, 2 * KV_COLS:3 * KV_COLS].astype(BF16)
    v_new = kvn[:, 3 * KV_COLS:4 * KV_COLS].astype(BF16)
    scores = [_dot_nt(qpad, r[0, 0, :, :KV_COLS].astype(BF16)) for r in page_refs]
    scores.append(_dot_nt(qpad, k_new))
    s = jnp.concatenate(scores, axis=1)
    nk = past_len + LANES
    kpos = lax.broadcasted_iota(jnp.int32, (1, nk), 1)
    picked = _dot(sel_ref[0], expand_ref[...]) > 0.5
    s = s - slope * (tposf - kpos.astype(F32))
    p = _masked_softmax(s, picked & (kpos <= tpos)).astype(BF16)
    o_sel = _dot(p[:, past_len:], v_new)
    for i, r in enumerate(page_refs):
        o_sel = o_sel + _dot(p[:, i * page:(i + 1) * page], r[0, 0, :, KV_COLS:].astype(BF16))

    winp = winp_ref[0]
    newwin_ref[0, 0:wbuf - tq, :] = winp[tq:, :]
    newwin_ref[0, wbuf - tq:wbuf, :] = winn_ref[0]
    winn = pad_rows(winn_ref[0])
    sp = _dot_nt(qpad, winp[:, :KV_COLS].astype(BF16))
    sn = _dot_nt(qpad, winn[:, :KV_COLS].astype(BF16))
    wpos_p = past_len - wbuf + lax.broadcasted_iota(jnp.int32, (1, wbuf), 1)
    wpos = jnp.concatenate([wpos_p, newpos], axis=1)
    wd = tpos - wpos
    s = jnp.concatenate([sp, sn], axis=1) - slope * wd.astype(F32)
    p = _masked_softmax(s, (wd >= 0) & (wd < WINDOW) & (wpos >= 0)).astype(BF16)
    o_win = _dot(p[:, :wbuf], winp[:, KV_COLS:].astype(BF16)) + _dot(p[:, wbuf:], winn[:, KV_COLS:].astype(BF16))

    gates = gates_ref[0]
    o_cmp = ocmp_ref[0]
    merged = []
    for h in range(N_HEADS):
        rs = slice(h * tq, (h + 1) * tq)
        merged.append(gates[:, 3 * h:3 * h + 1] * o_cmp[rs] + gates[:, 3 * h + 1:3 * h + 2] * o_sel[rs]
                      + gates[:, 3 * h + 2:3 * h + 3] * o_win[rs])
    o_ref[0] = _sample_rows_to_tokens(jnp.concatenate(merged, axis=0), tq).astype(o_ref.dtype)


def _nsa_sample_attend(page_table, cache_l, q, gates, kv_new, win_new, win_past, o_cmp, sel, expand, slope_rows):
    bs, tq, d = q.shape
    n_pages = page_table.shape[1]
    page = cache_l.shape[1]
    wbuf = win_past.shape[1]
    rows = N_HEADS * tq

    def page_spec(p):
        return pl.BlockSpec((1, 1, page, 2 * KV_COLS), lambda s, pt: (0, pt[s, p], 0, 1))

    per = lambda r, n: pl.BlockSpec((1, r, n), lambda s, pt: (s, 0, 0))
    cst = lambda shape: pl.BlockSpec(shape, lambda s, pt: (0,) * len(shape))
    grid_spec = pltpu.PrefetchScalarGridSpec(
        num_scalar_prefetch=1,
        grid=(bs,),
        in_specs=[per(tq, d), per(tq, LANES), per(tq, 4 * KV_COLS), per(tq, 2 * KV_COLS),
                  per(wbuf, 2 * KV_COLS), per(rows, KV_COLS), per(rows, LANES),
                  cst(expand.shape), cst(slope_rows.shape)]
                 + [page_spec(p) for p in range(n_pages)],
        out_specs=[per(tq, d), per(wbuf, 2 * KV_COLS)],
    )
    return pl.pallas_call(
        functools.partial(_nsa_sample_attend_kernel, n_pages, page, tq),
        grid_spec=grid_spec,
        out_shape=[jax.ShapeDtypeStruct((bs, tq, d), BF16),
                   jax.ShapeDtypeStruct((bs, wbuf, 2 * KV_COLS), F32)],
        compiler_params=_cparams(("arbitrary",)),
        name="nsa_sample_attend",
    )(page_table, q, gates, kv_new, win_new, win_past, o_cmp, sel, expand, slope_rows,
      *([cache_l[None]] * n_pages))


def _cmul_add(ar, ai, hr, hi, xr, xi):
    return ar * hr - ai * hi + xr, ar * hi + ai * hr + xi


def _s5_kernel(mode, tm, x_ref, h0r_ref, h0i_ref, gain_ref, bre_ref, bim_ref, are_ref, aim_ref,
               pwr_ref, pwi_ref, cre_ref, cim_ref, d_ref, wglu_ref,
               xo_ref, hfr_ref, hfi_ref, hr_s, hi_s, cr_s, ci_s):
    d = x_ref.shape[1]
    nstate = cr_s.shape[1]
    nblk = bre_ref.shape[0]
    cb = d // nblk
    sb = nstate // nblk
    tpb = sb // LANES
    x = x_ref[...]
    uf = _rms(x, gain_ref[...])
    ub = uf.astype(BF16)

    if mode == "carry":
        nj = tm // SUBLANES
        cs = nj + SUBLANES
        dst = [(s * cs, s * nj, nj) for s in range(SUBLANES)]
    else:
        dst = [(0, 0, tm)]
    for j in range(nblk):
        uj = ub[:, j * cb:(j + 1) * cb]
        br = _dot(uj, bre_ref[j])
        bi = _dot(uj, bim_ref[j])
        for k in range(tpb):
            for (o, i0, n) in dst:
                hr_s[j * tpb + k, o:o + n, :] = br[i0:i0 + n, k * LANES:(k + 1) * LANES]
                hi_s[j * tpb + k, o:o + n, :] = bi[i0:i0 + n, k * LANES:(k + 1) * LANES]

    lc = SSM_LANE_CHUNK
    par = lc // LANES
    if mode == "carry":
        @pl.when(pl.program_id(1) == 0)
        def _():
            cr_s[...] = jnp.broadcast_to(h0r_ref[0], cr_s.shape)
            ci_s[...] = jnp.broadcast_to(h0i_ref[0], ci_s.shape)

        sub = lax.broadcasted_iota(jnp.int32, (SUBLANES, 1), 0)
        for c in range(nstate // lc):
            sl = slice(c * lc, (c + 1) * lc)
            ar = jnp.broadcast_to(are_ref[:, sl], (SUBLANES, lc))
            ai = jnp.broadcast_to(aim_ref[:, sl], (SUBLANES, lc))

            tiles = range(c * par, (c + 1) * par)
            lane = lambda v, k: v[:, k * LANES:(k + 1) * LANES]

            def p1(jj, st):
                idx = pl.ds(jj, SUBLANES, stride=cs)
                out = []
                for k, q in enumerate(tiles):
                    nr, ni = _cmul_add(lane(ar, k), lane(ai, k), st[2 * k], st[2 * k + 1],
                                       hr_s[q, idx, :], hi_s[q, idx, :])
                    hr_s[q, idx, :] = nr
                    hi_s[q, idx, :] = ni
                    out += [nr, ni]
                return tuple(out)

            z = jnp.zeros((SUBLANES, LANES), F32)
            lf = lax.fori_loop(0, nj, p1, (z,) * (2 * par))
            lfr = jnp.concatenate(lf[0::2], axis=1)
            lfi = jnp.concatenate(lf[1::2], axis=1)

            alr = pwr_ref[nj - 1:nj, sl]
            ali = pwi_ref[nj - 1:nj, sl]
            inr = cr_s[0:1, sl]
            ini = ci_s[0:1, sl]
            hin_r = jnp.zeros((SUBLANES, lc), F32)
            hin_i = jnp.zeros((SUBLANES, lc), F32)
            for s in range(SUBLANES):
                hin_r = jnp.where(sub == s, inr, hin_r)
                hin_i = jnp.where(sub == s, ini, hin_i)
                inr, ini = _cmul_add(alr, ali, inr, ini, lfr[s:s + 1], lfi[s:s + 1])
            cr_s[:, sl] = jnp.broadcast_to(inr, (SUBLANES, lc))
            ci_s[:, sl] = jnp.broadcast_to(ini, (SUBLANES, lc))

            def p2(jj, carry):
                idx = pl.ds(jj, SUBLANES, stride=cs)
                pr = pwr_ref[pl.ds(jj, 1), sl]
                pi_ = pwi_ref[pl.ds(jj, 1), sl]
                for k, q in enumerate(tiles):
                    nr, ni = _cmul_add(lane(pr, k), lane(pi_, k), lane(hin_r, k), lane(hin_i, k),
                                       hr_s[q, idx, :], hi_s[q, idx, :])
                    hr_s[q, idx, :] = nr
                    hi_s[q, idx, :] = ni
                return carry

            lax.fori_loop(0, nj, p2, 0)

        @pl.when(pl.program_id(1) == pl.num_programs(1) - 1)
        def _():
            hfr_ref[0] = cr_s[0:1, :]
            hfi_ref[0] = ci_s[0:1, :]
    else:
        nseq = tm // mode
        for q in range(nstate // LANES):
            sl = slice(q * LANES, (q + 1) * LANES)
            ar = jnp.broadcast_to(are_ref[:, sl], (nseq, LANES))
            ai = jnp.broadcast_to(aim_ref[:, sl], (nseq, LANES))
            sr = h0r_ref[:, sl]
            si = h0i_ref[:, sl]
            for t in range(mode):
                idx = pl.ds(t, nseq, stride=mode)
                sr, si = _cmul_add(ar, ai, sr, si, hr_s[q, idx, :], hi_s[q, idx, :])
                hr_s[q, idx, :] = sr
                hi_s[q, idx, :] = si
            hfr_ref[:, sl] = sr
            hfi_ref[:, sl] = si

    def block_rows(ref, j):
        cols = [jnp.concatenate([ref[j * tpb + k, o:o + n, :] for (o, _, n) in dst], axis=0) for k in range(tpb)]
        return jnp.concatenate(cols, axis=1).astype(BF16)

    ys = []
    for j in range(nblk):
        ys.append(_dot(block_rows(hr_s, j), cre_ref[j]) - _dot(block_rows(hi_s, j), cim_ref[j]))
    y = jnp.concatenate(ys, axis=1) + d_ref[...] * uf
    ab = _dot(_gelu(y).astype(BF16), wglu_ref[...])
    xo_ref[...] = x + ab[:, :d] * _sigmoid(ab[:, d:])


def _s5_layer(x2, h0r, h0i, seq_len, n_seq, gain, prm):
    rows, d = x2.shape
    nstate = h0r.shape[1]
    if seq_len >= 256:
        mode, tm = "carry", 256
        nt = seq_len // tm
        grid = (n_seq, nt)
        xmap = lambda b, i: (b * nt + i, 0)
        h0_spec = pl.BlockSpec((1, 1, nstate), lambda b, i: (b, 0, 0))
        hf_spec = pl.BlockSpec((1, 1, nstate), lambda b, i: (b, 0, 0))
        h0r, h0i = h0r[:, None, :], h0i[:, None, :]
        hf_shape = (n_seq, 1, nstate)
        srows = (tm // SUBLANES + SUBLANES) * SUBLANES
        sem = ("arbitrary", "arbitrary")
        cst = lambda shape: pl.BlockSpec(shape, lambda b, i: (0,) * len(shape), pipeline_mode=pl.Buffered(1))
    else:
        mode = seq_len
        tm = min(256, rows)
        nseq_t = tm // seq_len
        grid = (rows // tm,)
        xmap = lambda i: (i, 0)
        h0_spec = pl.BlockSpec((nseq_t, nstate), lambda i: (i, 0))
        hf_spec = pl.BlockSpec((nseq_t, nstate), lambda i: (i, 0))
        hf_shape = (n_seq, nstate)
        srows = tm
        sem = ("arbitrary",)
        cst = lambda shape: pl.BlockSpec(shape, lambda i: (0,) * len(shape), pipeline_mode=pl.Buffered(1))
    pw_r, pw_i = prm["pw_r"], prm["pw_i"]
    consts = [gain, prm["b_r"], prm["b_i"], prm["a_r"], prm["a_i"], pw_r, pw_i, prm["c_r"], prm["c_i"],
              prm["d"], prm["w_glu"]]
    outs = pl.pallas_call(
        functools.partial(_s5_kernel, mode, tm),
        grid=grid,
        in_specs=[pl.BlockSpec((tm, d), xmap), h0_spec, h0_spec] + [cst(c.shape) for c in consts],
        out_specs=[pl.BlockSpec((tm, d), xmap), hf_spec, hf_spec],
        out_shape=[jax.ShapeDtypeStruct((rows, d), F32), jax.ShapeDtypeStruct(hf_shape, F32),
                   jax.ShapeDtypeStruct(hf_shape, F32)],
        scratch_shapes=[pltpu.VMEM((nstate // LANES, srows, LANES), F32),
                        pltpu.VMEM((nstate // LANES, srows, LANES), F32),
                        pltpu.VMEM((SUBLANES, nstate), F32), pltpu.VMEM((SUBLANES, nstate), F32)],
        compiler_params=_cparams(sem),
        name="s5_layer",
    )(x2, h0r, h0i, *consts)
    xo, hfr, hfi = outs
    return xo, hfr.reshape(n_seq, nstate), hfi.reshape(n_seq, nstate)


def _s5_params(a_re, a_im, log_dt, b_re, b_im, c_re, c_im, d_skip, w_glu, n_pow):
    g, p = a_re.shape
    ch = b_re.shape[2]
    lam = lax.complex(a_re.astype(F32), a_im.astype(F32))
    dt = jnp.exp(log_dt.astype(F32))[:, None]
    a_bar = jnp.exp(lam * dt)
    b_bar = ((a_bar - 1.0) / lam)[:, :, None] * lax.complex(b_re.astype(F32), b_im.astype(F32))
    nblk = g // SSM_GROUP_BLOCK
    eye = jnp.eye(SSM_GROUP_BLOCK, dtype=F32)

    def b_blocks(m):
        m = m.reshape(nblk, SSM_GROUP_BLOCK, p, ch)
        return jnp.einsum("jgpc,gh->jgchp", m, eye).reshape(nblk, SSM_GROUP_BLOCK * ch, SSM_GROUP_BLOCK * p).astype(BF16)

    def c_blocks(m):
        m = m.reshape(nblk, SSM_GROUP_BLOCK, ch, p)
        return jnp.einsum("jgcp,gh->jgphc", m, eye).reshape(nblk, SSM_GROUP_BLOCK * p, SSM_GROUP_BLOCK * ch).astype(BF16)

    pw = jnp.cumprod(jnp.broadcast_to(a_bar.reshape(1, g * p), (n_pow, g * p)), axis=0)
    return dict(b_r=b_blocks(b_bar.real), b_i=b_blocks(b_bar.imag),
                a_r=a_bar.real.reshape(1, g * p), a_i=a_bar.imag.reshape(1, g * p),
                pw_r=pw.real, pw_i=pw.imag,
                c_r=c_blocks(c_re.astype(F32)), c_i=c_blocks(c_im.astype(F32)),
                d=d_skip.astype(F32).reshape(1, -1), w_glu=w_glu.astype(BF16))


def _ffn_ple_kernel(mode, final_norm, x_ref, p_ref, b1_ref, b2_ref, gffn_ref, wup_ref, cw_ref, cb_ref,
                    wdown_ref, gple_ref, wproj_ref, wgate_ref, gfin_ref, xo_ref, tail_ref, carry_ref):
    tm = x_ref.shape[0]
    f = cw_ref.shape[1]
    x = x_ref[...]
    u = _rms(x, gffn_ref[...]).astype(BF16)
    hg = _dot(u, wup_ref[...])
    h = hg[:, :f]
    gate_branch = hg[:, f:]
    rowi = lax.broadcasted_iota(jnp.int32, (tm, 1), 0)
    r1 = pltpu.roll(h, 1, 0)
    r2 = pltpu.roll(h, 2, 0)
    if mode == "carry":
        @pl.when(pl.program_id(1) == 0)
        def _():
            carry_ref[SUBLANES - 2:SUBLANES, :] = b1_ref[0]
        c0 = carry_ref[SUBLANES - 2:SUBLANES - 1, :]
        c1 = carry_ref[SUBLANES - 1:SUBLANES, :]
        hm1 = jnp.where(rowi == 0, c1, r1)
        hm2 = jnp.where(rowi == 0, c0, jnp.where(rowi == 1, c1, r2))
        carry_ref[...] = h[tm - SUBLANES:tm, :]
        tail_ref[...] = h[tm - SUBLANES:tm, :]
    else:
        t = rowi % mode
        hm1 = jnp.where(t == 0, b1_ref[...], r1)
        hm2 = jnp.where(t < 2, b2_ref[...], r2)
        tail_ref[...] = h
    conv = cb_ref[...] + cw_ref[0:1, :] * hm2 + cw_ref[1:2, :] * hm1 + cw_ref[2:3, :] * h
    act = (_gelu(conv) * gate_branch).astype(BF16)
    x1 = x + _dot(act, wdown_ref[...])
    gate = _sigmoid(_dot(_rms(x1, gple_ref[...]).astype(BF16), wgate_ref[...]))
    x2 = x1 + _dot(p_ref[...].astype(BF16), wproj_ref[...]) * gate
    xo_ref[...] = _rms(x2, gfin_ref[...]) if final_norm else x2


def _ffn_ple(x2, p2, buf, seq_len, n_seq, final_norm, gffn, wup, cw, cb, wdown, gple, wproj, wgate, gfin):
    rows, d = x2.shape
    f = cw.shape[1]
    ple = p2.shape[1]
    consts = [gffn, wup, cw, cb, wdown, gple, wproj, wgate, gfin]
    if seq_len >= 256:
        mode, tm = "carry", 256
        nt = seq_len // tm
        grid = (n_seq, nt)
        rmap = lambda b, i: (b * nt + i, 0)
        b1, b2 = buf, buf
        bspec = pl.BlockSpec((1, CONV_W - 1, f), lambda b, i: (b, 0, 0))
        tail_rows = SUBLANES
        sem = ("arbitrary", "arbitrary")
        cst = lambda shape: pl.BlockSpec(shape, lambda b, i: (0,) * len(shape), pipeline_mode=pl.Buffered(1))
    else:
        mode = seq_len
        tm = min(256, rows)
        grid = (rows // tm,)
        rmap = lambda i: (i, 0)
        zero = jnp.zeros((n_seq, seq_len, f), F32)
        b1 = zero.at[:, 0].set(buf[:, 1]).reshape(rows, f)
        b2 = zero.at[:, 0].set(buf[:, 0]).at[:, 1].set(buf[:, 1]).reshape(rows, f)
        bspec = pl.BlockSpec((tm, f), rmap)
        tail_rows = tm
        sem = ("arbitrary",)
        cst = lambda shape: pl.BlockSpec(shape, lambda i: (0,) * len(shape), pipeline_mode=pl.Buffered(1))
    n_tiles = rows // tm
    xo, tail = pl.pallas_call(
        functools.partial(_ffn_ple_kernel, mode, final_norm),
        grid=grid,
        in_specs=[pl.BlockSpec((tm, d), rmap), pl.BlockSpec((tm, ple), rmap), bspec, bspec]
                 + [cst(c.shape) for c in consts],
        out_specs=[pl.BlockSpec((tm, d), rmap), pl.BlockSpec((tail_rows, f), rmap)],
        out_shape=[jax.ShapeDtypeStruct((rows, d), F32), jax.ShapeDtypeStruct((n_tiles * tail_rows, f), F32)],
        scratch_shapes=[pltpu.VMEM((SUBLANES, f), F32)],
        compiler_params=_cparams(sem),
        name="ffn_ple",
    )(x2, p2, b1, b2, *consts)
    if mode == "carry":
        new_buf = tail.reshape(n_seq, n_tiles // n_seq, SUBLANES, f)[:, -1, SUBLANES - (CONV_W - 1):, :]
    else:
        new_buf = tail.reshape(n_seq, seq_len, f)[:, seq_len - (CONV_W - 1):, :]
    return xo, new_buf


def _block_diag_heads(w):
    eye = jnp.eye(N_KV_HEADS, dtype=w.dtype)
    out = jnp.einsum("...de,gh->...gdhe", w, eye)
    return out.reshape(*w.shape[:-2], KV_COLS, KV_COLS)


def _slopes():
    return np.exp2(-8.0 * np.arange(1, N_HEADS + 1, dtype=np.float64) / N_HEADS).astype(np.float32)


def _expand_matrix(n_keys):
    return (np.arange(LANES)[:, None] == (np.arange(n_keys)[None, :] // L_SEL)).astype(np.float32)


def _pool_matrix(n_cmp):
    return (np.arange(n_cmp)[:, None] // SEL_PER_CMP == np.arange(LANES)[None, :]).astype(np.float32)


def _nsa_layer(xp, xs, cache_l, cache_win_l, page_table, gain, w_in, w_out, pe, w1, w2):
    bsz, t, d = xp.shape
    bs, tq, _ = xs.shape
    page = cache_l.shape[1]
    n_pages = page_table.shape[1]
    past_len = n_pages * page
    n_pool = cache_l.shape[0]
    assert t % SEL_CHUNK == 0 and t // L_SEL <= LANES and t >= WINDOW + Q_TILE
    assert (past_len + LANES) // L_SEL <= LANES and page % D_CMP == 0 and tq <= SUBLANES

    qcols = N_HEADS * HEAD_DIM
    nmain = qcols + 6 * KV_COLS
    w_main = w_in[:, :nmain].astype(BF16)
    w_gate = jnp.pad(w_in[:, nmain:], ((0, 0), (0, LANES - (w_in.shape[1] - nmain)))).astype(BF16)
    w_out_b = w_out.astype(BF16)
    pe_t = jnp.tile(pe, (1, 1, N_KV_HEADS))
    w1_bd = _block_diag_heads(w1).astype(BF16)
    w2_bd = _block_diag_heads(w2).astype(BF16)
    slopes = _slopes()

    q, kv, win, ksel, vsel, kwin, vwin, gates = _attn_in_proj(xp.reshape(bsz * t, d), gain, w_main, w_gate)
    sh = lambda a: a.reshape(bsz, t, a.shape[-1])
    kv3 = sh(kv)
    cmp = _segment_sums(kv.reshape(bsz, t // D_CMP, D_CMP * 4 * KV_COLS), pe_t, w1_bd, w2_bd)
    expand = jnp.asarray(_expand_matrix(t).reshape(LANES, t // SEL_CHUNK, SEL_CHUNK).transpose(1, 0, 2), BF16)
    pool = jnp.asarray(_pool_matrix(t // D_CMP), BF16)
    slope_rows = jnp.asarray(np.repeat(slopes.reshape(N_KV_HEADS, GROUP), Q_TILE, axis=1)[..., None])
    o = _nsa_prompt_attend(sh(q), sh(gates), cmp, sh(ksel), sh(vsel), sh(kwin), sh(vwin), expand, pool, slope_rows)
    xp_new = _matmul_res(o.reshape(bsz * t, d), w_out_b, xp.reshape(bsz * t, d)).reshape(bsz, t, d)
    keep = min(WINDOW, t)
    kv_p = kv3.reshape(bsz, t, 4, N_KV_HEADS, HEAD_DIM)
    win_p = sh(win)[:, t - keep:].reshape(bsz, keep, 2, N_KV_HEADS, HEAD_DIM)

    qs, kvs, wins, _, _, _, _, gates_s = _attn_in_proj(xs.reshape(bs * tq, d), gain, w_main, w_gate)
    shs = lambda a: a.reshape(bs, tq, a.shape[-1])
    pages_per_tile = math.gcd(n_pool, 128)
    a_all, b_all = _segment_sums(
        cache_l.reshape(n_pool // pages_per_tile, pages_per_tile * page // D_CMP, D_CMP * 4 * KV_COLS),
        pe_t, w1_bd, None)
    nseg_s = past_len // D_CMP
    pool_s = jnp.asarray(_pool_matrix(nseg_s), BF16)
    slope_s = jnp.asarray(np.repeat(slopes, tq)[:, None])
    o_cmp, sel = _nsa_sample_cmp(page_table, shs(qs), a_all, b_all, w2_bd, pool_s, slope_s, page, past_len)
    expand_s = jnp.asarray(_expand_matrix(past_len + LANES), BF16)
    wbuf = cache_win_l.shape[1]
    o_s, new_win = _nsa_sample_attend(page_table, cache_l.reshape(n_pool, page, 4 * KV_COLS), shs(qs), shs(gates_s),
                                      shs(kvs), shs(wins), cache_win_l.reshape(bs, wbuf, 2 * KV_COLS),
                                      o_cmp, sel, expand_s, slope_s)
    xs_new = _matmul_res(o_s.reshape(bs * tq, d), w_out_b, xs.reshape(bs * tq, d)).reshape(bs, tq, d)
    kv_s = shs(kvs).reshape(bs, tq, 4, N_KV_HEADS, HEAD_DIM)
    win_s = new_win.reshape(bs, wbuf, 2, N_KV_HEADS, HEAD_DIM)
    return xp_new, xs_new, kv_p, kv_s, win_p, win_s


def kernel(x_prompt, x_sample, cache_kv, cache_win, state_ssm_re, state_ssm_im, state_conv, page_table,
           p_prompt, p_sample, norm_mix, norm_ffn, norm_ple, norm_final, w_attn_in, w_attn_out,
           cmp_pe, cmp_w1, cmp_w2, ssm_a_re, ssm_a_im, ssm_log_dt, ssm_b_re, ssm_b_im, ssm_c_re, ssm_c_im,
           ssm_d, w_glu, w_ffn_up, ffn_conv_w, ffn_conv_b, w_ffn_down, w_ple_proj, w_ple_gate):
    bsz, t, d = x_prompt.shape
    bs, tq, _ = x_sample.shape
    depth = norm_mix.shape[0]
    f = ffn_conv_w.shape[2]
    g, p = ssm_a_re.shape[1:]
    xp, xs = x_prompt, x_sample
    row = lambda v: v.reshape(1, -1).astype(F32)
    kv_p, kv_s, win_p, win_s = [], [], [], []
    sre_p, sim_p, sre_s, sim_s = [], [], [], []
    cb_p, cb_s = [], []
    for i in range(depth):
        j = i // 2
        if i % 2 == 0:
            xp, xs, kvp, kvs, wp, ws = _nsa_layer(xp, xs, cache_kv[j], cache_win[j], page_table, row(norm_mix[i]),
                                                  w_attn_in[j], w_attn_out[j], cmp_pe[j], cmp_w1[j], cmp_w2[j])
            kv_p.append(kvp)
            kv_s.append(kvs)
            win_p.append(wp)
            win_s.append(ws)
        else:
            prm = _s5_params(ssm_a_re[j], ssm_a_im[j], ssm_log_dt[j], ssm_b_re[j], ssm_b_im[j], ssm_c_re[j],
                             ssm_c_im[j], ssm_d[j], w_glu[j], 256 // SUBLANES)
            zero = jnp.zeros((bsz, g * p), F32)
            xp2, hr, hi = _s5_layer(xp.reshape(bsz * t, d), zero, zero, t, bsz, row(norm_mix[i]), prm)
            xs2, hrs, his = _s5_layer(xs.reshape(bs * tq, d), state_ssm_re[j].reshape(bs, g * p).astype(F32),
                                      state_ssm_im[j].reshape(bs, g * p).astype(F32), tq, bs, row(norm_mix[i]), prm)
            xp, xs = xp2.reshape(bsz, t, d), xs2.reshape(bs, tq, d)
            sre_p.append(hr.reshape(bsz, g, p))
            sim_p.append(hi.reshape(bsz, g, p))
            sre_s.append(hrs.reshape(bs, g, p))
            sim_s.append(his.reshape(bs, g, p))
        last = i == depth - 1
        ffn_w = (row(norm_ffn[i]), w_ffn_up[i].astype(BF16), ffn_conv_w[i].astype(F32), row(ffn_conv_b[i]),
                 w_ffn_down[i].astype(BF16), row(norm_ple[i]), w_ple_proj[i].astype(BF16),
                 w_ple_gate[i].astype(BF16), row(norm_final))
        xp2, bp = _ffn_ple(xp.reshape(bsz * t, d), p_prompt[i].reshape(bsz * t, -1),
                           jnp.zeros((bsz, CONV_W - 1, f), F32), t, bsz, last, *ffn_w)
        xs2, bs_new = _ffn_ple(xs.reshape(bs * tq, d), p_sample[i].reshape(bs * tq, -1), state_conv[i].astype(F32),
                               tq, bs, last, *ffn_w)
        xp, xs = xp2.reshape(bsz, t, d), xs2.reshape(bs, tq, d)
        cb_p.append(bp)
        cb_s.append(bs_new)
    return (xp, xs, jnp.stack(kv_p), jnp.stack(kv_s), jnp.stack(win_p), jnp.stack(win_s),
            jnp.stack(sre_p), jnp.stack(sim_p), jnp.stack(sre_s), jnp.stack(sim_s),
            jnp.stack(cb_p), jnp.stack(cb_s))
```

```python
import functools
import math

import numpy as np
import jax
import jax.numpy as jnp
from jax import lax
from jax.experimental import pallas as pl
from jax.experimental.pallas import tpu as pltpu

F32 = jnp.float32
BF16 = jnp.bfloat16

N_HEADS = 16
HEAD_DIM = 64
N_KV_HEADS = 4
GROUP = N_HEADS // N_KV_HEADS
KV_COLS = N_KV_HEADS * HEAD_DIM
L_CMP = 32
D_CMP = 16
L_SEL = 64
TOP_N = 16
WINDOW = 512
SSM_CH = 16
STATE_P = 64
CONV_W = 3
NORM_EPS = 1e-6
NEG_INF = -1e30
FORCE_BONUS = 1e4
SEL_PER_CMP = L_SEL // D_CMP

LANES = 128
SUBLANES = 8
VMEM_LIMIT = 56 * 1024 * 1024

Q_TILE = 128
SEL_CHUNK = 512
SSM_LANE_CHUNK = 512
SSM_GROUP_BLOCK = 16


def _cparams(sem):
    return pltpu.CompilerParams(dimension_semantics=sem, vmem_limit_bytes=VMEM_LIMIT)


def _const_spec(shape):
    nd = len(shape)
    return pl.BlockSpec(shape, lambda *_: (0,) * nd, pipeline_mode=pl.Buffered(1))


def _rms(x, gain):
    y = x * lax.rsqrt(jnp.mean(x * x, axis=-1, keepdims=True) + NORM_EPS)
    return y * gain


def _gelu(x):
    c = math.sqrt(2.0 / math.pi)
    return x * (0.5 * (1.0 + jnp.tanh(c * (x + 0.044715 * (x * x * x)))))


def _sigmoid(x):
    return 1.0 / (1.0 + jnp.exp(-x))


def _dot(a, b):
    return jnp.dot(a, b, preferred_element_type=F32)


def _dot_nt(a, b):
    return lax.dot_general(a, b, (((1,), (1,)), ((), ())), preferred_element_type=F32)


def _masked_softmax(s, mask):
    sm = jnp.where(mask, s, NEG_INF)
    m = jnp.max(sm, axis=-1, keepdims=True)
    e = jnp.where(mask, jnp.exp(sm - m), 0.0)
    den = jnp.sum(e, axis=-1, keepdims=True)
    return e / jnp.where(den > 0.0, den, 1.0)


def _exact_pool(x, pool_bf):
    hi = x.astype(BF16)
    r1 = x - hi.astype(F32)
    mid = r1.astype(BF16)
    lo = (r1 - mid.astype(F32)).astype(BF16)
    return _dot(hi, pool_bf) + _dot(mid, pool_bf) + _dot(lo, pool_bf)


def _topk_mask(score, k):
    lane = lax.broadcasted_iota(jnp.int32, score.shape, 1).astype(F32)
    sel = jnp.zeros(score.shape, F32)
    s = score
    for _ in range(k):
        m = jnp.max(s, axis=-1, keepdims=True)
        idx = jnp.min(jnp.where(s == m, lane, float(LANES)), axis=-1, keepdims=True)
        hit = lane == idx
        sel = jnp.where(hit, 1.0, sel)
        s = jnp.where(hit, -jnp.inf, s)
    return sel


def _slot_ids():
    return lax.broadcasted_iota(jnp.int32, (1, KV_COLS), 1) // HEAD_DIM


def _place_slot(x, src_slot, dst_slot):
    shift = (HEAD_DIM * (dst_slot - src_slot)) % KV_COLS
    y = pltpu.roll(x, shift, 1) if shift else x
    return jnp.where(_slot_ids() == dst_slot, y, 0.0)


def _selection_scores(imp, tpos):
    blk = lax.broadcasted_iota(jnp.int32, imp.shape, 1)
    cur = tpos // L_SEL
    forced = (blk == 0) | (blk == cur) | (blk == cur - 1)
    started = blk * L_SEL <= tpos
    return jnp.where(started, imp + jnp.where(forced, FORCE_BONUS, 0.0), NEG_INF)


def _attn_in_kernel(x_ref, gain_ref, wm_ref, wg_ref,
                    q_ref, kv_ref, win_ref, ksel_ref, vsel_ref, kwin_ref, vwin_ref, gates_ref, *t_refs):
    d = x_ref.shape[1]
    u = _rms(x_ref[...], gain_ref[...]).astype(BF16)
    z = _dot(u, wm_ref[...])
    q_ref[...] = z[:, :d]
    kv = z[:, d:d + 4 * KV_COLS]
    kv_ref[...] = kv
    win = z[:, d + 4 * KV_COLS:d + 6 * KV_COLS]
    win_ref[...] = win
    ksel_ref[...] = kv[:, 2 * KV_COLS:3 * KV_COLS].astype(BF16)
    vsel_ref[...] = kv[:, 3 * KV_COLS:4 * KV_COLS].astype(BF16)
    kwin_ref[...] = win[:, :KV_COLS].astype(BF16)
    vwin_ref[...] = win[:, KV_COLS:].astype(BF16)
    gates_ref[...] = _sigmoid(_dot(u, wg_ref[...]))
    if t_refs:
        kvt_ref, wint_ref = t_refs
        kvt_ref[0] = kv.T
        wint_ref[0] = win.T


def _attn_in_proj(x2, gain, w_main, w_gate, seq_len=None):
    rows, d = x2.shape
    tm = min(512, rows)
    nmain = w_main.shape[1]
    row = lambda n: pl.BlockSpec((tm, n), lambda i: (i, 0))
    outs = [(d, F32), (4 * KV_COLS, F32), (2 * KV_COLS, F32), (KV_COLS, BF16), (KV_COLS, BF16),
            (KV_COLS, BF16), (KV_COLS, BF16), (LANES, F32)]
    out_specs = [row(n) for n, _ in outs]
    out_shape = [jax.ShapeDtypeStruct((rows, n), dt) for n, dt in outs]
    if seq_len is not None:
        nt = seq_len // tm
        for n in (4 * KV_COLS, 2 * KV_COLS):
            out_specs.append(pl.BlockSpec((1, n, tm), lambda i: (i // nt, 0, i % nt)))
            out_shape.append(jax.ShapeDtypeStruct((rows // seq_len, n, seq_len), F32))
    return pl.pallas_call(
        _attn_in_kernel,
        grid=(rows // tm,),
        in_specs=[row(d), _const_spec((1, d)), _const_spec((d, nmain)), _const_spec((d, LANES))],
        out_specs=out_specs,
        out_shape=out_shape,
        compiler_params=_cparams(("arbitrary",)),
        name="attn_in_proj",
    )(x2, gain, w_main, w_gate)


def _matmul_res_kernel(a_ref, w_ref, x_ref, o_ref):
    o_ref[...] = x_ref[...] + _dot(a_ref[...].astype(BF16), w_ref[...])


def _matmul_res(a, w, x):
    rows, k = a.shape
    n = w.shape[1]
    tm = min(512, rows)
    return pl.pallas_call(
        _matmul_res_kernel,
        grid=(rows // tm,),
        in_specs=[pl.BlockSpec((tm, k), lambda i: (i, 0)), _const_spec((k, n)),
                  pl.BlockSpec((tm, n), lambda i: (i, 0))],
        out_specs=pl.BlockSpec((tm, n), lambda i: (i, 0)),
        out_shape=jax.ShapeDtypeStruct((rows, n), F32),
        compiler_params=_cparams(("arbitrary",)),
        name="matmul_res",
    )(a, w, x)


def _finish_compress(a, b, w2):
    nseg = a.shape[0]
    pre = a + pltpu.roll(b, nseg - 1, 0)
    out = _dot(_gelu(pre).astype(BF16), w2)
    rowi = lax.broadcasted_iota(jnp.int32, (nseg, 1), 0)
    return jnp.where(rowi < nseg - 1, out, 0.0)


def _segment_sums_kernel(finish, x_ref, pea_ref, peb_ref, w1a_ref, w1b_ref, *refs):
    if finish:
        w2_ref, o_ref, acc_a, acc_b = refs
    else:
        a_ref, b_ref, acc_a, acc_b = refs
    l = pl.program_id(2)

    @pl.when(l == 0)
    def _():
        acc_a[...] = jnp.zeros(acc_a.shape, F32)
        acc_b[...] = jnp.zeros(acc_b.shape, F32)

    xs = x_ref[0]
    acc_a[...] += _dot((xs + pea_ref[0, 0]).astype(BF16), w1a_ref[0, 0])
    acc_b[...] += _dot((xs + peb_ref[0, 0]).astype(BF16), w1b_ref[0, 0])

    @pl.when(l == D_CMP - 1)
    def _():
        if finish:
            o_ref[0, 0] = _finish_compress(acc_a[...], acc_b[...], w2_ref[0])
        else:
            a_ref[0] = acc_a[...]
            b_ref[0] = acc_b[...]


def _segment_sums(rows3, pe_t, w1_bd, w2_bd):
    tiles, nseg, _ = rows3.shape
    finish = w2_bd is not None
    pe4 = pe_t.reshape(2, L_CMP, 1, KV_COLS)
    in_specs = [pl.BlockSpec((1, nseg, KV_COLS), lambda i, k, l: (i, 0, l * 4 + k)),
                pl.BlockSpec((1, 1, 1, KV_COLS), lambda i, k, l: (k, l, 0, 0)),
                pl.BlockSpec((1, 1, 1, KV_COLS), lambda i, k, l: (k, D_CMP + l, 0, 0)),
                pl.BlockSpec((1, 1, KV_COLS, KV_COLS), lambda i, k, l: (k, l, 0, 0)),
                pl.BlockSpec((1, 1, KV_COLS, KV_COLS), lambda i, k, l: (k, D_CMP + l, 0, 0))]
    args = [rows3, pe4, pe4, w1_bd, w1_bd]
    if finish:
        in_specs.append(pl.BlockSpec((1, KV_COLS, KV_COLS), lambda i, k, l: (k, 0, 0)))
        args.append(w2_bd)
        out_specs = pl.BlockSpec((1, 1, nseg, KV_COLS), lambda i, k, l: (i, k, 0, 0))
        out_shape = jax.ShapeDtypeStruct((tiles, 2, nseg, KV_COLS), F32)
    else:
        out_specs = [pl.BlockSpec((1, nseg, KV_COLS), lambda i, k, l: (i, 0, k))] * 2
        out_shape = [jax.ShapeDtypeStruct((tiles, nseg, 2 * KV_COLS), F32)] * 2
    return pl.pallas_call(
        functools.partial(_segment_sums_kernel, finish),
        grid=(tiles, 2, D_CMP),
        in_specs=in_specs,
        out_specs=out_specs,
        out_shape=out_shape,
        scratch_shapes=[pltpu.VMEM((nseg, KV_COLS), F32), pltpu.VMEM((nseg, KV_COLS), F32)],
        compiler_params=_cparams(("arbitrary", "arbitrary", "arbitrary")),
        name="segment_sums",
    )(*args)


def _pool_sums_kernel(pp, page, x_ref, pe_ref, w1_ref, a_ref, b_ref, rows_s):
    ftiles = 2 * KV_COLS // LANES

    def to_rows(p, carry):
        xt = x_ref[p]
        r0 = pl.multiple_of(p * page, page)
        for c in range(ftiles):
            rows_s[c, pl.ds(r0, page), :] = xt[c * LANES:(c + 1) * LANES, :].T
        return carry

    lax.fori_loop(0, pp, to_rows, 0)
    nseg = pp * page // D_CMP
    tpk = KV_COLS // LANES
    for kind in range(2):
        a = jnp.zeros((nseg, KV_COLS), F32)
        b = jnp.zeros((nseg, KV_COLS), F32)
        for l in range(D_CMP):
            xs = jnp.concatenate([rows_s[kind * tpk + c, pl.ds(l, nseg, stride=D_CMP), :] for c in range(tpk)],
                                 axis=1)
            a = a + _dot((xs + pe_ref[kind, l:l + 1, :]).astype(BF16), w1_ref[kind, l])
            b = b + _dot((xs + pe_ref[kind, D_CMP + l:D_CMP + l + 1, :]).astype(BF16), w1_ref[kind, D_CMP + l])
        a_ref[:, kind * KV_COLS:(kind + 1) * KV_COLS] = a
        b_ref[:, kind * KV_COLS:(kind + 1) * KV_COLS] = b


def _pool_segment_sums(cache_t, pe_t, w1_bd, pp):
    n_pool, _, page = cache_t.shape
    assert page == LANES and n_pool % pp == 0
    nseg = pp * page // D_CMP
    return pl.pallas_call(
        functools.partial(_pool_sums_kernel, pp, page),
        grid=(n_pool // pp,),
        in_specs=[pl.BlockSpec((pp, 2 * KV_COLS, page), lambda i: (i, 0, 0)),
                  _const_spec(pe_t.shape), _const_spec(w1_bd.shape)],
        out_specs=[pl.BlockSpec((nseg, 2 * KV_COLS), lambda i: (i, 0))] * 2,
        out_shape=[jax.ShapeDtypeStruct((n_pool * page // D_CMP, 2 * KV_COLS), F32)] * 2,
        scratch_shapes=[pltpu.VMEM((2 * KV_COLS // LANES, pp * page, LANES), F32)],
        compiler_params=_cparams(("arbitrary",)),
        name="pool_segment_sums",
    )(cache_t, pe_t, w1_bd)


def _nsa_prompt_kernel(nc, q_ref, gates_ref, cmp_ref, ksel_ref, vsel_ref, kwin_ref, vwin_ref,
                       expand_ref, pool_ref, slope_ref, o_ref):
    tq = Q_TILE
    rows = GROUP * tq
    t_total = ksel_ref.shape[1]
    ncp = cmp_ref.shape[2]
    qb = pl.program_id(1)
    q0 = qb * tq
    qf = q_ref[0] * (HEAD_DIM ** -0.5)
    gates = gates_ref[0]
    tpos1 = q0 + lax.broadcasted_iota(jnp.int32, (tq, 1), 0)
    tpos = jnp.concatenate([tpos1] * GROUP, axis=0)
    tposf = tpos.astype(F32)
    win_len = WINDOW + tq
    w0 = jnp.maximum(q0 - WINDOW, 0)
    n_sel_steps = (q0 + tq + SEL_CHUNK - 1) // SEL_CHUNK

    for g in range(N_KV_HEADS):
        piece = qf[:, g * KV_COLS:(g + 1) * KV_COLS]
        qpad = jnp.concatenate([_place_slot(piece, r, g) for r in range(GROUP)], axis=0).astype(BF16)
        slope = slope_ref[g]

        kc = cmp_ref[0, 0].astype(BF16)
        vc = cmp_ref[0, 1].astype(BF16)
        n_i = lax.broadcasted_iota(jnp.int32, (1, ncp), 1)
        c_end = jnp.where(n_i < nc, n_i * D_CMP + (L_CMP - 1), jnp.int32(2 ** 30))
        center = (n_i * D_CMP).astype(F32) + (L_CMP - 1) / 2
        s = _dot_nt(qpad, kc) - slope * (tposf - center)
        p_cmp = _masked_softmax(s, c_end <= tpos)
        o_cmp = _dot(p_cmp.astype(BF16), vc)

        imp = p_cmp[0:tq]
        for r in range(1, GROUP):
            imp = imp + p_cmp[r * tq:(r + 1) * tq]
        score = _selection_scores(_exact_pool(imp, pool_ref[...]), tpos1)
        sel = _topk_mask(score, TOP_N).astype(BF16)

        def sel_step(c, carry):
            m, l, acc = carry
            k0 = pl.multiple_of(c * SEL_CHUNK, SEL_CHUNK)
            k = ksel_ref[0, pl.ds(k0, SEL_CHUNK), :]
            v = vsel_ref[0, pl.ds(k0, SEL_CHUNK), :]
            kpos = k0 + lax.broadcasted_iota(jnp.int32, (1, SEL_CHUNK), 1)
            picked = _dot(sel, expand_ref[c])
            valid = (jnp.concatenate([picked] * GROUP, axis=0) > 0.5) & (kpos <= tpos)
            s = _dot_nt(qpad, k) - slope * (tposf - kpos.astype(F32))
            s = jnp.where(valid, s, NEG_INF)
            m_new = jnp.maximum(m, jnp.max(s, axis=-1, keepdims=True))
            alpha = jnp.exp(m - m_new)
            p = jnp.where(valid, jnp.exp(s - m_new), 0.0)
            l = alpha * l + jnp.sum(p, axis=-1, keepdims=True)
            acc = alpha * acc + _dot(p.astype(BF16), v)
            return m_new, l, acc

        m0 = jnp.full((rows, 1), NEG_INF, F32)
        l0 = jnp.zeros((rows, 1), F32)
        a0 = jnp.zeros((rows, KV_COLS), F32)
        _, l_sel, acc = lax.fori_loop(0, n_sel_steps, sel_step, (m0, l0, a0))
        o_sel = acc / jnp.where(l_sel > 0.0, l_sel, 1.0)

        w0a = pl.multiple_of(w0, tq)
        kw = kwin_ref[0, pl.ds(w0a, win_len), :]
        vw = vwin_ref[0, pl.ds(w0a, win_len), :]
        wd = tpos - (w0 + lax.broadcasted_iota(jnp.int32, (1, win_len), 1))
        s = _dot_nt(qpad, kw) - slope * wd.astype(F32)
        p_win = _masked_softmax(s, (wd >= 0) & (wd < WINDOW))
        o_win = _dot(p_win.astype(BF16), vw)

        chunk = jnp.zeros((tq, KV_COLS), F32)
        for r in range(GROUP):
            c0 = 3 * (g * GROUP + r)
            rs = slice(r * tq, (r + 1) * tq)
            o_r = (gates[:, c0:c0 + 1] * o_cmp[rs] + gates[:, c0 + 1:c0 + 2] * o_sel[rs]
                   + gates[:, c0 + 2:c0 + 3] * o_win[rs])
            chunk = chunk + _place_slot(o_r, g, r)
        o_ref[0, :, g * KV_COLS:(g + 1) * KV_COLS] = chunk.astype(o_ref.dtype)
    del t_total


def _nsa_prompt_attend(q, gates, cmp, ksel, vsel, kwin, vwin, expand, pool, slope_rows):
    bsz, t, d = q.shape
    nc = (t - L_CMP) // D_CMP + 1
    ncp = cmp.shape[2]
    seq = lambda n: pl.BlockSpec((1, t, n), lambda b, i: (b, 0, 0), pipeline_mode=pl.Buffered(1))
    return pl.pallas_call(
        functools.partial(_nsa_prompt_kernel, nc),
        grid=(bsz, t // Q_TILE),
        in_specs=[pl.BlockSpec((1, Q_TILE, d), lambda b, i: (b, i, 0)),
                  pl.BlockSpec((1, Q_TILE, LANES), lambda b, i: (b, i, 0)),
                  pl.BlockSpec((1, 2, ncp, KV_COLS), lambda b, i: (b, 0, 0, 0), pipeline_mode=pl.Buffered(1)),
                  seq(KV_COLS), seq(KV_COLS), seq(KV_COLS), seq(KV_COLS),
                  _const_spec(expand.shape), _const_spec(pool.shape), _const_spec(slope_rows.shape)],
        out_specs=pl.BlockSpec((1, Q_TILE, d), lambda b, i: (b, i, 0)),
        out_shape=jax.ShapeDtypeStruct((bsz, t, d), BF16),
        compiler_params=_cparams(("arbitrary", "arbitrary")),
        name="nsa_prompt_attend",
    )(q, gates, cmp, ksel, vsel, kwin, vwin, expand, pool, slope_rows)


def _sample_qpad(qf, tq):
    parts = []
    for g in range(N_KV_HEADS):
        piece = qf[:, g * KV_COLS:(g + 1) * KV_COLS]
        for r in range(GROUP):
            parts.append(_place_slot(piece, r, g))
    return jnp.concatenate(parts, axis=0).astype(BF16)


def _sample_rows_to_tokens(o, tq):
    chunks = []
    for g in range(N_KV_HEADS):
        chunk = jnp.zeros((tq, KV_COLS), F32)
        for r in range(GROUP):
            i0 = (g * GROUP + r) * tq
            chunk = chunk + _place_slot(o[i0:i0 + tq], g, r)
        chunks.append(chunk)
    return jnp.concatenate(chunks, axis=1)


def _nsa_sample_cmp_kernel(n_pages, past_len, tq, pt_ref, q_ref, w2_ref, pool_ref, slope_ref, *refs):
    a_refs = refs[:n_pages]
    b_refs = refs[n_pages:2 * n_pages]
    ocmp_ref, sel_ref = refs[2 * n_pages:]
    del pt_ref
    rows = N_HEADS * tq
    a = jnp.concatenate([r[0] for r in a_refs], axis=0)
    b = jnp.concatenate([r[0] for r in b_refs], axis=0)
    nseg = a.shape[0]
    kc = _finish_compress(a[:, :KV_COLS], b[:, :KV_COLS], w2_ref[0]).astype(BF16)
    vc = _finish_compress(a[:, KV_COLS:], b[:, KV_COLS:], w2_ref[1]).astype(BF16)

    qpad = _sample_qpad(q_ref[0] * (HEAD_DIM ** -0.5), tq)
    slope = slope_ref[...]
    rowi = lax.broadcasted_iota(jnp.int32, (rows, 1), 0)
    tpos = past_len + rowi % tq
    tposf = tpos.astype(F32)
    nc = (past_len + tq - L_CMP) // D_CMP + 1
    n_i = lax.broadcasted_iota(jnp.int32, (1, nseg), 1)
    c_end = jnp.where(n_i < nc, n_i * D_CMP + (L_CMP - 1), jnp.int32(2 ** 30))
    center = (n_i * D_CMP).astype(F32) + (L_CMP - 1) / 2
    s = _dot_nt(qpad, kc) - slope * (tposf - center)
    p_cmp = _masked_softmax(s, c_end <= tpos)
    ocmp_ref[0] = _dot(p_cmp.astype(BF16), vc)

    sels = []
    tpos1 = past_len + lax.broadcasted_iota(jnp.int32, (tq, 1), 0)
    for g in range(N_KV_HEADS):
        i0 = g * GROUP * tq
        imp = p_cmp[i0:i0 + tq]
        for r in range(1, GROUP):
            imp = imp + p_cmp[i0 + r * tq:i0 + (r + 1) * tq]
        score = _selection_scores(_exact_pool(imp, pool_ref[...]), tpos1)
        sel = _topk_mask(score, TOP_N)
        sels.extend([sel] * GROUP)
    sel_ref[0] = jnp.concatenate(sels, axis=0).astype(BF16)


def _nsa_sample_cmp(page_table, q, a_all, b_all, w2_bd, pool, slope_rows, page, past_len):
    bs, tq, d = q.shape
    n_pages = page_table.shape[1]
    seg_pp = page // D_CMP
    rows = N_HEADS * tq
    a3 = a_all.reshape(-1, seg_pp, 2 * KV_COLS)
    b3 = b_all.reshape(-1, seg_pp, 2 * KV_COLS)

    def page_spec(p):
        return pl.BlockSpec((1, seg_pp, 2 * KV_COLS), lambda s, pt: (pt[s, p], 0, 0))

    cst = lambda shape: pl.BlockSpec(shape, lambda s, pt: (0,) * len(shape))
    grid_spec = pltpu.PrefetchScalarGridSpec(
        num_scalar_prefetch=1,
        grid=(bs,),
        in_specs=[pl.BlockSpec((1, tq, d), lambda s, pt: (s, 0, 0)),
                  cst(w2_bd.shape), cst(pool.shape), cst(slope_rows.shape)]
                 + [page_spec(p) for p in range(n_pages)] * 2,
        out_specs=[pl.BlockSpec((1, rows, KV_COLS), lambda s, pt: (s, 0, 0)),
                   pl.BlockSpec((1, rows, LANES), lambda s, pt: (s, 0, 0))],
    )
    return pl.pallas_call(
        functools.partial(_nsa_sample_cmp_kernel, n_pages, past_len, tq),
        grid_spec=grid_spec,
        out_shape=[jax.ShapeDtypeStruct((bs, rows, KV_COLS), F32),
                   jax.ShapeDtypeStruct((bs, rows, LANES), BF16)],
        compiler_params=_cparams(("arbitrary",)),
        name="nsa_sample_cmp",
    )(page_table, q, w2_bd, pool, slope_rows, *([a3] * n_pages), *([b3] * n_pages))


def _nsa_sample_attend_kernel(n_pages, page, tq, pt_ref, q_ref, gates_ref, kvn_ref, winn_ref, winp_ref,
                              ocmp_ref, sel_ref, expand_ref, slope_ref, *refs):
    page_refs = refs[:n_pages]
    o_ref, newwin_ref = refs[n_pages:]
    del pt_ref
    rows = N_HEADS * tq
    past_len = n_pages * page
    wbuf = winp_ref.shape[2]
    qpad = _sample_qpad(q_ref[0] * (HEAD_DIM ** -0.5), tq)
    slope = slope_ref[...]
    rowi = lax.broadcasted_iota(jnp.int32, (rows, 1), 0)
    tpos = past_len + rowi % tq
    tposf = tpos.astype(F32)
    newpos = past_len + lax.broadcasted_iota(jnp.int32, (1, LANES), 1)
    pad_rows = lambda a: jnp.concatenate([a, jnp.zeros((LANES - tq, a.shape[1]), F32)], axis=0)

    kvn = pad_rows(kvn_ref[0])
    k_new = kvn[:, 2 * KV_COLS:3 * KV_COLS].astype(BF16)
    v_new = kvn[:, 3 * KV_COLS:4 * KV_COLS].astype(BF16)
    scores = [_dot(qpad, r[0, :KV_COLS, :].astype(BF16)) for r in page_refs]
    scores.append(_dot_nt(qpad, k_new))
    s = jnp.concatenate(scores, axis=1)
    nk = past_len + LANES
    kpos = lax.broadcasted_iota(jnp.int32, (1, nk), 1)
    picked = _dot(sel_ref[0], expand_ref[...]) > 0.5
    s = s - slope * (tposf - kpos.astype(F32))
    p = _masked_softmax(s, picked & (kpos <= tpos)).astype(BF16)
    o_sel = _dot(p[:, past_len:], v_new)
    for i, r in enumerate(page_refs):
        o_sel = o_sel + _dot_nt(p[:, i * page:(i + 1) * page], r[0, KV_COLS:, :].astype(BF16))

    winp = winp_ref[0]
    winn = pad_rows(winn_ref[0])
    winn_t = pltpu.roll(winn.T, LANES - tq, 1)
    lane_w = lax.broadcasted_iota(jnp.int32, (1, wbuf), 1)
    newwin_ref[0] = jnp.where(lane_w >= wbuf - tq, jnp.concatenate([winn_t] * (wbuf // LANES), axis=1),
                              pltpu.roll(winp, wbuf - tq, 1))
    sp = _dot(qpad, winp[:KV_COLS, :].astype(BF16))
    sn = _dot_nt(qpad, winn[:, :KV_COLS].astype(BF16))
    wpos_p = past_len - wbuf + lax.broadcasted_iota(jnp.int32, (1, wbuf), 1)
    wpos = jnp.concatenate([wpos_p, newpos], axis=1)
    wd = tpos - wpos
    s = jnp.concatenate([sp, sn], axis=1) - slope * wd.astype(F32)
    p = _masked_softmax(s, (wd >= 0) & (wd < WINDOW) & (wpos >= 0)).astype(BF16)
    o_win = _dot_nt(p[:, :wbuf], winp[KV_COLS:, :].astype(BF16)) + _dot(p[:, wbuf:], winn[:, KV_COLS:].astype(BF16))

    gates = gates_ref[0]
    o_cmp = ocmp_ref[0]
    merged = []
    for h in range(N_HEADS):
        rs = slice(h * tq, (h + 1) * tq)
        merged.append(gates[:, 3 * h:3 * h + 1] * o_cmp[rs] + gates[:, 3 * h + 1:3 * h + 2] * o_sel[rs]
                      + gates[:, 3 * h + 2:3 * h + 3] * o_win[rs])
    o_ref[0] = _sample_rows_to_tokens(jnp.concatenate(merged, axis=0), tq).astype(o_ref.dtype)


def _nsa_sample_attend(page_table, cache_t, q, gates, kv_new, win_new, win_past_t, o_cmp, sel, expand, slope_rows):
    bs, tq, d = q.shape
    n_pages = page_table.shape[1]
    page = cache_t.shape[2]
    wbuf = win_past_t.shape[2]
    rows = N_HEADS * tq

    def page_spec(p):
        return pl.BlockSpec((1, 2 * KV_COLS, page), lambda s, pt: (pt[s, p], 1, 0))

    per = lambda r, n: pl.BlockSpec((1, r, n), lambda s, pt: (s, 0, 0))
    cst = lambda shape: pl.BlockSpec(shape, lambda s, pt: (0,) * len(shape))
    grid_spec = pltpu.PrefetchScalarGridSpec(
        num_scalar_prefetch=1,
        grid=(bs,),
        in_specs=[per(tq, d), per(tq, LANES), per(tq, 4 * KV_COLS), per(tq, 2 * KV_COLS),
                  per(2 * KV_COLS, wbuf), per(rows, KV_COLS), per(rows, LANES),
                  cst(expand.shape), cst(slope_rows.shape)]
                 + [page_spec(p) for p in range(n_pages)],
        out_specs=[per(tq, d), per(2 * KV_COLS, wbuf)],
    )
    return pl.pallas_call(
        functools.partial(_nsa_sample_attend_kernel, n_pages, page, tq),
        grid_spec=grid_spec,
        out_shape=[jax.ShapeDtypeStruct((bs, tq, d), BF16),
                   jax.ShapeDtypeStruct((bs, 2 * KV_COLS, wbuf), F32)],
        compiler_params=_cparams(("arbitrary",)),
        name="nsa_sample_attend",
    )(page_table, q, gates, kv_new, win_new, win_past_t, o_cmp, sel, expand, slope_rows,
      *([cache_t] * n_pages))


def _cmul_add(ar, ai, hr, hi, xr, xi):
    return ar * hr - ai * hi + xr, ar * hi + ai * hr + xi


def _s5_kernel(mode, tm, x_ref, h0r_ref, h0i_ref, gain_ref, bre_ref, bim_ref, are_ref, aim_ref,
               pwr_ref, pwi_ref, cre_ref, cim_ref, d_ref, wglu_ref,
               xo_ref, hfr_ref, hfi_ref, hr_s, hi_s, cr_s, ci_s):
    d = x_ref.shape[1]
    nstate = cr_s.shape[1]
    nblk = bre_ref.shape[0]
    cb = d // nblk
    sb = nstate // nblk
    tpb = sb // LANES
    x = x_ref[...]
    uf = _rms(x, gain_ref[...])
    ub = uf.astype(BF16)

    if mode == "carry":
        nj = tm // SUBLANES
        cs = nj + SUBLANES
        dst = [(s * cs, s * nj, nj) for s in range(SUBLANES)]
    else:
        dst = [(0, 0, tm)]
    for j in range(nblk):
        uj = ub[:, j * cb:(j + 1) * cb]
        br = _dot(uj, bre_ref[j])
        bi = _dot(uj, bim_ref[j])
        for k in range(tpb):
            for (o, i0, n) in dst:
                hr_s[j * tpb + k, o:o + n, :] = br[i0:i0 + n, k * LANES:(k + 1) * LANES]
                hi_s[j * tpb + k, o:o + n, :] = bi[i0:i0 + n, k * LANES:(k + 1) * LANES]

    lc = SSM_LANE_CHUNK
    par = lc // LANES
    if mode == "carry":
        @pl.when(pl.program_id(1) == 0)
        def _():
            cr_s[...] = jnp.broadcast_to(h0r_ref[0], cr_s.shape)
            ci_s[...] = jnp.broadcast_to(h0i_ref[0], ci_s.shape)

        sub = lax.broadcasted_iota(jnp.int32, (SUBLANES, 1), 0)
        for c in range(nstate // lc):
            sl = slice(c * lc, (c + 1) * lc)
            ar = jnp.broadcast_to(are_ref[:, sl], (SUBLANES, lc))
            ai = jnp.broadcast_to(aim_ref[:, sl], (SUBLANES, lc))

            tiles = range(c * par, (c + 1) * par)
            lane = lambda v, k: v[:, k * LANES:(k + 1) * LANES]

            def p1(jj, st):
                idx = pl.ds(jj, SUBLANES, stride=cs)
                out = []
                for k, q in enumerate(tiles):
                    nr, ni = _cmul_add(lane(ar, k), lane(ai, k), st[2 * k], st[2 * k + 1],
                                       hr_s[q, idx, :], hi_s[q, idx, :])
                    hr_s[q, idx, :] = nr
                    hi_s[q, idx, :] = ni
                    out += [nr, ni]
                return tuple(out)

            z = jnp.zeros((SUBLANES, LANES), F32)
            lf = lax.fori_loop(0, nj, p1, (z,) * (2 * par))
            lfr = jnp.concatenate(lf[0::2], axis=1)
            lfi = jnp.concatenate(lf[1::2], axis=1)

            alr = pwr_ref[nj - 1:nj, sl]
            ali = pwi_ref[nj - 1:nj, sl]
            inr = cr_s[0:1, sl]
            ini = ci_s[0:1, sl]
            hin_r = jnp.zeros((SUBLANES, lc), F32)
            hin_i = jnp.zeros((SUBLANES, lc), F32)
            for s in range(SUBLANES):
                hin_r = jnp.where(sub == s, inr, hin_r)
                hin_i = jnp.where(sub == s, ini, hin_i)
                inr, ini = _cmul_add(alr, ali, inr, ini, lfr[s:s + 1], lfi[s:s + 1])
            cr_s[:, sl] = jnp.broadcast_to(inr, (SUBLANES, lc))
            ci_s[:, sl] = jnp.broadcast_to(ini, (SUBLANES, lc))

            def p2(jj, carry):
                idx = pl.ds(jj, SUBLANES, stride=cs)
                pr = pwr_ref[pl.ds(jj, 1), sl]
                pi_ = pwi_ref[pl.ds(jj, 1), sl]
                for k, q in enumerate(tiles):
                    nr, ni = _cmul_add(lane(pr, k), lane(pi_, k), lane(hin_r, k), lane(hin_i, k),
                                       hr_s[q, idx, :], hi_s[q, idx, :])
                    hr_s[q, idx, :] = nr
                    hi_s[q, idx, :] = ni
                return carry

            lax.fori_loop(0, nj, p2, 0)

        @pl.when(pl.program_id(1) == pl.num_programs(1) - 1)
        def _():
            hfr_ref[0] = cr_s[0:1, :]
            hfi_ref[0] = ci_s[0:1, :]
    else:
        nseq = tm // mode
        for q in range(nstate // LANES):
            sl = slice(q * LANES, (q + 1) * LANES)
            ar = jnp.broadcast_to(are_ref[:, sl], (nseq, LANES))
            ai = jnp.broadcast_to(aim_ref[:, sl], (nseq, LANES))
            sr = h0r_ref[:, sl]
            si = h0i_ref[:, sl]
            for t in range(mode):
                idx = pl.ds(t, nseq, stride=mode)
                sr, si = _cmul_add(ar, ai, sr, si, hr_s[q, idx, :], hi_s[q, idx, :])
                hr_s[q, idx, :] = sr
                hi_s[q, idx, :] = si
            hfr_ref[:, sl] = sr
            hfi_ref[:, sl] = si

    def block_rows(ref, j):
        cols = [jnp.concatenate([ref[j * tpb + k, o:o + n, :] for (o, _, n) in dst], axis=0) for k in range(tpb)]
        return jnp.concatenate(cols, axis=1).astype(BF16)

    ys = []
    for j in range(nblk):
        ys.append(_dot(block_rows(hr_s, j), cre_ref[j]) - _dot(block_rows(hi_s, j), cim_ref[j]))
    y = jnp.concatenate(ys, axis=1) + d_ref[...] * uf
    ab = _dot(_gelu(y).astype(BF16), wglu_ref[...])
    xo_ref[...] = x + ab[:, :d] * _sigmoid(ab[:, d:])


def _s5_layer(x2, h0r, h0i, seq_len, n_seq, gain, prm):
    rows, d = x2.shape
    nstate = h0r.shape[1]
    if seq_len >= 256:
        mode, tm = "carry", 256
        nt = seq_len // tm
        grid = (n_seq, nt)
        xmap = lambda b, i: (b * nt + i, 0)
        h0_spec = pl.BlockSpec((1, 1, nstate), lambda b, i: (b, 0, 0))
        hf_spec = pl.BlockSpec((1, 1, nstate), lambda b, i: (b, 0, 0))
        h0r, h0i = h0r[:, None, :], h0i[:, None, :]
        hf_shape = (n_seq, 1, nstate)
        srows = (tm // SUBLANES + SUBLANES) * SUBLANES
        sem = ("arbitrary", "arbitrary")
        cst = lambda shape: pl.BlockSpec(shape, lambda b, i: (0,) * len(shape), pipeline_mode=pl.Buffered(1))
    else:
        mode = seq_len
        tm = min(256, rows)
        nseq_t = tm // seq_len
        grid = (rows // tm,)
        xmap = lambda i: (i, 0)
        h0_spec = pl.BlockSpec((nseq_t, nstate), lambda i: (i, 0))
        hf_spec = pl.BlockSpec((nseq_t, nstate), lambda i: (i, 0))
        hf_shape = (n_seq, nstate)
        srows = tm
        sem = ("arbitrary",)
        cst = lambda shape: pl.BlockSpec(shape, lambda i: (0,) * len(shape), pipeline_mode=pl.Buffered(1))
    pw_r, pw_i = prm["pw_r"], prm["pw_i"]
    consts = [gain, prm["b_r"], prm["b_i"], prm["a_r"], prm["a_i"], pw_r, pw_i, prm["c_r"], prm["c_i"],
              prm["d"], prm["w_glu"]]
    outs = pl.pallas_call(
        functools.partial(_s5_kernel, mode, tm),
        grid=grid,
        in_specs=[pl.BlockSpec((tm, d), xmap), h0_spec, h0_spec] + [cst(c.shape) for c in consts],
        out_specs=[pl.BlockSpec((tm, d), xmap), hf_spec, hf_spec],
        out_shape=[jax.ShapeDtypeStruct((rows, d), F32), jax.ShapeDtypeStruct(hf_shape, F32),
                   jax.ShapeDtypeStruct(hf_shape, F32)],
        scratch_shapes=[pltpu.VMEM((nstate // LANES, srows, LANES), F32),
                        pltpu.VMEM((nstate // LANES, srows, LANES), F32),
                        pltpu.VMEM((SUBLANES, nstate), F32), pltpu.VMEM((SUBLANES, nstate), F32)],
        compiler_params=_cparams(sem),
        name="s5_layer",
    )(x2, h0r, h0i, *consts)
    xo, hfr, hfi = outs
    return xo, hfr.reshape(n_seq, nstate), hfi.reshape(n_seq, nstate)


def _s5_params(a_re, a_im, log_dt, b_re, b_im, c_re, c_im, d_skip, w_glu, n_pow):
    g, p = a_re.shape
    ch = b_re.shape[2]
    lr, li = a_re.astype(F32), a_im.astype(F32)
    dt = jnp.exp(log_dt.astype(F32))[:, None]
    mag = jnp.exp(lr * dt)
    ar, ai = mag * jnp.cos(li * dt), mag * jnp.sin(li * dt)
    den = lr * lr + li * li
    kr = ((ar - 1.0) * lr + ai * li) / den
    ki = (ai * lr - (ar - 1.0) * li) / den
    br, bi = b_re.astype(F32), b_im.astype(F32)
    bbar_r = kr[:, :, None] * br - ki[:, :, None] * bi
    bbar_i = kr[:, :, None] * bi + ki[:, :, None] * br
    nblk = g // SSM_GROUP_BLOCK
    eye = jnp.eye(SSM_GROUP_BLOCK, dtype=F32)

    def b_blocks(m):
        m = m.reshape(nblk, SSM_GROUP_BLOCK, p, ch)
        return jnp.einsum("jgpc,gh->jgchp", m, eye).reshape(nblk, SSM_GROUP_BLOCK * ch, SSM_GROUP_BLOCK * p).astype(BF16)

    def c_blocks(m):
        m = m.reshape(nblk, SSM_GROUP_BLOCK, ch, p)
        return jnp.einsum("jgcp,gh->jgphc", m, eye).reshape(nblk, SSM_GROUP_BLOCK * p, SSM_GROUP_BLOCK * ch).astype(BF16)

    ar, ai = ar.reshape(1, g * p), ai.reshape(1, g * p)
    pw_r, pw_i = [ar], [ai]
    for _ in range(n_pow - 1):
        pw_r, pw_i = pw_r + [pw_r[-1] * ar - pw_i[-1] * ai], pw_i + [pw_r[-1] * ai + pw_i[-1] * ar]
    return dict(b_r=b_blocks(bbar_r), b_i=b_blocks(bbar_i), a_r=ar, a_i=ai,
                pw_r=jnp.concatenate(pw_r, axis=0), pw_i=jnp.concatenate(pw_i, axis=0),
                c_r=c_blocks(c_re.astype(F32)), c_i=c_blocks(c_im.astype(F32)),
                d=d_skip.astype(F32).reshape(1, -1), w_glu=w_glu.astype(BF16))


def _ffn_ple_kernel(mode, final_norm, x_ref, p_ref, b1_ref, b2_ref, gffn_ref, wup_ref, cw_ref, cb_ref,
                    wdown_ref, gple_ref, wproj_ref, wgate_ref, gfin_ref, xo_ref, tail_ref, carry_ref):
    tm = x_ref.shape[0]
    f = cw_ref.shape[1]
    x = x_ref[...]
    u = _rms(x, gffn_ref[...]).astype(BF16)
    hg = _dot(u, wup_ref[...])
    h = hg[:, :f]
    gate_branch = hg[:, f:]
    rowi = lax.broadcasted_iota(jnp.int32, (tm, 1), 0)
    r1 = pltpu.roll(h, 1, 0)
    r2 = pltpu.roll(h, 2, 0)
    if mode == "carry":
        @pl.when(pl.program_id(1) == 0)
        def _():
            carry_ref[SUBLANES - 2:SUBLANES, :] = b1_ref[0]
        c0 = carry_ref[SUBLANES - 2:SUBLANES - 1, :]
        c1 = carry_ref[SUBLANES - 1:SUBLANES, :]
        hm1 = jnp.where(rowi == 0, c1, r1)
        hm2 = jnp.where(rowi == 0, c0, jnp.where(rowi == 1, c1, r2))
        carry_ref[...] = h[tm - SUBLANES:tm, :]
        tail_ref[...] = h[tm - SUBLANES:tm, :]
    else:
        t = rowi % mode
        hm1 = jnp.where(t == 0, b1_ref[...], r1)
        hm2 = jnp.where(t < 2, b2_ref[...], r2)
        tail_ref[...] = h
    conv = cb_ref[...] + cw_ref[0:1, :] * hm2 + cw_ref[1:2, :] * hm1 + cw_ref[2:3, :] * h
    act = (_gelu(conv) * gate_branch).astype(BF16)
    x1 = x + _dot(act, wdown_ref[...])
    gate = _sigmoid(_dot(_rms(x1, gple_ref[...]).astype(BF16), wgate_ref[...]))
    x2 = x1 + _dot(p_ref[...].astype(BF16), wproj_ref[...]) * gate
    xo_ref[...] = _rms(x2, gfin_ref[...]) if final_norm else x2


def _ffn_ple(x2, p2, buf, seq_len, n_seq, final_norm, gffn, wup, cw, cb, wdown, gple, wproj, wgate, gfin):
    rows, d = x2.shape
    f = cw.shape[1]
    ple = p2.shape[1]
    consts = [gffn, wup, cw, cb, wdown, gple, wproj, wgate, gfin]
    if seq_len >= 256:
        mode, tm = "carry", 256
        nt = seq_len // tm
        grid = (n_seq, nt)
        rmap = lambda b, i: (b * nt + i, 0)
        b1, b2 = buf, buf
        bspec = pl.BlockSpec((1, CONV_W - 1, f), lambda b, i: (b, 0, 0))
        tail_rows = SUBLANES
        sem = ("arbitrary", "arbitrary")
        cst = lambda shape: pl.BlockSpec(shape, lambda b, i: (0,) * len(shape), pipeline_mode=pl.Buffered(1))
    else:
        mode = seq_len
        tm = min(256, rows)
        grid = (rows // tm,)
        rmap = lambda i: (i, 0)
        zero = jnp.zeros((n_seq, seq_len, f), F32)
        b1 = zero.at[:, 0].set(buf[:, 1]).reshape(rows, f)
        b2 = zero.at[:, 0].set(buf[:, 0]).at[:, 1].set(buf[:, 1]).reshape(rows, f)
        bspec = pl.BlockSpec((tm, f), rmap)
        tail_rows = tm
        sem = ("arbitrary",)
        cst = lambda shape: pl.BlockSpec(shape, lambda i: (0,) * len(shape), pipeline_mode=pl.Buffered(1))
    n_tiles = rows // tm
    xo, tail = pl.pallas_call(
        functools.partial(_ffn_ple_kernel, mode, final_norm),
        grid=grid,
        in_specs=[pl.BlockSpec((tm, d), rmap), pl.BlockSpec((tm, ple), rmap), bspec, bspec]
                 + [cst(c.shape) for c in consts],
        out_specs=[pl.BlockSpec((tm, d), rmap), pl.BlockSpec((tail_rows, f), rmap)],
        out_shape=[jax.ShapeDtypeStruct((rows, d), F32), jax.ShapeDtypeStruct((n_tiles * tail_rows, f), F32)],
        scratch_shapes=[pltpu.VMEM((SUBLANES, f), F32)],
        compiler_params=_cparams(sem),
        name="ffn_ple",
    )(x2, p2, b1, b2, *consts)
    if mode == "carry":
        new_buf = tail.reshape(n_seq, n_tiles // n_seq, SUBLANES, f)[:, -1, SUBLANES - (CONV_W - 1):, :]
    else:
        new_buf = tail.reshape(n_seq, seq_len, f)[:, seq_len - (CONV_W - 1):, :]
    return xo, new_buf


def _block_diag_heads(w):
    eye = jnp.eye(N_KV_HEADS, dtype=w.dtype)
    out = jnp.einsum("...de,gh->...gdhe", w, eye)
    return out.reshape(*w.shape[:-2], KV_COLS, KV_COLS)


def _slopes():
    return np.exp2(-8.0 * np.arange(1, N_HEADS + 1, dtype=np.float64) / N_HEADS).astype(np.float32)


def _expand_matrix(n_keys):
    return (np.arange(LANES)[:, None] == (np.arange(n_keys)[None, :] // L_SEL)).astype(np.float32)


def _pool_matrix(n_cmp):
    return (np.arange(n_cmp)[:, None] // SEL_PER_CMP == np.arange(LANES)[None, :]).astype(np.float32)


def _nsa_layer(xp, xs, cache_l, cache_win_l, page_table, gain, w_in, w_out, pe, w1, w2):
    bsz, t, d = xp.shape
    bs, tq, _ = xs.shape
    page = cache_l.shape[1]
    n_pages = page_table.shape[1]
    past_len = n_pages * page
    n_pool = cache_l.shape[0]
    assert t % SEL_CHUNK == 0 and t // L_SEL <= LANES and t >= WINDOW + Q_TILE
    assert (past_len + LANES) // L_SEL <= LANES and page % D_CMP == 0 and tq <= SUBLANES

    qcols = N_HEADS * HEAD_DIM
    nmain = qcols + 6 * KV_COLS
    w_main = w_in[:, :nmain].astype(BF16)
    w_gate = jnp.pad(w_in[:, nmain:], ((0, 0), (0, LANES - (w_in.shape[1] - nmain)))).astype(BF16)
    w_out_b = w_out.astype(BF16)
    pe_t = jnp.tile(pe, (1, 1, N_KV_HEADS))
    w1_bd = _block_diag_heads(w1).astype(BF16)
    w2_bd = _block_diag_heads(w2).astype(BF16)
    slopes = _slopes()

    q, kv, win, ksel, vsel, kwin, vwin, gates, kv_t, win_t = _attn_in_proj(
        xp.reshape(bsz * t, d), gain, w_main, w_gate, seq_len=t)
    sh = lambda a: a.reshape(bsz, t, a.shape[-1])
    cmp = _segment_sums(kv.reshape(bsz, t // D_CMP, D_CMP * 4 * KV_COLS), pe_t, w1_bd, w2_bd)
    expand = jnp.asarray(_expand_matrix(t).reshape(LANES, t // SEL_CHUNK, SEL_CHUNK).transpose(1, 0, 2), BF16)
    pool = jnp.asarray(_pool_matrix(t // D_CMP), BF16)
    slope_rows = jnp.asarray(np.repeat(slopes.reshape(N_KV_HEADS, GROUP), Q_TILE, axis=1)[..., None])
    o = _nsa_prompt_attend(sh(q), sh(gates), cmp, sh(ksel), sh(vsel), sh(kwin), sh(vwin), expand, pool, slope_rows)
    xp_new = _matmul_res(o.reshape(bsz * t, d), w_out_b, xp.reshape(bsz * t, d)).reshape(bsz, t, d)
    keep = min(WINDOW, t)
    from_t = lambda a, n: jnp.transpose(a.reshape(a.shape[0], n, N_KV_HEADS, HEAD_DIM, a.shape[-1]), (0, 4, 1, 2, 3))
    kv_p = from_t(kv_t, 4)
    win_p = from_t(win_t[:, :, t - keep:], 2)

    cache_t = jnp.transpose(cache_l, (0, 2, 3, 4, 1)).reshape(n_pool, 4 * KV_COLS, page)
    wbuf = cache_win_l.shape[1]
    win_past_t = jnp.transpose(cache_win_l, (0, 2, 3, 4, 1)).reshape(bs, 2 * KV_COLS, wbuf)
    qs, kvs, wins, _, _, _, _, gates_s = _attn_in_proj(xs.reshape(bs * tq, d), gain, w_main, w_gate)
    shs = lambda a: a.reshape(bs, tq, a.shape[-1])
    a_all, b_all = _pool_segment_sums(cache_t, pe_t, w1_bd, math.gcd(n_pool, 32))
    nseg_s = past_len // D_CMP
    pool_s = jnp.asarray(_pool_matrix(nseg_s), BF16)
    slope_s = jnp.asarray(np.repeat(slopes, tq)[:, None])
    o_cmp, sel = _nsa_sample_cmp(page_table, shs(qs), a_all, b_all, w2_bd, pool_s, slope_s, page, past_len)
    expand_s = jnp.asarray(_expand_matrix(past_len + LANES), BF16)
    o_s, new_win_t = _nsa_sample_attend(page_table, cache_t, shs(qs), shs(gates_s), shs(kvs), shs(wins), win_past_t,
                                        o_cmp, sel, expand_s, slope_s)
    xs_new = _matmul_res(o_s.reshape(bs * tq, d), w_out_b, xs.reshape(bs * tq, d)).reshape(bs, tq, d)
    kv_s = shs(kvs).reshape(bs, tq, 4, N_KV_HEADS, HEAD_DIM)
    win_s = from_t(new_win_t, 2)
    return xp_new, xs_new, kv_p, kv_s, win_p, win_s


def kernel(x_prompt, x_sample, cache_kv, cache_win, state_ssm_re, state_ssm_im, state_conv, page_table,
           p_prompt, p_sample, norm_mix, norm_ffn, norm_ple, norm_final, w_attn_in, w_attn_out,
           cmp_pe, cmp_w1, cmp_w2, ssm_a_re, ssm_a_im, ssm_log_dt, ssm_b_re, ssm_b_im, ssm_c_re, ssm_c_im,
           ssm_d, w_glu, w_ffn_up, ffn_conv_w, ffn_conv_b, w_ffn_down, w_ple_proj, w_ple_gate):
    bsz, t, d = x_prompt.shape
    bs, tq, _ = x_sample.shape
    depth = norm_mix.shape[0]
    f = ffn_conv_w.shape[2]
    g, p = ssm_a_re.shape[1:]
    xp, xs = x_prompt, x_sample
    row = lambda v: v.reshape(1, -1).astype(F32)
    kv_p, kv_s, win_p, win_s = [], [], [], []
    sre_p, sim_p, sre_s, sim_s = [], [], [], []
    cb_p, cb_s = [], []
    for i in range(depth):
        j = i // 2
        if i % 2 == 0:
            xp, xs, kvp, kvs, wp, ws = _nsa_layer(xp, xs, cache_kv[j], cache_win[j], page_table, row(norm_mix[i]),
                                                  w_attn_in[j], w_attn_out[j], cmp_pe[j], cmp_w1[j], cmp_w2[j])
            kv_p.append(kvp)
            kv_s.append(kvs)
            win_p.append(wp)
            win_s.append(ws)
        else:
            prm = _s5_params(ssm_a_re[j], ssm_a_im[j], ssm_log_dt[j], ssm_b_re[j], ssm_b_im[j], ssm_c_re[j],
                             ssm_c_im[j], ssm_d[j], w_glu[j], 256 // SUBLANES)
            zero = jnp.zeros((bsz, g * p), F32)
            xp2, hr, hi = _s5_layer(xp.reshape(bsz * t, d), zero, zero, t, bsz, row(norm_mix[i]), prm)
            xs2, hrs, his = _s5_layer(xs.reshape(bs * tq, d), state_ssm_re[j].reshape(bs, g * p).astype(F32),
                                      state_ssm_im[j].reshape(bs, g * p).astype(F32), tq, bs, row(norm_mix[i]), prm)
            xp, xs = xp2.reshape(bsz, t, d), xs2.reshape(bs, tq, d)
            sre_p.append(hr.reshape(bsz, g, p))
            sim_p.append(hi.reshape(bsz, g, p))
            sre_s.append(hrs.reshape(bs, g, p))
            sim_s.append(his.reshape(bs, g, p))
        last = i == depth - 1
        ffn_w = (row(norm_ffn[i]), w_ffn_up[i].astype(BF16), ffn_conv_w[i].astype(F32), row(ffn_conv_b[i]),
                 w_ffn_down[i].astype(BF16), row(norm_ple[i]), w_ple_proj[i].astype(BF16),
                 w_ple_gate[i].astype(BF16), row(norm_final))
        xp2, bp = _ffn_ple(xp.reshape(bsz * t, d), p_prompt[i].reshape(bsz * t, -1),
                           jnp.zeros((bsz, CONV_W - 1, f), F32), t, bsz, last, *ffn_w)
        xs2, bs_new = _ffn_ple(xs.reshape(bs * tq, d), p_sample[i].reshape(bs * tq, -1), state_conv[i].astype(F32),
                               tq, bs, last, *ffn_w)
        xp, xs = xp2.reshape(bsz, t, d), xs2.reshape(bs, tq, d)
        cb_p.append(bp)
        cb_s.append(bs_new)
    return (xp, xs, jnp.stack(kv_p), jnp.stack(kv_s), jnp.stack(win_p), jnp.stack(win_s),
            jnp.stack(sre_p), jnp.stack(sim_p), jnp.stack(sre_s), jnp.stack(sim_s),
            jnp.stack(cb_p), jnp.stack(cb_s))
```

```python
import functools
import math

import numpy as np
import jax
import jax.numpy as jnp
from jax import lax
from jax.experimental import pallas as pl
from jax.experimental.pallas import tpu as pltpu

F32 = jnp.float32
BF16 = jnp.bfloat16

N_HEADS = 16
HEAD_DIM = 64
N_KV_HEADS = 4
GROUP = N_HEADS // N_KV_HEADS
KV_COLS = N_KV_HEADS * HEAD_DIM
L_CMP = 32
D_CMP = 16
L_SEL = 64
TOP_N = 16
WINDOW = 512
SSM_CH = 16
STATE_P = 64
CONV_W = 3
NORM_EPS = 1e-6
NEG_INF = -1e30
FORCE_BONUS = 1e4
SEL_PER_CMP = L_SEL // D_CMP

LANES = 128
SUBLANES = 8
VMEM_LIMIT = 56 * 1024 * 1024

Q_TILE = 128
SEL_CHUNK = 512
SSM_LANE_CHUNK = 512
SSM_GROUP_BLOCK = 16


def _cparams(sem):
    return pltpu.CompilerParams(dimension_semantics=sem, vmem_limit_bytes=VMEM_LIMIT)


def _const_spec(shape):
    nd = len(shape)
    return pl.BlockSpec(shape, lambda *_: (0,) * nd, pipeline_mode=pl.Buffered(1))


def _rms(x, gain):
    y = x * lax.rsqrt(jnp.mean(x * x, axis=-1, keepdims=True) + NORM_EPS)
    return y * gain


def _gelu(x):
    c = math.sqrt(2.0 / math.pi)
    return x * (0.5 * (1.0 + jnp.tanh(c * (x + 0.044715 * (x * x * x)))))


def _sigmoid(x):
    return 1.0 / (1.0 + jnp.exp(-x))


def _dot(a, b):
    return jnp.dot(a, b, preferred_element_type=F32)


def _dot_nt(a, b):
    return lax.dot_general(a, b, (((1,), (1,)), ((), ())), preferred_element_type=F32)


def _masked_softmax(s, mask):
    sm = jnp.where(mask, s, NEG_INF)
    m = jnp.max(sm, axis=-1, keepdims=True)
    e = jnp.where(mask, jnp.exp(sm - m), 0.0)
    den = jnp.sum(e, axis=-1, keepdims=True)
    return e / jnp.where(den > 0.0, den, 1.0)


def _exact_pool(x, pool_bf):
    hi = x.astype(BF16)
    r1 = x - hi.astype(F32)
    mid = r1.astype(BF16)
    lo = (r1 - mid.astype(F32)).astype(BF16)
    return _dot(hi, pool_bf) + _dot(mid, pool_bf) + _dot(lo, pool_bf)


def _topk_mask(score, k):
    lane = lax.broadcasted_iota(jnp.int32, score.shape, 1).astype(F32)
    sel = jnp.zeros(score.shape, F32)
    s = score
    for _ in range(k):
        m = jnp.max(s, axis=-1, keepdims=True)
        idx = jnp.min(jnp.where(s == m, lane, float(LANES)), axis=-1, keepdims=True)
        hit = lane == idx
        sel = jnp.where(hit, 1.0, sel)
        s = jnp.where(hit, -jnp.inf, s)
    return sel


def _slot_ids():
    return lax.broadcasted_iota(jnp.int32, (1, KV_COLS), 1) // HEAD_DIM


def _place_slot(x, src_slot, dst_slot):
    shift = (HEAD_DIM * (dst_slot - src_slot)) % KV_COLS
    y = pltpu.roll(x, shift, 1) if shift else x
    return jnp.where(_slot_ids() == dst_slot, y, 0.0)


def _selection_scores(imp, tpos):
    blk = lax.broadcasted_iota(jnp.int32, imp.shape, 1)
    cur = tpos // L_SEL
    forced = (blk == 0) | (blk == cur) | (blk == cur - 1)
    started = blk * L_SEL <= tpos
    return jnp.where(started, imp + jnp.where(forced, FORCE_BONUS, 0.0), NEG_INF)


def _attn_in_kernel(seq_len, x_ref, gain_ref, wm_ref, wg_ref, q_ref, kv_ref, gates_ref, *refs):
    tm, d = x_ref.shape
    u = _rms(x_ref[...], gain_ref[...]).astype(BF16)
    z = _dot(u, wm_ref[...])
    q_ref[...] = z[:, :d]
    kv = z[:, d:d + 4 * KV_COLS]
    kv_ref[...] = kv
    win = z[:, d + 4 * KV_COLS:d + 6 * KV_COLS]
    gates_ref[...] = _sigmoid(_dot(u, wg_ref[...]))
    if seq_len is None:
        (win_ref,) = refs
        win_ref[...] = win
        return
    ksel_ref, vsel_ref, kwin_ref, vwin_ref, kvt_ref, wint_ref = refs
    pos = (pl.program_id(0) % (seq_len // tm)) * tm + lax.broadcasted_iota(jnp.int32, (tm, 1), 0)
    aug = _position_aug(((pos // LANES) * LANES).astype(F32), (pos % LANES).astype(F32))
    lane = lax.broadcasted_iota(jnp.int32, (1, LANES), 1)
    for g in range(N_KV_HEADS):
        sl = slice(g * LANES, (g + 1) * LANES)
        ksel_ref[:, sl] = jnp.where(lane < AUG_LANE, _head_slot(kv[:, 2 * KV_COLS:3 * KV_COLS], g), aug).astype(BF16)
        kwin_ref[:, sl] = jnp.where(lane < AUG_LANE, _head_slot(win[:, :KV_COLS], g), aug).astype(BF16)
    vsel_ref[...] = kv[:, 3 * KV_COLS:4 * KV_COLS].astype(BF16)
    vwin_ref[...] = win[:, KV_COLS:].astype(BF16)
    kvt_ref[0] = kv.T
    wint_ref[0] = win.T


def _attn_in_proj(x2, gain, w_main, w_gate, seq_len=None):
    rows, d = x2.shape
    tm = min(512, rows)
    nmain = w_main.shape[1]
    row = lambda n: pl.BlockSpec((tm, n), lambda i: (i, 0))
    outs = [(d, F32), (4 * KV_COLS, F32), (LANES, F32)]
    if seq_len is None:
        outs += [(2 * KV_COLS, F32)]
    else:
        outs += [(N_KV_HEADS * LANES, BF16), (KV_COLS, BF16), (N_KV_HEADS * LANES, BF16), (KV_COLS, BF16)]
    out_specs = [row(n) for n, _ in outs]
    out_shape = [jax.ShapeDtypeStruct((rows, n), dt) for n, dt in outs]
    if seq_len is not None:
        nt = seq_len // tm
        for n in (4 * KV_COLS, 2 * KV_COLS):
            out_specs.append(pl.BlockSpec((1, n, tm), lambda i: (i // nt, 0, i % nt)))
            out_shape.append(jax.ShapeDtypeStruct((rows // seq_len, n, seq_len), F32))
    return pl.pallas_call(
        functools.partial(_attn_in_kernel, seq_len),
        grid=(rows // tm,),
        in_specs=[row(d), _const_spec((1, d)), _const_spec((d, nmain)), _const_spec((d, LANES))],
        out_specs=out_specs,
        out_shape=out_shape,
        compiler_params=_cparams(("arbitrary",)),
        name="attn_in_proj",
    )(x2, gain, w_main, w_gate)


def _matmul_res_kernel(a_ref, w_ref, x_ref, o_ref):
    o_ref[...] = x_ref[...] + _dot(a_ref[...].astype(BF16), w_ref[...])


def _matmul_res(a, w, x):
    rows, k = a.shape
    n = w.shape[1]
    tm = min(512, rows)
    return pl.pallas_call(
        _matmul_res_kernel,
        grid=(rows // tm,),
        in_specs=[pl.BlockSpec((tm, k), lambda i: (i, 0)), _const_spec((k, n)),
                  pl.BlockSpec((tm, n), lambda i: (i, 0))],
        out_specs=pl.BlockSpec((tm, n), lambda i: (i, 0)),
        out_shape=jax.ShapeDtypeStruct((rows, n), F32),
        compiler_params=_cparams(("arbitrary",)),
        name="matmul_res",
    )(a, w, x)


def _finish_compress(a, b, w2):
    nseg = a.shape[0]
    pre = a + pltpu.roll(b, nseg - 1, 0)
    out = _dot(_gelu(pre).astype(BF16), w2)
    rowi = lax.broadcasted_iota(jnp.int32, (nseg, 1), 0)
    return jnp.where(rowi < nseg - 1, out, 0.0)


def _segment_sums_kernel(x_ref, pea_ref, peb_ref, w1a_ref, w1b_ref, w2_ref, ck_ref, cv_ref, acc_a, acc_b):
    kind = pl.program_id(1)
    l = pl.program_id(2)
    nseg = acc_a.shape[0]

    @pl.when(l == 0)
    def _():
        acc_a[...] = jnp.zeros(acc_a.shape, F32)
        acc_b[...] = jnp.zeros(acc_b.shape, F32)

    xs = x_ref[0]
    acc_a[...] += _dot((xs + pea_ref[0, 0]).astype(BF16), w1a_ref[0, 0])
    acc_b[...] += _dot((xs + peb_ref[0, 0]).astype(BF16), w1b_ref[0, 0])

    @pl.when((l == D_CMP - 1) & (kind == 0))
    def _():
        kc = _finish_compress(acc_a[...], acc_b[...], w2_ref[0])
        n_i = lax.broadcasted_iota(jnp.int32, (nseg, 1), 0)
        hi = ((n_i * D_CMP) // LANES * LANES).astype(F32)
        lo = ((n_i * D_CMP) % LANES).astype(F32) + (L_CMP - 1) / 2
        aug = _position_aug(hi, lo)
        lane = lax.broadcasted_iota(jnp.int32, (1, LANES), 1)
        for g in range(N_KV_HEADS):
            ck_ref[0, :, g * LANES:(g + 1) * LANES] = jnp.where(lane < AUG_LANE, _head_slot(kc, g), aug).astype(BF16)

    @pl.when((l == D_CMP - 1) & (kind == 1))
    def _():
        cv_ref[0] = _finish_compress(acc_a[...], acc_b[...], w2_ref[0]).astype(BF16)


def _segment_sums(rows3, pe_t, w1_bd, w2_bd):
    bsz, nseg, _ = rows3.shape
    pe4 = pe_t.reshape(2, L_CMP, 1, KV_COLS)
    return pl.pallas_call(
        _segment_sums_kernel,
        grid=(bsz, 2, D_CMP),
        in_specs=[pl.BlockSpec((1, nseg, KV_COLS), lambda i, k, l: (i, 0, l * 4 + k)),
                  pl.BlockSpec((1, 1, 1, KV_COLS), lambda i, k, l: (k, l, 0, 0)),
                  pl.BlockSpec((1, 1, 1, KV_COLS), lambda i, k, l: (k, D_CMP + l, 0, 0)),
                  pl.BlockSpec((1, 1, KV_COLS, KV_COLS), lambda i, k, l: (k, l, 0, 0)),
                  pl.BlockSpec((1, 1, KV_COLS, KV_COLS), lambda i, k, l: (k, D_CMP + l, 0, 0)),
                  pl.BlockSpec((1, KV_COLS, KV_COLS), lambda i, k, l: (k, 0, 0))],
        out_specs=[pl.BlockSpec((1, nseg, N_KV_HEADS * LANES), lambda i, k, l: (i, 0, 0)),
                   pl.BlockSpec((1, nseg, KV_COLS), lambda i, k, l: (i, 0, 0))],
        out_shape=[jax.ShapeDtypeStruct((bsz, nseg, N_KV_HEADS * LANES), BF16),
                   jax.ShapeDtypeStruct((bsz, nseg, KV_COLS), BF16)],
        scratch_shapes=[pltpu.VMEM((nseg, KV_COLS), F32), pltpu.VMEM((nseg, KV_COLS), F32)],
        compiler_params=_cparams(("arbitrary", "arbitrary", "arbitrary")),
        name="segment_sums",
    )(rows3, pe4, pe4, w1_bd, w1_bd, w2_bd)


def _pool_sums_kernel(pp, page, x_ref, pe_ref, w1_ref, a_ref, b_ref, rows_s):
    ftiles = 2 * KV_COLS // LANES

    def to_rows(p, carry):
        xt = x_ref[p]
        r0 = pl.multiple_of(p * page, page)
        for c in range(ftiles):
            rows_s[c, pl.ds(r0, page), :] = xt[c * LANES:(c + 1) * LANES, :].T
        return carry

    lax.fori_loop(0, pp, to_rows, 0)
    nseg = pp * page // D_CMP
    tpk = KV_COLS // LANES
    for kind in range(2):
        a = jnp.zeros((nseg, KV_COLS), F32)
        b = jnp.zeros((nseg, KV_COLS), F32)
        for l in range(D_CMP):
            xs = jnp.concatenate([rows_s[kind * tpk + c, pl.ds(l, nseg, stride=D_CMP), :] for c in range(tpk)],
                                 axis=1)
            a = a + _dot((xs + pe_ref[kind, l:l + 1, :]).astype(BF16), w1_ref[kind, l])
            b = b + _dot((xs + pe_ref[kind, D_CMP + l:D_CMP + l + 1, :]).astype(BF16), w1_ref[kind, D_CMP + l])
        a_ref[:, kind * KV_COLS:(kind + 1) * KV_COLS] = a
        b_ref[:, kind * KV_COLS:(kind + 1) * KV_COLS] = b


def _pool_segment_sums(cache_t, pe_t, w1_bd, pp):
    n_pool, _, page = cache_t.shape
    assert page == LANES and n_pool % pp == 0
    nseg = pp * page // D_CMP
    return pl.pallas_call(
        functools.partial(_pool_sums_kernel, pp, page),
        grid=(n_pool // pp,),
        in_specs=[pl.BlockSpec((pp, 2 * KV_COLS, page), lambda i: (i, 0, 0)),
                  _const_spec(pe_t.shape), _const_spec(w1_bd.shape)],
        out_specs=[pl.BlockSpec((nseg, 2 * KV_COLS), lambda i: (i, 0))] * 2,
        out_shape=[jax.ShapeDtypeStruct((n_pool * page // D_CMP, 2 * KV_COLS), F32)] * 2,
        scratch_shapes=[pltpu.VMEM((2 * KV_COLS // LANES, pp * page, LANES), F32)],
        compiler_params=_cparams(("arbitrary",)),
        name="pool_segment_sums",
    )(cache_t, pe_t, w1_bd)


AUG_LANE = HEAD_DIM
N_PIECES = 3
SOFTMAX_ROWS = 32


def _position_aug(hi, lo):
    lane = lax.broadcasted_iota(jnp.int32, (1, LANES), 1)
    in_hi = (lane >= AUG_LANE) & (lane < AUG_LANE + N_PIECES)
    in_lo = (lane >= AUG_LANE + N_PIECES) & (lane < AUG_LANE + 2 * N_PIECES)
    return jnp.where(in_hi, hi, jnp.where(in_lo, lo, 0.0))


def _head_slot(x, g):
    tile = x[:, LANES * (g // 2):LANES * (g // 2 + 1)]
    return pltpu.roll(tile, HEAD_DIM, 1) if g % 2 else tile


def _for_row_blocks(n_rows, body):
    for r0 in range(0, n_rows, SOFTMAX_ROWS):
        body(r0)


def _nsa_prompt_kernel(nc, q_ref, gates_ref, ck_ref, cv_ref, ksel_ref, vsel_ref, kwin_ref, vwin_ref,
                       onehot_ref, pool_ref, qaug_ref, o_ref,
                       qa_s, s_s, p_s, m_s, l_s, al_s, acc_s, ocmp_s, imp_s, score_s, selb_s):
    tq = Q_TILE
    rows = GROUP * tq
    rb = SOFTMAX_ROWS
    ncp = ck_ref.shape[1]
    qb = pl.program_id(1)
    q0 = qb * tq
    gates = gates_ref[0]
    lane = lax.broadcasted_iota(jnp.int32, (1, LANES), 1)
    tpos1 = q0 + lax.broadcasted_iota(jnp.int32, (tq, 1), 0)
    row_pos = lambda r0: q0 + r0 % tq + lax.broadcasted_iota(jnp.int32, (rb, 1), 0)
    win_len = WINDOW + tq
    w0 = pl.multiple_of(jnp.maximum(q0 - WINDOW, 0), tq)
    n_past_chunks = q0 // SEL_CHUNK
    pair = lambda g: slice(LANES * (g // 2), LANES * (g // 2 + 1))
    slot = lambda g: slice(LANES * g, LANES * (g + 1))

    def softmax_once(r0, width, valid):
        sb = jnp.where(valid, s_s[pl.ds(r0, rb), :width], NEG_INF)
        m = jnp.max(sb, axis=-1, keepdims=True)
        e = jnp.exp(sb - m)
        den = jnp.sum(e, axis=-1, keepdims=True)
        p = e * jnp.where(m > 0.5 * NEG_INF, 1.0 / den, 0.0)
        p_s[pl.ds(r0, rb), :width] = p.astype(BF16)
        return p

    n_i = lax.broadcasted_iota(jnp.int32, (1, ncp), 1)
    c_end = jnp.where(n_i < nc, n_i * D_CMP + (L_CMP - 1), jnp.int32(2 ** 30))

    for g in range(N_KV_HEADS):
        for r in range(GROUP):
            h = g * GROUP + r
            qh = q_ref[0, :, LANES * (h // 2):LANES * (h // 2 + 1)] * (HEAD_DIM ** -0.5)
            if h % 2:
                qh = pltpu.roll(qh, HEAD_DIM, 1)
            qa_s[g, r * tq:(r + 1) * tq, :] = jnp.where(lane < AUG_LANE, qh, qaug_ref[h:h + 1, :]).astype(BF16)

        s_s[:, :ncp] = _dot_nt(qa_s[g], ck_ref[0, :, slot(g)])
        imp_s[...] = jnp.zeros(imp_s.shape, F32)

        def cmp_block(r0):
            p = softmax_once(r0, ncp, c_end <= row_pos(r0))
            imp_s[pl.ds(r0 % tq, rb), :] += p

        _for_row_blocks(rows, cmp_block)
        ocmp_s[g] = _dot(p_s[:, :ncp], cv_ref[0, :, pair(g)])
        score_s[g * tq:(g + 1) * tq, :] = _selection_scores(_exact_pool(imp_s[...], pool_ref[...]), tpos1)

    selb_s[...] = jnp.where(_topk_mask(score_s[...], TOP_N) > 0.5, 0.0, NEG_INF).astype(BF16)

    for g in range(N_KV_HEADS):
        sel_bias = selb_s[g * tq:(g + 1) * tq, :]
        q_full = jnp.concatenate([qa_s[g], jnp.concatenate([sel_bias] * GROUP, axis=0)], axis=1)

        m_s[...] = jnp.full(m_s.shape, NEG_INF, F32)
        l_s[...] = jnp.zeros(l_s.shape, F32)
        acc_s[...] = jnp.zeros(acc_s.shape, F32)

        def sel_chunk(c, causal):
            k0 = pl.multiple_of(c * SEL_CHUNK, SEL_CHUNK)
            k_full = jnp.concatenate([ksel_ref[0, pl.ds(k0, SEL_CHUNK), slot(g)],
                                      onehot_ref[pl.ds(k0, SEL_CHUNK), :]], axis=1)
            s_s[:, :SEL_CHUNK] = _dot_nt(q_full, k_full)
            kpos = k0 + lax.broadcasted_iota(jnp.int32, (1, SEL_CHUNK), 1)

            def block(r0):
                rs = pl.ds(r0, rb)
                sb = s_s[rs, :SEL_CHUNK]
                if causal:
                    sb = jnp.where(kpos <= row_pos(r0), sb, NEG_INF)
                m_old = m_s[rs, :]
                m_new = jnp.maximum(m_old, jnp.max(sb, axis=-1, keepdims=True))
                alpha = jnp.exp(m_old - m_new)
                p = jnp.exp(sb - jnp.tile(m_new, (1, SEL_CHUNK // LANES)))
                l_s[rs, :] = alpha * l_s[rs, :] + jnp.sum(p, axis=-1, keepdims=True)
                m_s[rs, :] = m_new
                al_s[rs, :] = alpha
                p_s[rs, :SEL_CHUNK] = p.astype(BF16)

            _for_row_blocks(rows, block)
            acc_s[...] = al_s[...] * acc_s[...] + _dot(p_s[:, :SEL_CHUNK], vsel_ref[0, pl.ds(k0, SEL_CHUNK), pair(g)])

        def past_chunk(c, carry):
            sel_chunk(c, False)
            return carry

        lax.fori_loop(0, n_past_chunks, past_chunk, 0)
        sel_chunk(n_past_chunks, True)
        o_sel = acc_s[...] / l_s[...]

        s_s[:, :win_len] = _dot_nt(qa_s[g], kwin_ref[0, pl.ds(w0, win_len), slot(g)])
        wpos = w0 + lax.broadcasted_iota(jnp.int32, (1, win_len), 1)

        def win_block(r0):
            wd = row_pos(r0) - wpos
            softmax_once(r0, win_len, (wd >= 0) & (wd < WINDOW))

        _for_row_blocks(rows, win_block)
        o_win = _dot(p_s[:, :win_len], vwin_ref[0, pl.ds(w0, win_len), pair(g)])
        o_cmp = ocmp_s[g]

        halves = []
        for r in range(GROUP):
            c0 = 3 * (g * GROUP + r)
            rs = slice(r * tq, (r + 1) * tq)
            o_r = (gates[:, c0:c0 + 1] * o_cmp[rs] + gates[:, c0 + 1:c0 + 2] * o_sel[rs]
                   + gates[:, c0 + 2:c0 + 3] * o_win[rs])
            halves.append(pltpu.roll(o_r, HEAD_DIM, 1) if (r - g) % 2 else o_r)
        for j in range(GROUP // 2):
            tile = jnp.where(lane < HEAD_DIM, halves[2 * j], halves[2 * j + 1])
            o_ref[0, :, LANES * (2 * g + j):LANES * (2 * g + j + 1)] = tile.astype(o_ref.dtype)


def _nsa_prompt_attend(q, gates, cmp_k, cmp_v, ksel, vsel, kwin, vwin, onehot, pool, qaug):
    bsz, t, d = q.shape
    nc = (t - L_CMP) // D_CMP + 1
    ncp = cmp_k.shape[1]
    rows = GROUP * Q_TILE
    width = max(ncp, SEL_CHUNK, WINDOW + Q_TILE)
    seq = lambda a: pl.BlockSpec((1,) + a.shape[1:], lambda b, i: (b, 0, 0), pipeline_mode=pl.Buffered(1))
    stat = pltpu.VMEM((rows, LANES), F32)
    return pl.pallas_call(
        functools.partial(_nsa_prompt_kernel, nc),
        grid=(bsz, t // Q_TILE),
        in_specs=[pl.BlockSpec((1, Q_TILE, d), lambda b, i: (b, i, 0)),
                  pl.BlockSpec((1, Q_TILE, LANES), lambda b, i: (b, i, 0)),
                  seq(cmp_k), seq(cmp_v), seq(ksel), seq(vsel), seq(kwin), seq(vwin),
                  _const_spec(onehot.shape), _const_spec(pool.shape), _const_spec(qaug.shape)],
        out_specs=pl.BlockSpec((1, Q_TILE, d), lambda b, i: (b, i, 0)),
        out_shape=jax.ShapeDtypeStruct((bsz, t, d), BF16),
        scratch_shapes=[pltpu.VMEM((N_KV_HEADS, rows, LANES), BF16),
                        pltpu.VMEM((rows, width), F32),
                        pltpu.VMEM((rows, width), BF16),
                        stat, stat, stat, stat,
                        pltpu.VMEM((N_KV_HEADS, rows, LANES), F32),
                        pltpu.VMEM((Q_TILE, ncp), F32),
                        pltpu.VMEM((N_KV_HEADS * Q_TILE, LANES), F32),
                        pltpu.VMEM((N_KV_HEADS * Q_TILE, LANES), BF16)],
        compiler_params=_cparams(("arbitrary", "arbitrary")),
        name="nsa_prompt_attend",
    )(q, gates, cmp_k, cmp_v, ksel, vsel, kwin, vwin, onehot, pool, qaug)


def _sample_qpad(qf, tq):
    parts = []
    for g in range(N_KV_HEADS):
        piece = qf[:, g * KV_COLS:(g + 1) * KV_COLS]
        for r in range(GROUP):
            parts.append(_place_slot(piece, r, g))
    return jnp.concatenate(parts, axis=0).astype(BF16)


def _sample_rows_to_tokens(o, tq):
    chunks = []
    for g in range(N_KV_HEADS):
        chunk = jnp.zeros((tq, KV_COLS), F32)
        for r in range(GROUP):
            i0 = (g * GROUP + r) * tq
            chunk = chunk + _place_slot(o[i0:i0 + tq], g, r)
        chunks.append(chunk)
    return jnp.concatenate(chunks, axis=1)


def _nsa_sample_cmp_kernel(n_pages, past_len, tq, pt_ref, q_ref, w2_ref, pool_ref, slope_ref, *refs):
    a_refs = refs[:n_pages]
    b_refs = refs[n_pages:2 * n_pages]
    ocmp_ref, sel_ref = refs[2 * n_pages:]
    del pt_ref
    rows = N_HEADS * tq
    a = jnp.concatenate([r[0] for r in a_refs], axis=0)
    b = jnp.concatenate([r[0] for r in b_refs], axis=0)
    nseg = a.shape[0]
    kc = _finish_compress(a[:, :KV_COLS], b[:, :KV_COLS], w2_ref[0]).astype(BF16)
    vc = _finish_compress(a[:, KV_COLS:], b[:, KV_COLS:], w2_ref[1]).astype(BF16)

    qpad = _sample_qpad(q_ref[0] * (HEAD_DIM ** -0.5), tq)
    slope = slope_ref[...]
    rowi = lax.broadcasted_iota(jnp.int32, (rows, 1), 0)
    tpos = past_len + rowi % tq
    tposf = tpos.astype(F32)
    nc = (past_len + tq - L_CMP) // D_CMP + 1
    n_i = lax.broadcasted_iota(jnp.int32, (1, nseg), 1)
    c_end = jnp.where(n_i < nc, n_i * D_CMP + (L_CMP - 1), jnp.int32(2 ** 30))
    center = (n_i * D_CMP).astype(F32) + (L_CMP - 1) / 2
    s = _dot_nt(qpad, kc) - slope * (tposf - center)
    p_cmp = _masked_softmax(s, c_end <= tpos)
    ocmp_ref[0] = _dot(p_cmp.astype(BF16), vc)

    imps = []
    for g in range(N_KV_HEADS):
        i0 = g * GROUP * tq
        imp = p_cmp[i0:i0 + tq]
        for r in range(1, GROUP):
            imp = imp + p_cmp[i0 + r * tq:i0 + (r + 1) * tq]
        imps.append(imp)
    tpos_g = past_len + lax.broadcasted_iota(jnp.int32, (N_KV_HEADS * tq, 1), 0) % tq
    score = _selection_scores(_exact_pool(jnp.concatenate(imps, axis=0), pool_ref[...]), tpos_g)
    sel = _topk_mask(score, TOP_N)
    sel_ref[0] = jnp.concatenate([sel[g * tq:(g + 1) * tq] for g in range(N_KV_HEADS) for _ in range(GROUP)],
                                 axis=0).astype(BF16)


def _nsa_sample_cmp(page_table, q, a_all, b_all, w2_bd, pool, slope_rows, page, past_len):
    bs, tq, d = q.shape
    n_pages = page_table.shape[1]
    seg_pp = page // D_CMP
    rows = N_HEADS * tq
    a3 = a_all.reshape(-1, seg_pp, 2 * KV_COLS)
    b3 = b_all.reshape(-1, seg_pp, 2 * KV_COLS)

    def page_spec(p):
        return pl.BlockSpec((1, seg_pp, 2 * KV_COLS), lambda s, pt: (pt[s, p], 0, 0))

    cst = lambda shape: pl.BlockSpec(shape, lambda s, pt: (0,) * len(shape))
    grid_spec = pltpu.PrefetchScalarGridSpec(
        num_scalar_prefetch=1,
        grid=(bs,),
        in_specs=[pl.BlockSpec((1, tq, d), lambda s, pt: (s, 0, 0)),
                  cst(w2_bd.shape), cst(pool.shape), cst(slope_rows.shape)]
                 + [page_spec(p) for p in range(n_pages)] * 2,
        out_specs=[pl.BlockSpec((1, rows, KV_COLS), lambda s, pt: (s, 0, 0)),
                   pl.BlockSpec((1, rows, LANES), lambda s, pt: (s, 0, 0))],
    )
    return pl.pallas_call(
        functools.partial(_nsa_sample_cmp_kernel, n_pages, past_len, tq),
        grid_spec=grid_spec,
        out_shape=[jax.ShapeDtypeStruct((bs, rows, KV_COLS), F32),
                   jax.ShapeDtypeStruct((bs, rows, LANES), BF16)],
        compiler_params=_cparams(("arbitrary",)),
        name="nsa_sample_cmp",
    )(page_table, q, w2_bd, pool, slope_rows, *([a3] * n_pages), *([b3] * n_pages))


def _nsa_sample_attend_kernel(n_pages, page, tq, pt_ref, q_ref, gates_ref, kvn_ref, winn_ref, winp_ref,
                              ocmp_ref, sel_ref, expand_ref, slope_ref, *refs):
    page_refs = refs[:n_pages]
    o_ref, newwin_ref = refs[n_pages:]
    del pt_ref
    rows = N_HEADS * tq
    past_len = n_pages * page
    wbuf = winp_ref.shape[2]
    qpad = _sample_qpad(q_ref[0] * (HEAD_DIM ** -0.5), tq)
    slope = slope_ref[...]
    rowi = lax.broadcasted_iota(jnp.int32, (rows, 1), 0)
    tpos = past_len + rowi % tq
    tposf = tpos.astype(F32)
    newpos = past_len + lax.broadcasted_iota(jnp.int32, (1, LANES), 1)
    pad_rows = lambda a: jnp.concatenate([a, jnp.zeros((LANES - tq, a.shape[1]), F32)], axis=0)

    kvn = pad_rows(kvn_ref[0])
    k_new = kvn[:, 2 * KV_COLS:3 * KV_COLS].astype(BF16)
    v_new = kvn[:, 3 * KV_COLS:4 * KV_COLS].astype(BF16)
    scores = [_dot(qpad, r[0, :KV_COLS, :].astype(BF16)) for r in page_refs]
    scores.append(_dot_nt(qpad, k_new))
    s = jnp.concatenate(scores, axis=1)
    nk = past_len + LANES
    kpos = lax.broadcasted_iota(jnp.int32, (1, nk), 1)
    picked = _dot(sel_ref[0], expand_ref[...]) > 0.5
    s = s - slope * (tposf - kpos.astype(F32))
    p = _masked_softmax(s, picked & (kpos <= tpos)).astype(BF16)
    o_sel = _dot(p[:, past_len:], v_new)
    for i, r in enumerate(page_refs):
        o_sel = o_sel + _dot_nt(p[:, i * page:(i + 1) * page], r[0, KV_COLS:, :].astype(BF16))

    winp = winp_ref[0]
    winn = pad_rows(winn_ref[0])
    winn_t = pltpu.roll(winn.T, LANES - tq, 1)
    lane_w = lax.broadcasted_iota(jnp.int32, (1, wbuf), 1)
    newwin_ref[0] = jnp.where(lane_w >= wbuf - tq, jnp.concatenate([winn_t] * (wbuf // LANES), axis=1),
                              pltpu.roll(winp, wbuf - tq, 1))
    sp = _dot(qpad, winp[:KV_COLS, :].astype(BF16))
    sn = _dot_nt(qpad, winn[:, :KV_COLS].astype(BF16))
    wpos_p = past_len - wbuf + lax.broadcasted_iota(jnp.int32, (1, wbuf), 1)
    wpos = jnp.concatenate([wpos_p, newpos], axis=1)
    wd = tpos - wpos
    s = jnp.concatenate([sp, sn], axis=1) - slope * wd.astype(F32)
    p = _masked_softmax(s, (wd >= 0) & (wd < WINDOW) & (wpos >= 0)).astype(BF16)
    o_win = _dot_nt(p[:, :wbuf], winp[KV_COLS:, :].astype(BF16)) + _dot(p[:, wbuf:], winn[:, KV_COLS:].astype(BF16))

    gates = gates_ref[0]
    o_cmp = ocmp_ref[0]
    merged = []
    for h in range(N_HEADS):
        rs = slice(h * tq, (h + 1) * tq)
        merged.append(gates[:, 3 * h:3 * h + 1] * o_cmp[rs] + gates[:, 3 * h + 1:3 * h + 2] * o_sel[rs]
                      + gates[:, 3 * h + 2:3 * h + 3] * o_win[rs])
    o_ref[0] = _sample_rows_to_tokens(jnp.concatenate(merged, axis=0), tq).astype(o_ref.dtype)


def _nsa_sample_attend(page_table, cache_t, q, gates, kv_new, win_new, win_past_t, o_cmp, sel, expand, slope_rows):
    bs, tq, d = q.shape
    n_pages = page_table.shape[1]
    page = cache_t.shape[2]
    wbuf = win_past_t.shape[2]
    rows = N_HEADS * tq

    def page_spec(p):
        return pl.BlockSpec((1, 2 * KV_COLS, page), lambda s, pt: (pt[s, p], 1, 0))

    per = lambda r, n: pl.BlockSpec((1, r, n), lambda s, pt: (s, 0, 0))
    cst = lambda shape: pl.BlockSpec(shape, lambda s, pt: (0,) * len(shape))
    grid_spec = pltpu.PrefetchScalarGridSpec(
        num_scalar_prefetch=1,
        grid=(bs,),
        in_specs=[per(tq, d), per(tq, LANES), per(tq, 4 * KV_COLS), per(tq, 2 * KV_COLS),
                  per(2 * KV_COLS, wbuf), per(rows, KV_COLS), per(rows, LANES),
                  cst(expand.shape), cst(slope_rows.shape)]
                 + [page_spec(p) for p in range(n_pages)],
        out_specs=[per(tq, d), per(2 * KV_COLS, wbuf)],
    )
    return pl.pallas_call(
        functools.partial(_nsa_sample_attend_kernel, n_pages, page, tq),
        grid_spec=grid_spec,
        out_shape=[jax.ShapeDtypeStruct((bs, tq, d), BF16),
                   jax.ShapeDtypeStruct((bs, 2 * KV_COLS, wbuf), F32)],
        compiler_params=_cparams(("arbitrary",)),
        name="nsa_sample_attend",
    )(page_table, q, gates, kv_new, win_new, win_past_t, o_cmp, sel, expand, slope_rows,
      *([cache_t] * n_pages))


def _cmul_add(ar, ai, hr, hi, xr, xi):
    return ar * hr - ai * hi + xr, ar * hi + ai * hr + xi


def _s5_kernel(mode, tm, x_ref, h0r_ref, h0i_ref, gain_ref, bre_ref, bim_ref, are_ref, aim_ref,
               pwr_ref, pwi_ref, cre_ref, cim_ref, d_ref, wglu_ref,
               xo_ref, hfr_ref, hfi_ref, hr_s, hi_s, cr_s, ci_s):
    d = x_ref.shape[1]
    nstate = cr_s.shape[1]
    nblk = bre_ref.shape[0]
    cb = d // nblk
    sb = nstate // nblk
    tpb = sb // LANES
    x = x_ref[...]
    uf = _rms(x, gain_ref[...])
    ub = uf.astype(BF16)

    if mode == "carry":
        nj = tm // SUBLANES
        cs = nj + SUBLANES
        dst = [(s * cs, s * nj, nj) for s in range(SUBLANES)]
    else:
        dst = [(0, 0, tm)]
    for j in range(nblk):
        uj = ub[:, j * cb:(j + 1) * cb]
        br = _dot(uj, bre_ref[j])
        bi = _dot(uj, bim_ref[j])
        for k in range(tpb):
            for (o, i0, n) in dst:
                hr_s[j * tpb + k, o:o + n, :] = br[i0:i0 + n, k * LANES:(k + 1) * LANES]
                hi_s[j * tpb + k, o:o + n, :] = bi[i0:i0 + n, k * LANES:(k + 1) * LANES]

    lc = SSM_LANE_CHUNK
    par = lc // LANES
    if mode == "carry":
        @pl.when(pl.program_id(1) == 0)
        def _():
            cr_s[...] = jnp.broadcast_to(h0r_ref[0], cr_s.shape)
            ci_s[...] = jnp.broadcast_to(h0i_ref[0], ci_s.shape)

        sub = lax.broadcasted_iota(jnp.int32, (SUBLANES, 1), 0)
        for c in range(nstate // lc):
            sl = slice(c * lc, (c + 1) * lc)
            ar = jnp.broadcast_to(are_ref[:, sl], (SUBLANES, lc))
            ai = jnp.broadcast_to(aim_ref[:, sl], (SUBLANES, lc))

            tiles = range(c * par, (c + 1) * par)
            lane = lambda v, k: v[:, k * LANES:(k + 1) * LANES]

            def p1(jj, st):
                idx = pl.ds(jj, SUBLANES, stride=cs)
                out = []
                for k, q in enumerate(tiles):
                    nr, ni = _cmul_add(lane(ar, k), lane(ai, k), st[2 * k], st[2 * k + 1],
                                       hr_s[q, idx, :], hi_s[q, idx, :])
                    hr_s[q, idx, :] = nr
                    hi_s[q, idx, :] = ni
                    out += [nr, ni]
                return tuple(out)

            z = jnp.zeros((SUBLANES, LANES), F32)
            lf = lax.fori_loop(0, nj, p1, (z,) * (2 * par))
            lfr = jnp.concatenate(lf[0::2], axis=1)
            lfi = jnp.concatenate(lf[1::2], axis=1)

            alr = pwr_ref[nj - 1:nj, sl]
            ali = pwi_ref[nj - 1:nj, sl]
            inr = cr_s[0:1, sl]
            ini = ci_s[0:1, sl]
            hin_r = jnp.zeros((SUBLANES, lc), F32)
            hin_i = jnp.zeros((SUBLANES, lc), F32)
            for s in range(SUBLANES):
                hin_r = jnp.where(sub == s, inr, hin_r)
                hin_i = jnp.where(sub == s, ini, hin_i)
                inr, ini = _cmul_add(alr, ali, inr, ini, lfr[s:s + 1], lfi[s:s + 1])
            cr_s[:, sl] = jnp.broadcast_to(inr, (SUBLANES, lc))
            ci_s[:, sl] = jnp.broadcast_to(ini, (SUBLANES, lc))

            def p2(jj, carry):
                idx = pl.ds(jj, SUBLANES, stride=cs)
                pr = pwr_ref[pl.ds(jj, 1), sl]
                pi_ = pwi_ref[pl.ds(jj, 1), sl]
                for k, q in enumerate(tiles):
                    nr, ni = _cmul_add(lane(pr, k), lane(pi_, k), lane(hin_r, k), lane(hin_i, k),
                                       hr_s[q, idx, :], hi_s[q, idx, :])
                    hr_s[q, idx, :] = nr
                    hi_s[q, idx, :] = ni
                return carry

            lax.fori_loop(0, nj, p2, 0)

        @pl.when(pl.program_id(1) == pl.num_programs(1) - 1)
        def _():
            hfr_ref[0] = cr_s[0:1, :]
            hfi_ref[0] = ci_s[0:1, :]
    else:
        nseq = tm // mode
        for q in range(nstate // LANES):
            sl = slice(q * LANES, (q + 1) * LANES)
            ar = jnp.broadcast_to(are_ref[:, sl], (nseq, LANES))
            ai = jnp.broadcast_to(aim_ref[:, sl], (nseq, LANES))
            sr = h0r_ref[:, sl]
            si = h0i_ref[:, sl]
            for t in range(mode):
                idx = pl.ds(t, nseq, stride=mode)
                sr, si = _cmul_add(ar, ai, sr, si, hr_s[q, idx, :], hi_s[q, idx, :])
                hr_s[q, idx, :] = sr
                hi_s[q, idx, :] = si
            hfr_ref[:, sl] = sr
            hfi_ref[:, sl] = si

    def block_rows(ref, j):
        cols = [jnp.concatenate([ref[j * tpb + k, o:o + n, :] for (o, _, n) in dst], axis=0) for k in range(tpb)]
        return jnp.concatenate(cols, axis=1).astype(BF16)

    ys = []
    for j in range(nblk):
        ys.append(_dot(block_rows(hr_s, j), cre_ref[j]) - _dot(block_rows(hi_s, j), cim_ref[j]))
    y = jnp.concatenate(ys, axis=1) + d_ref[...] * uf
    ab = _dot(_gelu(y).astype(BF16), wglu_ref[...])
    xo_ref[...] = x + ab[:, :d] * _sigmoid(ab[:, d:])


def _s5_layer(x2, h0r, h0i, seq_len, n_seq, gain, prm):
    rows, d = x2.shape
    nstate = h0r.shape[1]
    if seq_len >= 256:
        mode, tm = "carry", 256
        nt = seq_len // tm
        grid = (n_seq, nt)
        xmap = lambda b, i: (b * nt + i, 0)
        h0_spec = pl.BlockSpec((1, 1, nstate), lambda b, i: (b, 0, 0))
        hf_spec = pl.BlockSpec((1, 1, nstate), lambda b, i: (b, 0, 0))
        h0r, h0i = h0r[:, None, :], h0i[:, None, :]
        hf_shape = (n_seq, 1, nstate)
        srows = (tm // SUBLANES + SUBLANES) * SUBLANES
        sem = ("arbitrary", "arbitrary")
        cst = lambda shape: pl.BlockSpec(shape, lambda b, i: (0,) * len(shape), pipeline_mode=pl.Buffered(1))
    else:
        mode = seq_len
        tm = min(256, rows)
        nseq_t = tm // seq_len
        grid = (rows // tm,)
        xmap = lambda i: (i, 0)
        h0_spec = pl.BlockSpec((nseq_t, nstate), lambda i: (i, 0))
        hf_spec = pl.BlockSpec((nseq_t, nstate), lambda i: (i, 0))
        hf_shape = (n_seq, nstate)
        srows = tm
        sem = ("arbitrary",)
        cst = lambda shape: pl.BlockSpec(shape, lambda i: (0,) * len(shape), pipeline_mode=pl.Buffered(1))
    pw_r, pw_i = prm["pw_r"], prm["pw_i"]
    consts = [gain, prm["b_r"], prm["b_i"], prm["a_r"], prm["a_i"], pw_r, pw_i, prm["c_r"], prm["c_i"],
              prm["d"], prm["w_glu"]]
    outs = pl.pallas_call(
        functools.partial(_s5_kernel, mode, tm),
        grid=grid,
        in_specs=[pl.BlockSpec((tm, d), xmap), h0_spec, h0_spec] + [cst(c.shape) for c in consts],
        out_specs=[pl.BlockSpec((tm, d), xmap), hf_spec, hf_spec],
        out_shape=[jax.ShapeDtypeStruct((rows, d), F32), jax.ShapeDtypeStruct(hf_shape, F32),
                   jax.ShapeDtypeStruct(hf_shape, F32)],
        scratch_shapes=[pltpu.VMEM((nstate // LANES, srows, LANES), F32),
                        pltpu.VMEM((nstate // LANES, srows, LANES), F32),
                        pltpu.VMEM((SUBLANES, nstate), F32), pltpu.VMEM((SUBLANES, nstate), F32)],
        compiler_params=_cparams(sem),
        name="s5_layer",
    )(x2, h0r, h0i, *consts)
    xo, hfr, hfi = outs
    return xo, hfr.reshape(n_seq, nstate), hfi.reshape(n_seq, nstate)


def _s5_params(a_re, a_im, log_dt, b_re, b_im, c_re, c_im, d_skip, w_glu, n_pow):
    g, p = a_re.shape
    ch = b_re.shape[2]
    lr, li = a_re.astype(F32), a_im.astype(F32)
    dt = jnp.exp(log_dt.astype(F32))[:, None]
    mag = jnp.exp(lr * dt)
    ar, ai = mag * jnp.cos(li * dt), mag * jnp.sin(li * dt)
    den = lr * lr + li * li
    kr = ((ar - 1.0) * lr + ai * li) / den
    ki = (ai * lr - (ar - 1.0) * li) / den
    br, bi = b_re.astype(F32), b_im.astype(F32)
    bbar_r = kr[:, :, None] * br - ki[:, :, None] * bi
    bbar_i = kr[:, :, None] * bi + ki[:, :, None] * br
    nblk = g // SSM_GROUP_BLOCK
    eye = jnp.eye(SSM_GROUP_BLOCK, dtype=F32)

    def b_blocks(m):
        m = m.reshape(nblk, SSM_GROUP_BLOCK, p, ch)
        return jnp.einsum("jgpc,gh->jgchp", m, eye).reshape(nblk, SSM_GROUP_BLOCK * ch, SSM_GROUP_BLOCK * p).astype(BF16)

    def c_blocks(m):
        m = m.reshape(nblk, SSM_GROUP_BLOCK, ch, p)
        return jnp.einsum("jgcp,gh->jgphc", m, eye).reshape(nblk, SSM_GROUP_BLOCK * p, SSM_GROUP_BLOCK * ch).astype(BF16)

    ar, ai = ar.reshape(1, g * p), ai.reshape(1, g * p)
    pw_r, pw_i = [ar], [ai]
    for _ in range(n_pow - 1):
        pw_r, pw_i = pw_r + [pw_r[-1] * ar - pw_i[-1] * ai], pw_i + [pw_r[-1] * ai + pw_i[-1] * ar]
    return dict(b_r=b_blocks(bbar_r), b_i=b_blocks(bbar_i), a_r=ar, a_i=ai,
                pw_r=jnp.concatenate(pw_r, axis=0), pw_i=jnp.concatenate(pw_i, axis=0),
                c_r=c_blocks(c_re.astype(F32)), c_i=c_blocks(c_im.astype(F32)),
                d=d_skip.astype(F32).reshape(1, -1), w_glu=w_glu.astype(BF16))


def _ffn_ple_kernel(mode, final_norm, x_ref, p_ref, b1_ref, b2_ref, gffn_ref, wup_ref, cw_ref, cb_ref,
                    wdown_ref, gple_ref, wproj_ref, wgate_ref, gfin_ref, xo_ref, tail_ref, carry_ref):
    tm = x_ref.shape[0]
    f = cw_ref.shape[1]
    x = x_ref[...]
    u = _rms(x, gffn_ref[...]).astype(BF16)
    hg = _dot(u, wup_ref[...])
    h = hg[:, :f]
    gate_branch = hg[:, f:]
    rowi = lax.broadcasted_iota(jnp.int32, (tm, 1), 0)
    r1 = pltpu.roll(h, 1, 0)
    r2 = pltpu.roll(h, 2, 0)
    if mode == "carry":
        @pl.when(pl.program_id(1) == 0)
        def _():
            carry_ref[SUBLANES - 2:SUBLANES, :] = b1_ref[0]
        c0 = carry_ref[SUBLANES - 2:SUBLANES - 1, :]
        c1 = carry_ref[SUBLANES - 1:SUBLANES, :]
        hm1 = jnp.where(rowi == 0, c1, r1)
        hm2 = jnp.where(rowi == 0, c0, jnp.where(rowi == 1, c1, r2))
        carry_ref[...] = h[tm - SUBLANES:tm, :]
        tail_ref[...] = h[tm - SUBLANES:tm, :]
    else:
        t = rowi % mode
        hm1 = jnp.where(t == 0, b1_ref[...], r1)
        hm2 = jnp.where(t < 2, b2_ref[...], r2)
        tail_ref[...] = h
    conv = cb_ref[...] + cw_ref[0:1, :] * hm2 + cw_ref[1:2, :] * hm1 + cw_ref[2:3, :] * h
    act = (_gelu(conv) * gate_branch).astype(BF16)
    x1 = x + _dot(act, wdown_ref[...])
    gate = _sigmoid(_dot(_rms(x1, gple_ref[...]).astype(BF16), wgate_ref[...]))
    x2 = x1 + _dot(p_ref[...].astype(BF16), wproj_ref[...]) * gate
    xo_ref[...] = _rms(x2, gfin_ref[...]) if final_norm else x2


def _ffn_ple(x2, p2, buf, seq_len, n_seq, final_norm, gffn, wup, cw, cb, wdown, gple, wproj, wgate, gfin):
    rows, d = x2.shape
    f = cw.shape[1]
    ple = p2.shape[1]
    consts = [gffn, wup, cw, cb, wdown, gple, wproj, wgate, gfin]
    if seq_len >= 256:
        mode, tm = "carry", 256
        nt = seq_len // tm
        grid = (n_seq, nt)
        rmap = lambda b, i: (b * nt + i, 0)
        b1, b2 = buf, buf
        bspec = pl.BlockSpec((1, CONV_W - 1, f), lambda b, i: (b, 0, 0))
        tail_rows = SUBLANES
        sem = ("arbitrary", "arbitrary")
        cst = lambda shape: pl.BlockSpec(shape, lambda b, i: (0,) * len(shape), pipeline_mode=pl.Buffered(1))
    else:
        mode = seq_len
        tm = min(256, rows)
        grid = (rows // tm,)
        rmap = lambda i: (i, 0)
        zero = jnp.zeros((n_seq, seq_len, f), F32)
        b1 = zero.at[:, 0].set(buf[:, 1]).reshape(rows, f)
        b2 = zero.at[:, 0].set(buf[:, 0]).at[:, 1].set(buf[:, 1]).reshape(rows, f)
        bspec = pl.BlockSpec((tm, f), rmap)
        tail_rows = tm
        sem = ("arbitrary",)
        cst = lambda shape: pl.BlockSpec(shape, lambda i: (0,) * len(shape), pipeline_mode=pl.Buffered(1))
    n_tiles = rows // tm
    xo, tail = pl.pallas_call(
        functools.partial(_ffn_ple_kernel, mode, final_norm),
        grid=grid,
        in_specs=[pl.BlockSpec((tm, d), rmap), pl.BlockSpec((tm, ple), rmap), bspec, bspec]
                 + [cst(c.shape) for c in consts],
        out_specs=[pl.BlockSpec((tm, d), rmap), pl.BlockSpec((tail_rows, f), rmap)],
        out_shape=[jax.ShapeDtypeStruct((rows, d), F32), jax.ShapeDtypeStruct((n_tiles * tail_rows, f), F32)],
        scratch_shapes=[pltpu.VMEM((SUBLANES, f), F32)],
        compiler_params=_cparams(sem),
        name="ffn_ple",
    )(x2, p2, b1, b2, *consts)
    if mode == "carry":
        new_buf = tail.reshape(n_seq, n_tiles // n_seq, SUBLANES, f)[:, -1, SUBLANES - (CONV_W - 1):, :]
    else:
        new_buf = tail.reshape(n_seq, seq_len, f)[:, seq_len - (CONV_W - 1):, :]
    return xo, new_buf


def _block_diag_heads(w):
    eye = jnp.eye(N_KV_HEADS, dtype=w.dtype)
    out = jnp.einsum("...de,gh->...gdhe", w, eye)
    return out.reshape(*w.shape[:-2], KV_COLS, KV_COLS)


def _slopes():
    return np.exp2(-8.0 * np.arange(1, N_HEADS + 1, dtype=np.float64) / N_HEADS).astype(np.float32)


def _query_aug(slopes):
    rest = jnp.asarray(slopes, F32)
    aug = jnp.zeros((N_HEADS, LANES), F32)
    for i in range(N_PIECES):
        piece = rest.astype(BF16).astype(F32)
        rest = rest - piece
        aug = aug.at[:, AUG_LANE + i].set(piece).at[:, AUG_LANE + N_PIECES + i].set(piece)
    return aug


def _expand_matrix(n_keys):
    return (np.arange(LANES)[:, None] == (np.arange(n_keys)[None, :] // L_SEL)).astype(np.float32)


def _pool_matrix(n_cmp):
    return (np.arange(n_cmp)[:, None] // SEL_PER_CMP == np.arange(LANES)[None, :]).astype(np.float32)


def _nsa_layer(xp, xs, cache_l, cache_win_l, page_table, gain, w_in, w_out, pe, w1, w2):
    bsz, t, d = xp.shape
    bs, tq, _ = xs.shape
    page = cache_l.shape[1]
    n_pages = page_table.shape[1]
    past_len = n_pages * page
    n_pool = cache_l.shape[0]
    assert t % SEL_CHUNK == 0 and t // L_SEL <= LANES and t >= WINDOW + Q_TILE
    assert (past_len + LANES) // L_SEL <= LANES and page % D_CMP == 0 and tq <= SUBLANES

    qcols = N_HEADS * HEAD_DIM
    nmain = qcols + 6 * KV_COLS
    w_main = w_in[:, :nmain].astype(BF16)
    w_gate = jnp.pad(w_in[:, nmain:], ((0, 0), (0, LANES - (w_in.shape[1] - nmain)))).astype(BF16)
    w_out_b = w_out.astype(BF16)
    pe_t = jnp.tile(pe, (1, 1, N_KV_HEADS))
    w1_bd = _block_diag_heads(w1).astype(BF16)
    w2_bd = _block_diag_heads(w2).astype(BF16)
    slopes = _slopes()

    q, kv, gates, ksel, vsel, kwin, vwin, kv_t, win_t = _attn_in_proj(
        xp.reshape(bsz * t, d), gain, w_main, w_gate, seq_len=t)
    sh = lambda a: a.reshape(bsz, t, a.shape[-1])
    cmp_k, cmp_v = _segment_sums(kv.reshape(bsz, t // D_CMP, D_CMP * 4 * KV_COLS), pe_t, w1_bd, w2_bd)
    onehot = jnp.asarray(_expand_matrix(t).T, BF16)
    pool = jnp.asarray(_pool_matrix(t // D_CMP), BF16)
    o = _nsa_prompt_attend(sh(q), sh(gates), cmp_k, cmp_v, sh(ksel), sh(vsel), sh(kwin), sh(vwin), onehot, pool,
                           jnp.asarray(_query_aug(slopes)))
    xp_new = _matmul_res(o.reshape(bsz * t, d), w_out_b, xp.reshape(bsz * t, d)).reshape(bsz, t, d)
    keep = min(WINDOW, t)
    from_t = lambda a, n: jnp.transpose(a.reshape(a.shape[0], n, N_KV_HEADS, HEAD_DIM, a.shape[-1]), (0, 4, 1, 2, 3))
    kv_p = from_t(kv_t, 4)
    win_p = from_t(win_t[:, :, t - keep:], 2)

    cache_t = jnp.transpose(cache_l, (0, 2, 3, 4, 1)).reshape(n_pool, 4 * KV_COLS, page)
    wbuf = cache_win_l.shape[1]
    win_past_t = jnp.transpose(cache_win_l, (0, 2, 3, 4, 1)).reshape(bs, 2 * KV_COLS, wbuf)
    qs, kvs, gates_s, wins = _attn_in_proj(xs.reshape(bs * tq, d), gain, w_main, w_gate)
    shs = lambda a: a.reshape(bs, tq, a.shape[-1])
    a_all, b_all = _pool_segment_sums(cache_t, pe_t, w1_bd, math.gcd(n_pool, 32))
    nseg_s = past_len // D_CMP
    pool_s = jnp.asarray(_pool_matrix(nseg_s), BF16)
    slope_s = jnp.asarray(np.repeat(slopes, tq)[:, None])
    o_cmp, sel = _nsa_sample_cmp(page_table, shs(qs), a_all, b_all, w2_bd, pool_s, slope_s, page, past_len)
    expand_s = jnp.asarray(_expand_matrix(past_len + LANES), BF16)
    o_s, new_win_t = _nsa_sample_attend(page_table, cache_t, shs(qs), shs(gates_s), shs(kvs), shs(wins), win_past_t,
                                        o_cmp, sel, expand_s, slope_s)
    xs_new = _matmul_res(o_s.reshape(bs * tq, d), w_out_b, xs.reshape(bs * tq, d)).reshape(bs, tq, d)
    kv_s = shs(kvs).reshape(bs, tq, 4, N_KV_HEADS, HEAD_DIM)
    win_s = from_t(new_win_t, 2)
    return xp_new, xs_new, kv_p, kv_s, win_p, win_s


def kernel(x_prompt, x_sample, cache_kv, cache_win, state_ssm_re, state_ssm_im, state_conv, page_table,
           p_prompt, p_sample, norm_mix, norm_ffn, norm_ple, norm_final, w_attn_in, w_attn_out,
           cmp_pe, cmp_w1, cmp_w2, ssm_a_re, ssm_a_im, ssm_log_dt, ssm_b_re, ssm_b_im, ssm_c_re, ssm_c_im,
           ssm_d, w_glu, w_ffn_up, ffn_conv_w, ffn_conv_b, w_ffn_down, w_ple_proj, w_ple_gate):
    bsz, t, d = x_prompt.shape
    bs, tq, _ = x_sample.shape
    depth = norm_mix.shape[0]
    f = ffn_conv_w.shape[2]
    g, p = ssm_a_re.shape[1:]
    xp, xs = x_prompt, x_sample
    row = lambda v: v.reshape(1, -1).astype(F32)
    kv_p, kv_s, win_p, win_s = [], [], [], []
    sre_p, sim_p, sre_s, sim_s = [], [], [], []
    cb_p, cb_s = [], []
    for i in range(depth):
        j = i // 2
        if i % 2 == 0:
            xp, xs, kvp, kvs, wp, ws = _nsa_layer(xp, xs, cache_kv[j], cache_win[j], page_table, row(norm_mix[i]),
                                                  w_attn_in[j], w_attn_out[j], cmp_pe[j], cmp_w1[j], cmp_w2[j])
            kv_p.append(kvp)
            kv_s.append(kvs)
            win_p.append(wp)
            win_s.append(ws)
        else:
            prm = _s5_params(ssm_a_re[j], ssm_a_im[j], ssm_log_dt[j], ssm_b_re[j], ssm_b_im[j], ssm_c_re[j],
                             ssm_c_im[j], ssm_d[j], w_glu[j], 256 // SUBLANES)
            zero = jnp.zeros((bsz, g * p), F32)
            xp2, hr, hi = _s5_layer(xp.reshape(bsz * t, d), zero, zero, t, bsz, row(norm_mix[i]), prm)
            xs2, hrs, his = _s5_layer(xs.reshape(bs * tq, d), state_ssm_re[j].reshape(bs, g * p).astype(F32),
                                      state_ssm_im[j].reshape(bs, g * p).astype(F32), tq, bs, row(norm_mix[i]), prm)
            xp, xs = xp2.reshape(bsz, t, d), xs2.reshape(bs, tq, d)
            sre_p.append(hr.reshape(bsz, g, p))
            sim_p.append(hi.reshape(bsz, g, p))
            sre_s.append(hrs.reshape(bs, g, p))
            sim_s.append(his.reshape(bs, g, p))
        last = i == depth - 1
        ffn_w = (row(norm_ffn[i]), w_ffn_up[i].astype(BF16), ffn_conv_w[i].astype(F32), row(ffn_conv_b[i]),
                 w_ffn_down[i].astype(BF16), row(norm_ple[i]), w_ple_proj[i].astype(BF16),
                 w_ple_gate[i].astype(BF16), row(norm_final))
        xp2, bp = _ffn_ple(xp.reshape(bsz * t, d), p_prompt[i].reshape(bsz * t, -1),
                           jnp.zeros((bsz, CONV_W - 1, f), F32), t, bsz, last, *ffn_w)
        xs2, bs_new = _ffn_ple(xs.reshape(bs * tq, d), p_sample[i].reshape(bs * tq, -1), state_conv[i].astype(F32),
                               tq, bs, last, *ffn_w)
        xp, xs = xp2.reshape(bsz, t, d), xs2.reshape(bs, tq, d)
        cb_p.append(bp)
        cb_s.append(bs_new)
    return (xp, xs, jnp.stack(kv_p), jnp.stack(kv_s), jnp.stack(win_p), jnp.stack(win_s),
            jnp.stack(sre_p), jnp.stack(sim_p), jnp.stack(sre_s), jnp.stack(sim_s),
            jnp.stack(cb_p), jnp.stack(cb_s))
```

```python
import functools
import math

import numpy as np
import jax
import jax.numpy as jnp
from jax import lax
from jax.experimental import pallas as pl
from jax.experimental.pallas import tpu as pltpu

F32 = jnp.float32
BF16 = jnp.bfloat16

N_HEADS = 16
HEAD_DIM = 64
N_KV_HEADS = 4
GROUP = N_HEADS // N_KV_HEADS
KV_COLS = N_KV_HEADS * HEAD_DIM
L_CMP = 32
D_CMP = 16
L_SEL = 64
TOP_N = 16
WINDOW = 512
SSM_CH = 16
STATE_P = 64
CONV_W = 3
NORM_EPS = 1e-6
NEG_INF = -1e30
FORCE_BONUS = 1e4
SEL_PER_CMP = L_SEL // D_CMP

LANES = 128
SUBLANES = 8
VMEM_LIMIT = 56 * 1024 * 1024

Q_TILE = 128
SEL_CHUNK = 512
SSM_LANE_CHUNK = 512
SSM_GROUP_BLOCK = 16


def _cparams(sem):
    return pltpu.CompilerParams(dimension_semantics=sem, vmem_limit_bytes=VMEM_LIMIT)


def _const_spec(shape):
    nd = len(shape)
    return pl.BlockSpec(shape, lambda *_: (0,) * nd, pipeline_mode=pl.Buffered(1))


def _rms(x, gain):
    y = x * lax.rsqrt(jnp.mean(x * x, axis=-1, keepdims=True) + NORM_EPS)
    return y * gain


def _gelu(x):
    c = math.sqrt(2.0 / math.pi)
    return x * (0.5 * (1.0 + jnp.tanh(c * (x + 0.044715 * (x * x * x)))))


def _sigmoid(x):
    return 1.0 / (1.0 + jnp.exp(-x))


def _dot(a, b):
    return jnp.dot(a, b, preferred_element_type=F32)


def _dot_nt(a, b):
    return lax.dot_general(a, b, (((1,), (1,)), ((), ())), preferred_element_type=F32)


def _masked_softmax(s, mask):
    sm = jnp.where(mask, s, NEG_INF)
    m = jnp.max(sm, axis=-1, keepdims=True)
    e = jnp.where(mask, jnp.exp(sm - m), 0.0)
    den = jnp.sum(e, axis=-1, keepdims=True)
    return e / jnp.where(den > 0.0, den, 1.0)


def _exact_pool(x, pool_bf):
    hi = x.astype(BF16)
    r1 = x - hi.astype(F32)
    mid = r1.astype(BF16)
    lo = (r1 - mid.astype(F32)).astype(BF16)
    return _dot(hi, pool_bf) + _dot(mid, pool_bf) + _dot(lo, pool_bf)


def _topk_mask(score, k):
    s = score.T
    cand = lax.broadcasted_iota(jnp.int32, s.shape, 0).astype(F32)
    sel = jnp.zeros(s.shape, F32)
    for _ in range(k):
        m = jnp.max(s, axis=0, keepdims=True)
        first = jnp.min(jnp.where(s == m, cand, float(LANES)), axis=0, keepdims=True)
        hit = cand == first
        sel = jnp.where(hit, 1.0, sel)
        s = jnp.where(hit, -jnp.inf, s)
    return sel.T


def _slot_ids():
    return lax.broadcasted_iota(jnp.int32, (1, KV_COLS), 1) // HEAD_DIM


def _place_slot(x, src_slot, dst_slot):
    shift = (HEAD_DIM * (dst_slot - src_slot)) % KV_COLS
    y = pltpu.roll(x, shift, 1) if shift else x
    return jnp.where(_slot_ids() == dst_slot, y, 0.0)


def _selection_scores(imp, tpos):
    blk = lax.broadcasted_iota(jnp.int32, imp.shape, 1)
    cur = tpos // L_SEL
    forced = (blk == 0) | (blk == cur) | (blk == cur - 1)
    started = blk * L_SEL <= tpos
    return jnp.where(started, imp + jnp.where(forced, FORCE_BONUS, 0.0), NEG_INF)


def _attn_in_kernel(seq_len, x_ref, gain_ref, wm_ref, wg_ref, q_ref, kv_ref, gates_ref, *refs):
    tm, d = x_ref.shape
    u = _rms(x_ref[...], gain_ref[...]).astype(BF16)
    z = _dot(u, wm_ref[...])
    q_ref[...] = z[:, :d]
    kv = z[:, d:d + 4 * KV_COLS]
    kv_ref[...] = kv
    win = z[:, d + 4 * KV_COLS:d + 6 * KV_COLS]
    gates_ref[...] = _sigmoid(_dot(u, wg_ref[...]))
    if seq_len is None:
        (win_ref,) = refs
        win_ref[...] = win
        return
    ksel_ref, vsel_ref, kwin_ref, vwin_ref, kvt_ref, wint_ref = refs
    pos = (pl.program_id(0) % (seq_len // tm)) * tm + lax.broadcasted_iota(jnp.int32, (tm, 1), 0)
    aug = _position_aug(((pos // LANES) * LANES).astype(F32), (pos % LANES).astype(F32))
    lane = lax.broadcasted_iota(jnp.int32, (1, LANES), 1)
    for g in range(N_KV_HEADS):
        sl = slice(g * LANES, (g + 1) * LANES)
        ksel_ref[:, sl] = jnp.where(lane < AUG_LANE, _head_slot(kv[:, 2 * KV_COLS:3 * KV_COLS], g), aug).astype(BF16)
        kwin_ref[:, sl] = jnp.where(lane < AUG_LANE, _head_slot(win[:, :KV_COLS], g), aug).astype(BF16)
    vsel_ref[...] = kv[:, 3 * KV_COLS:4 * KV_COLS].astype(BF16)
    vwin_ref[...] = win[:, KV_COLS:].astype(BF16)
    kvt_ref[0] = kv.T
    wint_ref[0] = win.T


def _attn_in_proj(x2, gain, w_main, w_gate, seq_len=None):
    rows, d = x2.shape
    tm = min(512, rows)
    nmain = w_main.shape[1]
    row = lambda n: pl.BlockSpec((tm, n), lambda i: (i, 0))
    outs = [(d, F32), (4 * KV_COLS, F32), (LANES, F32)]
    if seq_len is None:
        outs += [(2 * KV_COLS, F32)]
    else:
        outs += [(N_KV_HEADS * LANES, BF16), (KV_COLS, BF16), (N_KV_HEADS * LANES, BF16), (KV_COLS, BF16)]
    out_specs = [row(n) for n, _ in outs]
    out_shape = [jax.ShapeDtypeStruct((rows, n), dt) for n, dt in outs]
    if seq_len is not None:
        nt = seq_len // tm
        for n in (4 * KV_COLS, 2 * KV_COLS):
            out_specs.append(pl.BlockSpec((1, n, tm), lambda i: (i // nt, 0, i % nt)))
            out_shape.append(jax.ShapeDtypeStruct((rows // seq_len, n, seq_len), F32))
    return pl.pallas_call(
        functools.partial(_attn_in_kernel, seq_len),
        grid=(rows // tm,),
        in_specs=[row(d), _const_spec((1, d)), _const_spec((d, nmain)), _const_spec((d, LANES))],
        out_specs=out_specs,
        out_shape=out_shape,
        compiler_params=_cparams(("arbitrary",)),
        name="attn_in_proj",
    )(x2, gain, w_main, w_gate)


def _matmul_res_kernel(a_ref, w_ref, x_ref, o_ref):
    o_ref[...] = x_ref[...] + _dot(a_ref[...].astype(BF16), w_ref[...])


def _matmul_res(a, w, x):
    rows, k = a.shape
    n = w.shape[1]
    tm = min(512, rows)
    return pl.pallas_call(
        _matmul_res_kernel,
        grid=(rows // tm,),
        in_specs=[pl.BlockSpec((tm, k), lambda i: (i, 0)), _const_spec((k, n)),
                  pl.BlockSpec((tm, n), lambda i: (i, 0))],
        out_specs=pl.BlockSpec((tm, n), lambda i: (i, 0)),
        out_shape=jax.ShapeDtypeStruct((rows, n), F32),
        compiler_params=_cparams(("arbitrary",)),
        name="matmul_res",
    )(a, w, x)


def _finish_compress(a, b, w2):
    nseg = a.shape[0]
    pre = a + pltpu.roll(b, nseg - 1, 0)
    out = _dot(_gelu(pre).astype(BF16), w2)
    rowi = lax.broadcasted_iota(jnp.int32, (nseg, 1), 0)
    return jnp.where(rowi < nseg - 1, out, 0.0)


def _segment_sums_kernel(x_ref, pea_ref, peb_ref, w1a_ref, w1b_ref, w2_ref, ck_ref, cv_ref, acc_a, acc_b):
    kind = pl.program_id(1)
    l = pl.program_id(2)
    nseg = acc_a.shape[0]

    @pl.when(l == 0)
    def _():
        acc_a[...] = jnp.zeros(acc_a.shape, F32)
        acc_b[...] = jnp.zeros(acc_b.shape, F32)

    xs = x_ref[0]
    acc_a[...] += _dot((xs + pea_ref[0, 0]).astype(BF16), w1a_ref[0, 0])
    acc_b[...] += _dot((xs + peb_ref[0, 0]).astype(BF16), w1b_ref[0, 0])

    @pl.when((l == D_CMP - 1) & (kind == 0))
    def _():
        kc = _finish_compress(acc_a[...], acc_b[...], w2_ref[0])
        n_i = lax.broadcasted_iota(jnp.int32, (nseg, 1), 0)
        hi = ((n_i * D_CMP) // LANES * LANES).astype(F32)
        lo = ((n_i * D_CMP) % LANES).astype(F32) + (L_CMP - 1) / 2
        aug = _position_aug(hi, lo)
        lane = lax.broadcasted_iota(jnp.int32, (1, LANES), 1)
        for g in range(N_KV_HEADS):
            ck_ref[0, :, g * LANES:(g + 1) * LANES] = jnp.where(lane < AUG_LANE, _head_slot(kc, g), aug).astype(BF16)

    @pl.when((l == D_CMP - 1) & (kind == 1))
    def _():
        cv_ref[0] = _finish_compress(acc_a[...], acc_b[...], w2_ref[0]).astype(BF16)


def _segment_sums(rows3, pe_t, w1_bd, w2_bd):
    bsz, nseg, _ = rows3.shape
    pe4 = pe_t.reshape(2, L_CMP, 1, KV_COLS)
    return pl.pallas_call(
        _segment_sums_kernel,
        grid=(bsz, 2, D_CMP),
        in_specs=[pl.BlockSpec((1, nseg, KV_COLS), lambda i, k, l: (i, 0, l * 4 + k)),
                  pl.BlockSpec((1, 1, 1, KV_COLS), lambda i, k, l: (k, l, 0, 0)),
                  pl.BlockSpec((1, 1, 1, KV_COLS), lambda i, k, l: (k, D_CMP + l, 0, 0)),
                  pl.BlockSpec((1, 1, KV_COLS, KV_COLS), lambda i, k, l: (k, l, 0, 0)),
                  pl.BlockSpec((1, 1, KV_COLS, KV_COLS), lambda i, k, l: (k, D_CMP + l, 0, 0)),
                  pl.BlockSpec((1, KV_COLS, KV_COLS), lambda i, k, l: (k, 0, 0))],
        out_specs=[pl.BlockSpec((1, nseg, N_KV_HEADS * LANES), lambda i, k, l: (i, 0, 0)),
                   pl.BlockSpec((1, nseg, KV_COLS), lambda i, k, l: (i, 0, 0))],
        out_shape=[jax.ShapeDtypeStruct((bsz, nseg, N_KV_HEADS * LANES), BF16),
                   jax.ShapeDtypeStruct((bsz, nseg, KV_COLS), BF16)],
        scratch_shapes=[pltpu.VMEM((nseg, KV_COLS), F32), pltpu.VMEM((nseg, KV_COLS), F32)],
        compiler_params=_cparams(("arbitrary", "arbitrary", "arbitrary")),
        name="segment_sums",
    )(rows3, pe4, pe4, w1_bd, w1_bd, w2_bd)


def _pool_sums_kernel(pp, page, x_ref, pe_ref, w1_ref, a_ref, b_ref, rows_s):
    ftiles = 2 * KV_COLS // LANES

    def to_rows(p, carry):
        xt = x_ref[p]
        r0 = pl.multiple_of(p * page, page)
        for c in range(ftiles):
            rows_s[c, pl.ds(r0, page), :] = xt[c * LANES:(c + 1) * LANES, :].T
        return carry

    lax.fori_loop(0, pp, to_rows, 0, unroll=4)
    nseg = pp * page // D_CMP
    tpk = KV_COLS // LANES
    for kind in range(2):
        a = jnp.zeros((nseg, KV_COLS), F32)
        b = jnp.zeros((nseg, KV_COLS), F32)
        for l in range(D_CMP):
            xs = jnp.concatenate([rows_s[kind * tpk + c, pl.ds(l, nseg, stride=D_CMP), :] for c in range(tpk)],
                                 axis=1)
            a = a + _dot((xs + pe_ref[kind, l:l + 1, :]).astype(BF16), w1_ref[kind, l])
            b = b + _dot((xs + pe_ref[kind, D_CMP + l:D_CMP + l + 1, :]).astype(BF16), w1_ref[kind, D_CMP + l])
        a_ref[:, kind * KV_COLS:(kind + 1) * KV_COLS] = a
        b_ref[:, kind * KV_COLS:(kind + 1) * KV_COLS] = b


def _pool_segment_sums(cache_t, pe_t, w1_bd, pp):
    n_pool, _, page = cache_t.shape
    assert page == LANES and n_pool % pp == 0
    nseg = pp * page // D_CMP
    return pl.pallas_call(
        functools.partial(_pool_sums_kernel, pp, page),
        grid=(n_pool // pp,),
        in_specs=[pl.BlockSpec((pp, 2 * KV_COLS, page), lambda i: (i, 0, 0)),
                  _const_spec(pe_t.shape), _const_spec(w1_bd.shape)],
        out_specs=[pl.BlockSpec((nseg, 2 * KV_COLS), lambda i: (i, 0))] * 2,
        out_shape=[jax.ShapeDtypeStruct((n_pool * page // D_CMP, 2 * KV_COLS), F32)] * 2,
        scratch_shapes=[pltpu.VMEM((2 * KV_COLS // LANES, pp * page, LANES), F32)],
        compiler_params=_cparams(("arbitrary",)),
        name="pool_segment_sums",
    )(cache_t, pe_t, w1_bd)


AUG_LANE = HEAD_DIM
N_PIECES = 3
SOFTMAX_ROWS = 32


def _position_aug(hi, lo):
    lane = lax.broadcasted_iota(jnp.int32, (1, LANES), 1)
    in_hi = (lane >= AUG_LANE) & (lane < AUG_LANE + N_PIECES)
    in_lo = (lane >= AUG_LANE + N_PIECES) & (lane < AUG_LANE + 2 * N_PIECES)
    return jnp.where(in_hi, hi, jnp.where(in_lo, lo, 0.0))


def _head_slot(x, g):
    tile = x[:, LANES * (g // 2):LANES * (g // 2 + 1)]
    return pltpu.roll(tile, HEAD_DIM, 1) if g % 2 else tile


def _for_row_blocks(n_rows, body):
    for r0 in range(0, n_rows, SOFTMAX_ROWS):
        body(r0)


def _nsa_prompt_kernel(nc, q_ref, gates_ref, ck_ref, cv_ref, ksel_ref, vsel_ref, kwin_ref, vwin_ref,
                       onehot_ref, pool_ref, qaug_ref, o_ref,
                       qa_s, s_s, s2_s, p_s, p2_s, m_s, l_s, al_s, acc_s, ocmp_s, imp_s, score_s, selb_s):
    tq = Q_TILE
    rows = GROUP * tq
    rb = SOFTMAX_ROWS
    ncp = ck_ref.shape[1]
    qb = pl.program_id(1)
    q0 = qb * tq
    gates = gates_ref[0]
    lane = lax.broadcasted_iota(jnp.int32, (1, LANES), 1)
    tpos1 = q0 + lax.broadcasted_iota(jnp.int32, (tq, 1), 0)
    row_pos = lambda r0: q0 + r0 % tq + lax.broadcasted_iota(jnp.int32, (rb, 1), 0)
    win_len = WINDOW + tq
    w0 = pl.multiple_of(jnp.maximum(q0 - WINDOW, 0), tq)
    n_past_chunks = q0 // SEL_CHUNK
    pair = lambda g: slice(LANES * (g // 2), LANES * (g // 2 + 1))
    slot = lambda g: slice(LANES * g, LANES * (g + 1))

    def softmax_once(r0, width, valid):
        sb = jnp.where(valid, s_s[pl.ds(r0, rb), :width], NEG_INF)
        m = jnp.max(sb, axis=-1, keepdims=True)
        e = jnp.exp(sb - m)
        den = jnp.sum(e, axis=-1, keepdims=True)
        p = e * jnp.where(m > 0.5 * NEG_INF, 1.0 / den, 0.0)
        p_s[pl.ds(r0, rb), :width] = p.astype(BF16)
        return p

    n_i = lax.broadcasted_iota(jnp.int32, (1, ncp), 1)
    c_end = jnp.where(n_i < nc, n_i * D_CMP + (L_CMP - 1), jnp.int32(2 ** 30))

    for g in range(N_KV_HEADS):
        for r in range(GROUP):
            h = g * GROUP + r
            qh = q_ref[0, :, LANES * (h // 2):LANES * (h // 2 + 1)] * (HEAD_DIM ** -0.5)
            if h % 2:
                qh = pltpu.roll(qh, HEAD_DIM, 1)
            qa_s[g, r * tq:(r + 1) * tq, :] = jnp.where(lane < AUG_LANE, qh, qaug_ref[h:h + 1, :]).astype(BF16)

        s_s[:, :ncp] = _dot_nt(qa_s[g], ck_ref[0, :, slot(g)])
        imp_s[...] = jnp.zeros(imp_s.shape, F32)

        def cmp_block(r0):
            p = softmax_once(r0, ncp, c_end <= row_pos(r0))
            imp_s[pl.ds(r0 % tq, rb), :] += p

        _for_row_blocks(rows, cmp_block)
        ocmp_s[g] = _dot(p_s[:, :ncp], cv_ref[0, :, pair(g)])
        score_s[g * tq:(g + 1) * tq, :] = _selection_scores(_exact_pool(imp_s[...], pool_ref[...]), tpos1)

    selb_s[...] = jnp.where(_topk_mask(score_s[...], TOP_N) > 0.5, 0.0, NEG_INF).astype(BF16)

    for g in range(N_KV_HEADS):
        sel_bias = selb_s[g * tq:(g + 1) * tq, :]
        q_full = jnp.concatenate([qa_s[g], jnp.concatenate([sel_bias] * GROUP, axis=0)], axis=1)

        m_s[...] = jnp.full(m_s.shape, NEG_INF, F32)
        l_s[...] = jnp.zeros(l_s.shape, F32)
        acc_s[...] = jnp.zeros(acc_s.shape, F32)

        def sel_scores(c, buf):
            k0 = pl.multiple_of(c * SEL_CHUNK, SEL_CHUNK)
            k_full = jnp.concatenate([ksel_ref[0, pl.ds(k0, SEL_CHUNK), slot(g)],
                                      onehot_ref[pl.ds(k0, SEL_CHUNK), :]], axis=1)
            buf[:, :SEL_CHUNK] = _dot_nt(q_full, k_full)

        def sel_consume(c, buf, causal):
            pbuf = p_s if buf is s_s else p2_s
            k0 = pl.multiple_of(c * SEL_CHUNK, SEL_CHUNK)
            kpos = k0 + lax.broadcasted_iota(jnp.int32, (1, SEL_CHUNK), 1)

            def block(r0):
                rs = pl.ds(r0, rb)
                sb = buf[rs, :SEL_CHUNK]
                if causal:
                    sb = jnp.where(kpos <= row_pos(r0), sb, NEG_INF)
                m_old = m_s[rs, :]
                m_new = jnp.maximum(m_old, jnp.max(sb, axis=-1, keepdims=True))
                alpha = jnp.exp(m_old - m_new)
                p = jnp.exp(sb - jnp.tile(m_new, (1, SEL_CHUNK // LANES)))
                l_s[rs, :] = alpha * l_s[rs, :] + jnp.sum(p, axis=-1, keepdims=True)
                m_s[rs, :] = m_new
                al_s[rs, :] = alpha
                pbuf[rs, :SEL_CHUNK] = p.astype(BF16)

            _for_row_blocks(rows, block)
            acc_s[...] = al_s[...] * acc_s[...] + _dot(pbuf[:, :SEL_CHUNK], vsel_ref[0, pl.ds(k0, SEL_CHUNK), pair(g)])

        sel_scores(0, s_s)

        def past_pair(i, carry):
            sel_scores(2 * i + 1, s2_s)
            sel_consume(2 * i, s_s, False)
            sel_scores(2 * i + 2, s_s)
            sel_consume(2 * i + 1, s2_s, False)
            return carry

        lax.fori_loop(0, n_past_chunks // 2, past_pair, 0)

        @pl.when(n_past_chunks % 2 == 0)
        def _():
            sel_consume(n_past_chunks, s_s, True)

        @pl.when(n_past_chunks % 2 == 1)
        def _():
            sel_scores(n_past_chunks, s2_s)
            sel_consume(n_past_chunks - 1, s_s, False)
            sel_consume(n_past_chunks, s2_s, True)

        o_sel = acc_s[...] / l_s[...]

        s_s[:, :win_len] = _dot_nt(qa_s[g], kwin_ref[0, pl.ds(w0, win_len), slot(g)])
        wpos = w0 + lax.broadcasted_iota(jnp.int32, (1, win_len), 1)

        def win_block(r0):
            wd = row_pos(r0) - wpos
            softmax_once(r0, win_len, (wd >= 0) & (wd < WINDOW))

        _for_row_blocks(rows, win_block)
        o_win = _dot(p_s[:, :win_len], vwin_ref[0, pl.ds(w0, win_len), pair(g)])
        o_cmp = ocmp_s[g]

        halves = []
        for r in range(GROUP):
            c0 = 3 * (g * GROUP + r)
            rs = slice(r * tq, (r + 1) * tq)
            o_r = (gates[:, c0:c0 + 1] * o_cmp[rs] + gates[:, c0 + 1:c0 + 2] * o_sel[rs]
                   + gates[:, c0 + 2:c0 + 3] * o_win[rs])
            halves.append(pltpu.roll(o_r, HEAD_DIM, 1) if (r - g) % 2 else o_r)
        for j in range(GROUP // 2):
            tile = jnp.where(lane < HEAD_DIM, halves[2 * j], halves[2 * j + 1])
            o_ref[0, :, LANES * (2 * g + j):LANES * (2 * g + j + 1)] = tile.astype(o_ref.dtype)


def _nsa_prompt_attend(q, gates, cmp_k, cmp_v, ksel, vsel, kwin, vwin, onehot, pool, qaug):
    bsz, t, d = q.shape
    nc = (t - L_CMP) // D_CMP + 1
    ncp = cmp_k.shape[1]
    rows = GROUP * Q_TILE
    width = max(ncp, SEL_CHUNK, WINDOW + Q_TILE)
    seq = lambda a: pl.BlockSpec((1,) + a.shape[1:], lambda b, i: (b, 0, 0), pipeline_mode=pl.Buffered(1))
    stat = pltpu.VMEM((rows, LANES), F32)
    return pl.pallas_call(
        functools.partial(_nsa_prompt_kernel, nc),
        grid=(bsz, t // Q_TILE),
        in_specs=[pl.BlockSpec((1, Q_TILE, d), lambda b, i: (b, i, 0)),
                  pl.BlockSpec((1, Q_TILE, LANES), lambda b, i: (b, i, 0)),
                  seq(cmp_k), seq(cmp_v), seq(ksel), seq(vsel), seq(kwin), seq(vwin),
                  _const_spec(onehot.shape), _const_spec(pool.shape), _const_spec(qaug.shape)],
        out_specs=pl.BlockSpec((1, Q_TILE, d), lambda b, i: (b, i, 0)),
        out_shape=jax.ShapeDtypeStruct((bsz, t, d), BF16),
        scratch_shapes=[pltpu.VMEM((N_KV_HEADS, rows, LANES), BF16),
                        pltpu.VMEM((rows, width), F32),
                        pltpu.VMEM((rows, SEL_CHUNK), F32),
                        pltpu.VMEM((rows, width), BF16),
                        pltpu.VMEM((rows, SEL_CHUNK), BF16),
                        stat, stat, stat, stat,
                        pltpu.VMEM((N_KV_HEADS, rows, LANES), F32),
                        pltpu.VMEM((Q_TILE, ncp), F32),
                        pltpu.VMEM((N_KV_HEADS * Q_TILE, LANES), F32),
                        pltpu.VMEM((N_KV_HEADS * Q_TILE, LANES), BF16)],
        compiler_params=_cparams(("arbitrary", "arbitrary")),
        name="nsa_prompt_attend",
    )(q, gates, cmp_k, cmp_v, ksel, vsel, kwin, vwin, onehot, pool, qaug)


def _sample_qpad(qf, tq):
    parts = []
    for g in range(N_KV_HEADS):
        piece = qf[:, g * KV_COLS:(g + 1) * KV_COLS]
        for r in range(GROUP):
            parts.append(_place_slot(piece, r, g))
    return jnp.concatenate(parts, axis=0).astype(BF16)


def _sample_rows_to_tokens(o, tq):
    chunks = []
    for g in range(N_KV_HEADS):
        chunk = jnp.zeros((tq, KV_COLS), F32)
        for r in range(GROUP):
            i0 = (g * GROUP + r) * tq
            chunk = chunk + _place_slot(o[i0:i0 + tq], g, r)
        chunks.append(chunk)
    return jnp.concatenate(chunks, axis=1)


def _nsa_sample_cmp_kernel(n_pages, past_len, tq, pt_ref, q_ref, w2_ref, pool_ref, slope_ref, *refs):
    a_refs = refs[:n_pages]
    b_refs = refs[n_pages:2 * n_pages]
    ocmp_ref, sel_ref = refs[2 * n_pages:]
    del pt_ref
    rows = N_HEADS * tq
    a = jnp.concatenate([r[0] for r in a_refs], axis=0)
    b = jnp.concatenate([r[0] for r in b_refs], axis=0)
    nseg = a.shape[0]
    kc = _finish_compress(a[:, :KV_COLS], b[:, :KV_COLS], w2_ref[0]).astype(BF16)
    vc = _finish_compress(a[:, KV_COLS:], b[:, KV_COLS:], w2_ref[1]).astype(BF16)

    qpad = _sample_qpad(q_ref[0] * (HEAD_DIM ** -0.5), tq)
    slope = slope_ref[...]
    rowi = lax.broadcasted_iota(jnp.int32, (rows, 1), 0)
    tpos = past_len + rowi % tq
    tposf = tpos.astype(F32)
    nc = (past_len + tq - L_CMP) // D_CMP + 1
    n_i = lax.broadcasted_iota(jnp.int32, (1, nseg), 1)
    c_end = jnp.where(n_i < nc, n_i * D_CMP + (L_CMP - 1), jnp.int32(2 ** 30))
    center = (n_i * D_CMP).astype(F32) + (L_CMP - 1) / 2
    s = _dot_nt(qpad, kc) - slope * (tposf - center)
    p_cmp = _masked_softmax(s, c_end <= tpos)
    ocmp_ref[0] = _dot(p_cmp.astype(BF16), vc)

    imps = []
    for g in range(N_KV_HEADS):
        i0 = g * GROUP * tq
        imp = p_cmp[i0:i0 + tq]
        for r in range(1, GROUP):
            imp = imp + p_cmp[i0 + r * tq:i0 + (r + 1) * tq]
        imps.append(imp)
    tpos_g = past_len + lax.broadcasted_iota(jnp.int32, (N_KV_HEADS * tq, 1), 0) % tq
    score = _selection_scores(_exact_pool(jnp.concatenate(imps, axis=0), pool_ref[...]), tpos_g)
    n_sel_rows = N_KV_HEADS * tq
    pad = jnp.full((LANES - n_sel_rows, LANES), NEG_INF, F32)
    sel = _topk_mask(jnp.concatenate([score, pad], axis=0), TOP_N)[:n_sel_rows]
    sel_ref[0] = jnp.concatenate([sel[g * tq:(g + 1) * tq] for g in range(N_KV_HEADS) for _ in range(GROUP)],
                                 axis=0).astype(BF16)


def _nsa_sample_cmp(page_table, q, a_all, b_all, w2_bd, pool, slope_rows, page, past_len):
    bs, tq, d = q.shape
    n_pages = page_table.shape[1]
    seg_pp = page // D_CMP
    rows = N_HEADS * tq
    a3 = a_all.reshape(-1, seg_pp, 2 * KV_COLS)
    b3 = b_all.reshape(-1, seg_pp, 2 * KV_COLS)

    def page_spec(p):
        return pl.BlockSpec((1, seg_pp, 2 * KV_COLS), lambda s, pt: (pt[s, p], 0, 0))

    cst = lambda shape: pl.BlockSpec(shape, lambda s, pt: (0,) * len(shape))
    grid_spec = pltpu.PrefetchScalarGridSpec(
        num_scalar_prefetch=1,
        grid=(bs,),
        in_specs=[pl.BlockSpec((1, tq, d), lambda s, pt: (s, 0, 0)),
                  cst(w2_bd.shape), cst(pool.shape), cst(slope_rows.shape)]
                 + [page_spec(p) for p in range(n_pages)] * 2,
        out_specs=[pl.BlockSpec((1, rows, KV_COLS), lambda s, pt: (s, 0, 0)),
                   pl.BlockSpec((1, rows, LANES), lambda s, pt: (s, 0, 0))],
    )
    return pl.pallas_call(
        functools.partial(_nsa_sample_cmp_kernel, n_pages, past_len, tq),
        grid_spec=grid_spec,
        out_shape=[jax.ShapeDtypeStruct((bs, rows, KV_COLS), F32),
                   jax.ShapeDtypeStruct((bs, rows, LANES), BF16)],
        compiler_params=_cparams(("arbitrary",)),
        name="nsa_sample_cmp",
    )(page_table, q, w2_bd, pool, slope_rows, *([a3] * n_pages), *([b3] * n_pages))


def _nsa_sample_attend_kernel(n_pages, page, tq, pt_ref, q_ref, gates_ref, kvn_ref, winn_ref, winp_ref,
                              ocmp_ref, sel_ref, expand_ref, slope_ref, *refs):
    page_refs = refs[:n_pages]
    o_ref, newwin_ref = refs[n_pages:]
    del pt_ref
    rows = N_HEADS * tq
    past_len = n_pages * page
    wbuf = winp_ref.shape[2]
    qpad = _sample_qpad(q_ref[0] * (HEAD_DIM ** -0.5), tq)
    slope = slope_ref[...]
    rowi = lax.broadcasted_iota(jnp.int32, (rows, 1), 0)
    tpos = past_len + rowi % tq
    tposf = tpos.astype(F32)
    newpos = past_len + lax.broadcasted_iota(jnp.int32, (1, LANES), 1)
    pad_rows = lambda a: jnp.concatenate([a, jnp.zeros((LANES - tq, a.shape[1]), F32)], axis=0)

    kvn = pad_rows(kvn_ref[0])
    k_new = kvn[:, 2 * KV_COLS:3 * KV_COLS].astype(BF16)
    v_new = kvn[:, 3 * KV_COLS:4 * KV_COLS].astype(BF16)
    scores = [_dot(qpad, r[0, :KV_COLS, :].astype(BF16)) for r in page_refs]
    scores.append(_dot_nt(qpad, k_new))
    s = jnp.concatenate(scores, axis=1)
    nk = past_len + LANES
    kpos = lax.broadcasted_iota(jnp.int32, (1, nk), 1)
    picked = _dot(sel_ref[0], expand_ref[...]) > 0.5
    s = s - slope * (tposf - kpos.astype(F32))
    p = _masked_softmax(s, picked & (kpos <= tpos)).astype(BF16)
    o_sel = _dot(p[:, past_len:], v_new)
    for i, r in enumerate(page_refs):
        o_sel = o_sel + _dot_nt(p[:, i * page:(i + 1) * page], r[0, KV_COLS:, :].astype(BF16))

    winp = winp_ref[0]
    winn = pad_rows(winn_ref[0])
    winn_t = pltpu.roll(winn.T, LANES - tq, 1)
    lane_w = lax.broadcasted_iota(jnp.int32, (1, wbuf), 1)
    newwin_ref[0] = jnp.where(lane_w >= wbuf - tq, jnp.concatenate([winn_t] * (wbuf // LANES), axis=1),
                              pltpu.roll(winp, wbuf - tq, 1))
    sp = _dot(qpad, winp[:KV_COLS, :].astype(BF16))
    sn = _dot_nt(qpad, winn[:, :KV_COLS].astype(BF16))
    wpos_p = past_len - wbuf + lax.broadcasted_iota(jnp.int32, (1, wbuf), 1)
    wpos = jnp.concatenate([wpos_p, newpos], axis=1)
    wd = tpos - wpos
    s = jnp.concatenate([sp, sn], axis=1) - slope * wd.astype(F32)
    p = _masked_softmax(s, (wd >= 0) & (wd < WINDOW) & (wpos >= 0)).astype(BF16)
    o_win = _dot_nt(p[:, :wbuf], winp[KV_COLS:, :].astype(BF16)) + _dot(p[:, wbuf:], winn[:, KV_COLS:].astype(BF16))

    gates = gates_ref[0]
    o_cmp = ocmp_ref[0]
    merged = []
    for h in range(N_HEADS):
        rs = slice(h * tq, (h + 1) * tq)
        merged.append(gates[:, 3 * h:3 * h + 1] * o_cmp[rs] + gates[:, 3 * h + 1:3 * h + 2] * o_sel[rs]
                      + gates[:, 3 * h + 2:3 * h + 3] * o_win[rs])
    o_ref[0] = _sample_rows_to_tokens(jnp.concatenate(merged, axis=0), tq).astype(o_ref.dtype)


def _nsa_sample_attend(page_table, cache_t, q, gates, kv_new, win_new, win_past_t, o_cmp, sel, expand, slope_rows):
    bs, tq, d = q.shape
    n_pages = page_table.shape[1]
    page = cache_t.shape[2]
    wbuf = win_past_t.shape[2]
    rows = N_HEADS * tq

    def page_spec(p):
        return pl.BlockSpec((1, 2 * KV_COLS, page), lambda s, pt: (pt[s, p], 1, 0))

    per = lambda r, n: pl.BlockSpec((1, r, n), lambda s, pt: (s, 0, 0))
    cst = lambda shape: pl.BlockSpec(shape, lambda s, pt: (0,) * len(shape))
    grid_spec = pltpu.PrefetchScalarGridSpec(
        num_scalar_prefetch=1,
        grid=(bs,),
        in_specs=[per(tq, d), per(tq, LANES), per(tq, 4 * KV_COLS), per(tq, 2 * KV_COLS),
                  per(2 * KV_COLS, wbuf), per(rows, KV_COLS), per(rows, LANES),
                  cst(expand.shape), cst(slope_rows.shape)]
                 + [page_spec(p) for p in range(n_pages)],
        out_specs=[per(tq, d), per(2 * KV_COLS, wbuf)],
    )
    return pl.pallas_call(
        functools.partial(_nsa_sample_attend_kernel, n_pages, page, tq),
        grid_spec=grid_spec,
        out_shape=[jax.ShapeDtypeStruct((bs, tq, d), BF16),
                   jax.ShapeDtypeStruct((bs, 2 * KV_COLS, wbuf), F32)],
        compiler_params=_cparams(("arbitrary",)),
        name="nsa_sample_attend",
    )(page_table, q, gates, kv_new, win_new, win_past_t, o_cmp, sel, expand, slope_rows,
      *([cache_t] * n_pages))


def _cmul_add(ar, ai, hr, hi, xr, xi):
    return ar * hr - ai * hi + xr, ar * hi + ai * hr + xi


def _s5_kernel(mode, tm, x_ref, h0r_ref, h0i_ref, gain_ref, bre_ref, bim_ref, are_ref, aim_ref,
               pwr_ref, pwi_ref, cre_ref, cim_ref, d_ref, wglu_ref,
               xo_ref, hfr_ref, hfi_ref, hr_s, hi_s, cr_s, ci_s):
    d = x_ref.shape[1]
    nstate = cr_s.shape[1]
    nblk = bre_ref.shape[0]
    cb = d // nblk
    sb = nstate // nblk
    tpb = sb // LANES
    x = x_ref[...]
    uf = _rms(x, gain_ref[...])
    ub = uf.astype(BF16)

    if mode == "carry":
        nj = tm // SUBLANES
        cs = nj + SUBLANES
        dst = [(s * cs, s * nj, nj) for s in range(SUBLANES)]
    else:
        dst = [(0, 0, tm)]
    for j in range(nblk):
        uj = ub[:, j * cb:(j + 1) * cb]
        br = _dot(uj, bre_ref[j])
        bi = _dot(uj, bim_ref[j])
        for k in range(tpb):
            for (o, i0, n) in dst:
                hr_s[j * tpb + k, o:o + n, :] = br[i0:i0 + n, k * LANES:(k + 1) * LANES]
                hi_s[j * tpb + k, o:o + n, :] = bi[i0:i0 + n, k * LANES:(k + 1) * LANES]

    lc = SSM_LANE_CHUNK
    par = lc // LANES
    if mode == "carry":
        @pl.when(pl.program_id(1) == 0)
        def _():
            cr_s[...] = jnp.broadcast_to(h0r_ref[0], cr_s.shape)
            ci_s[...] = jnp.broadcast_to(h0i_ref[0], ci_s.shape)

        sub = lax.broadcasted_iota(jnp.int32, (SUBLANES, 1), 0)
        for c in range(nstate // lc):
            sl = slice(c * lc, (c + 1) * lc)
            ar = jnp.broadcast_to(are_ref[:, sl], (SUBLANES, lc))
            ai = jnp.broadcast_to(aim_ref[:, sl], (SUBLANES, lc))

            tiles = range(c * par, (c + 1) * par)
            lane = lambda v, k: v[:, k * LANES:(k + 1) * LANES]

            def p1(jj, st):
                idx = pl.ds(jj, SUBLANES, stride=cs)
                out = []
                for k, q in enumerate(tiles):
                    nr, ni = _cmul_add(lane(ar, k), lane(ai, k), st[2 * k], st[2 * k + 1],
                                       hr_s[q, idx, :], hi_s[q, idx, :])
                    hr_s[q, idx, :] = nr
                    hi_s[q, idx, :] = ni
                    out += [nr, ni]
                return tuple(out)

            z = jnp.zeros((SUBLANES, LANES), F32)
            lf = lax.fori_loop(0, nj, p1, (z,) * (2 * par))
            lfr = jnp.concatenate(lf[0::2], axis=1)
            lfi = jnp.concatenate(lf[1::2], axis=1)

            alr = pwr_ref[nj - 1:nj, sl]
            ali = pwi_ref[nj - 1:nj, sl]
            inr = cr_s[0:1, sl]
            ini = ci_s[0:1, sl]
            hin_r = jnp.zeros((SUBLANES, lc), F32)
            hin_i = jnp.zeros((SUBLANES, lc), F32)
            for s in range(SUBLANES):
                hin_r = jnp.where(sub == s, inr, hin_r)
                hin_i = jnp.where(sub == s, ini, hin_i)
                inr, ini = _cmul_add(alr, ali, inr, ini, lfr[s:s + 1], lfi[s:s + 1])
            cr_s[:, sl] = jnp.broadcast_to(inr, (SUBLANES, lc))
            ci_s[:, sl] = jnp.broadcast_to(ini, (SUBLANES, lc))

            def p2(jj, carry):
                idx = pl.ds(jj, SUBLANES, stride=cs)
                pr = pwr_ref[pl.ds(jj, 1), sl]
                pi_ = pwi_ref[pl.ds(jj, 1), sl]
                for k, q in enumerate(tiles):
                    nr, ni = _cmul_add(lane(pr, k), lane(pi_, k), lane(hin_r, k), lane(hin_i, k),
                                       hr_s[q, idx, :], hi_s[q, idx, :])
                    hr_s[q, idx, :] = nr
                    hi_s[q, idx, :] = ni
                return carry

            lax.fori_loop(0, nj, p2, 0, unroll=4)

        @pl.when(pl.program_id(1) == pl.num_programs(1) - 1)
        def _():
            hfr_ref[0] = cr_s[0:1, :]
            hfi_ref[0] = ci_s[0:1, :]
    else:
        nseq = tm // mode
        for q in range(nstate // LANES):
            sl = slice(q * LANES, (q + 1) * LANES)
            ar = jnp.broadcast_to(are_ref[:, sl], (nseq, LANES))
            ai = jnp.broadcast_to(aim_ref[:, sl], (nseq, LANES))
            sr = h0r_ref[:, sl]
            si = h0i_ref[:, sl]
            for t in range(mode):
                idx = pl.ds(t, nseq, stride=mode)
                sr, si = _cmul_add(ar, ai, sr, si, hr_s[q, idx, :], hi_s[q, idx, :])
                hr_s[q, idx, :] = sr
                hi_s[q, idx, :] = si
            hfr_ref[:, sl] = sr
            hfi_ref[:, sl] = si

    def block_rows(ref, j):
        cols = [jnp.concatenate([ref[j * tpb + k, o:o + n, :] for (o, _, n) in dst], axis=0) for k in range(tpb)]
        return jnp.concatenate(cols, axis=1).astype(BF16)

    ys = []
    for j in range(nblk):
        ys.append(_dot(block_rows(hr_s, j), cre_ref[j]) - _dot(block_rows(hi_s, j), cim_ref[j]))
    y = jnp.concatenate(ys, axis=1) + d_ref[...] * uf
    ab = _dot(_gelu(y).astype(BF16), wglu_ref[...])
    xo_ref[...] = x + ab[:, :d] * _sigmoid(ab[:, d:])


def _s5_layer(x2, h0r, h0i, seq_len, n_seq, gain, prm):
    rows, d = x2.shape
    nstate = h0r.shape[1]
    if seq_len >= 256:
        mode, tm = "carry", 256
        nt = seq_len // tm
        grid = (n_seq, nt)
        xmap = lambda b, i: (b * nt + i, 0)
        h0_spec = pl.BlockSpec((1, 1, nstate), lambda b, i: (b, 0, 0))
        hf_spec = pl.BlockSpec((1, 1, nstate), lambda b, i: (b, 0, 0))
        h0r, h0i = h0r[:, None, :], h0i[:, None, :]
        hf_shape = (n_seq, 1, nstate)
        srows = (tm // SUBLANES + SUBLANES) * SUBLANES
        sem = ("arbitrary", "arbitrary")
        cst = lambda shape: pl.BlockSpec(shape, lambda b, i: (0,) * len(shape), pipeline_mode=pl.Buffered(1))
    else:
        mode = seq_len
        tm = min(256, rows)
        nseq_t = tm // seq_len
        grid = (rows // tm,)
        xmap = lambda i: (i, 0)
        h0_spec = pl.BlockSpec((nseq_t, nstate), lambda i: (i, 0))
        hf_spec = pl.BlockSpec((nseq_t, nstate), lambda i: (i, 0))
        hf_shape = (n_seq, nstate)
        srows = tm
        sem = ("arbitrary",)
        cst = lambda shape: pl.BlockSpec(shape, lambda i: (0,) * len(shape), pipeline_mode=pl.Buffered(1))
    pw_r, pw_i = prm["pw_r"], prm["pw_i"]
    consts = [gain, prm["b_r"], prm["b_i"], prm["a_r"], prm["a_i"], pw_r, pw_i, prm["c_r"], prm["c_i"],
              prm["d"], prm["w_glu"]]
    outs = pl.pallas_call(
        functools.partial(_s5_kernel, mode, tm),
        grid=grid,
        in_specs=[pl.BlockSpec((tm, d), xmap), h0_spec, h0_spec] + [cst(c.shape) for c in consts],
        out_specs=[pl.BlockSpec((tm, d), xmap), hf_spec, hf_spec],
        out_shape=[jax.ShapeDtypeStruct((rows, d), F32), jax.ShapeDtypeStruct(hf_shape, F32),
                   jax.ShapeDtypeStruct(hf_shape, F32)],
        scratch_shapes=[pltpu.VMEM((nstate // LANES, srows, LANES), F32),
                        pltpu.VMEM((nstate // LANES, srows, LANES), F32),
                        pltpu.VMEM((SUBLANES, nstate), F32), pltpu.VMEM((SUBLANES, nstate), F32)],
        compiler_params=_cparams(sem),
        name="s5_layer",
    )(x2, h0r, h0i, *consts)
    xo, hfr, hfi = outs
    return xo, hfr.reshape(n_seq, nstate), hfi.reshape(n_seq, nstate)


def _s5_params(a_re, a_im, log_dt, b_re, b_im, c_re, c_im, d_skip, w_glu, n_pow):
    g, p = a_re.shape
    ch = b_re.shape[2]
    lr, li = a_re.astype(F32), a_im.astype(F32)
    dt = jnp.exp(log_dt.astype(F32))[:, None]
    mag = jnp.exp(lr * dt)
    ar, ai = mag * jnp.cos(li * dt), mag * jnp.sin(li * dt)
    den = lr * lr + li * li
    kr = ((ar - 1.0) * lr + ai * li) / den
    ki = (ai * lr - (ar - 1.0) * li) / den
    br, bi = b_re.astype(F32), b_im.astype(F32)
    bbar_r = kr[:, :, None] * br - ki[:, :, None] * bi
    bbar_i = kr[:, :, None] * bi + ki[:, :, None] * br
    nblk = g // SSM_GROUP_BLOCK
    eye = jnp.eye(SSM_GROUP_BLOCK, dtype=F32)

    def b_blocks(m):
        m = m.reshape(nblk, SSM_GROUP_BLOCK, p, ch)
        return jnp.einsum("jgpc,gh->jgchp", m, eye).reshape(nblk, SSM_GROUP_BLOCK * ch, SSM_GROUP_BLOCK * p).astype(BF16)

    def c_blocks(m):
        m = m.reshape(nblk, SSM_GROUP_BLOCK, ch, p)
        return jnp.einsum("jgcp,gh->jgphc", m, eye).reshape(nblk, SSM_GROUP_BLOCK * p, SSM_GROUP_BLOCK * ch).astype(BF16)

    ar, ai = ar.reshape(1, g * p), ai.reshape(1, g * p)
    pw_r, pw_i = [ar], [ai]
    for _ in range(n_pow - 1):
        pw_r, pw_i = pw_r + [pw_r[-1] * ar - pw_i[-1] * ai], pw_i + [pw_r[-1] * ai + pw_i[-1] * ar]
    return dict(b_r=b_blocks(bbar_r), b_i=b_blocks(bbar_i), a_r=ar, a_i=ai,
                pw_r=jnp.concatenate(pw_r, axis=0), pw_i=jnp.concatenate(pw_i, axis=0),
                c_r=c_blocks(c_re.astype(F32)), c_i=c_blocks(c_im.astype(F32)),
                d=d_skip.astype(F32).reshape(1, -1), w_glu=w_glu.astype(BF16))


def _ffn_ple_kernel(mode, final_norm, x_ref, p_ref, b1_ref, b2_ref, gffn_ref, wup_ref, cw_ref, cb_ref,
                    wdown_ref, gple_ref, wproj_ref, wgate_ref, gfin_ref, xo_ref, tail_ref, carry_ref):
    tm = x_ref.shape[0]
    f = cw_ref.shape[1]
    x = x_ref[...]
    u = _rms(x, gffn_ref[...]).astype(BF16)
    hg = _dot(u, wup_ref[...])
    h = hg[:, :f]
    gate_branch = hg[:, f:]
    rowi = lax.broadcasted_iota(jnp.int32, (tm, 1), 0)
    r1 = pltpu.roll(h, 1, 0)
    r2 = pltpu.roll(h, 2, 0)
    if mode == "carry":
        @pl.when(pl.program_id(1) == 0)
        def _():
            carry_ref[SUBLANES - 2:SUBLANES, :] = b1_ref[0]
        c0 = carry_ref[SUBLANES - 2:SUBLANES - 1, :]
        c1 = carry_ref[SUBLANES - 1:SUBLANES, :]
        hm1 = jnp.where(rowi == 0, c1, r1)
        hm2 = jnp.where(rowi == 0, c0, jnp.where(rowi == 1, c1, r2))
        carry_ref[...] = h[tm - SUBLANES:tm, :]
        tail_ref[...] = h[tm - SUBLANES:tm, :]
    else:
        t = rowi % mode
        hm1 = jnp.where(t == 0, b1_ref[...], r1)
        hm2 = jnp.where(t < 2, b2_ref[...], r2)
        tail_ref[...] = h
    conv = cb_ref[...] + cw_ref[0:1, :] * hm2 + cw_ref[1:2, :] * hm1 + cw_ref[2:3, :] * h
    act = (_gelu(conv) * gate_branch).astype(BF16)
    x1 = x + _dot(act, wdown_ref[...])
    gate = _sigmoid(_dot(_rms(x1, gple_ref[...]).astype(BF16), wgate_ref[...]))
    x2 = x1 + _dot(p_ref[...].astype(BF16), wproj_ref[...]) * gate
    xo_ref[...] = _rms(x2, gfin_ref[...]) if final_norm else x2


def _ffn_ple(x2, p2, buf, seq_len, n_seq, final_norm, gffn, wup, cw, cb, wdown, gple, wproj, wgate, gfin):
    rows, d = x2.shape
    f = cw.shape[1]
    ple = p2.shape[1]
    consts = [gffn, wup, cw, cb, wdown, gple, wproj, wgate, gfin]
    if seq_len >= 256:
        mode, tm = "carry", 256
        nt = seq_len // tm
        grid = (n_seq, nt)
        rmap = lambda b, i: (b * nt + i, 0)
        b1, b2 = buf, buf
        bspec = pl.BlockSpec((1, CONV_W - 1, f), lambda b, i: (b, 0, 0))
        tail_rows = SUBLANES
        sem = ("arbitrary", "arbitrary")
        cst = lambda shape: pl.BlockSpec(shape, lambda b, i: (0,) * len(shape), pipeline_mode=pl.Buffered(1))
    else:
        mode = seq_len
        tm = min(256, rows)
        grid = (rows // tm,)
        rmap = lambda i: (i, 0)
        zero = jnp.zeros((n_seq, seq_len, f), F32)
        b1 = zero.at[:, 0].set(buf[:, 1]).reshape(rows, f)
        b2 = zero.at[:, 0].set(buf[:, 0]).at[:, 1].set(buf[:, 1]).reshape(rows, f)
        bspec = pl.BlockSpec((tm, f), rmap)
        tail_rows = tm
        sem = ("arbitrary",)
        cst = lambda shape: pl.BlockSpec(shape, lambda i: (0,) * len(shape), pipeline_mode=pl.Buffered(1))
    n_tiles = rows // tm
    xo, tail = pl.pallas_call(
        functools.partial(_ffn_ple_kernel, mode, final_norm),
        grid=grid,
        in_specs=[pl.BlockSpec((tm, d), rmap), pl.BlockSpec((tm, ple), rmap), bspec, bspec]
                 + [cst(c.shape) for c in consts],
        out_specs=[pl.BlockSpec((tm, d), rmap), pl.BlockSpec((tail_rows, f), rmap)],
        out_shape=[jax.ShapeDtypeStruct((rows, d), F32), jax.ShapeDtypeStruct((n_tiles * tail_rows, f), F32)],
        scratch_shapes=[pltpu.VMEM((SUBLANES, f), F32)],
        compiler_params=_cparams(sem),
        name="ffn_ple",
    )(x2, p2, b1, b2, *consts)
    if mode == "carry":
        new_buf = tail.reshape(n_seq, n_tiles // n_seq, SUBLANES, f)[:, -1, SUBLANES - (CONV_W - 1):, :]
    else:
        new_buf = tail.reshape(n_seq, seq_len, f)[:, seq_len - (CONV_W - 1):, :]
    return xo, new_buf


def _block_diag_heads(w):
    eye = jnp.eye(N_KV_HEADS, dtype=w.dtype)
    out = jnp.einsum("...de,gh->...gdhe", w, eye)
    return out.reshape(*w.shape[:-2], KV_COLS, KV_COLS)


def _slopes():
    return np.exp2(-8.0 * np.arange(1, N_HEADS + 1, dtype=np.float64) / N_HEADS).astype(np.float32)


def _query_aug(slopes):
    rest = jnp.asarray(slopes, F32)
    aug = jnp.zeros((N_HEADS, LANES), F32)
    for i in range(N_PIECES):
        piece = rest.astype(BF16).astype(F32)
        rest = rest - piece
        aug = aug.at[:, AUG_LANE + i].set(piece).at[:, AUG_LANE + N_PIECES + i].set(piece)
    return aug


def _expand_matrix(n_keys):
    return (np.arange(LANES)[:, None] == (np.arange(n_keys)[None, :] // L_SEL)).astype(np.float32)


def _pool_matrix(n_cmp):
    return (np.arange(n_cmp)[:, None] // SEL_PER_CMP == np.arange(LANES)[None, :]).astype(np.float32)


def _nsa_layer(xp, xs, cache_l, cache_win_l, page_table, gain, w_in, w_out, pe, w1, w2):
    bsz, t, d = xp.shape
    bs, tq, _ = xs.shape
    page = cache_l.shape[1]
    n_pages = page_table.shape[1]
    past_len = n_pages * page
    n_pool = cache_l.shape[0]
    assert t % SEL_CHUNK == 0 and t // L_SEL <= LANES and t >= WINDOW + Q_TILE
    assert (past_len + LANES) // L_SEL <= LANES and page % D_CMP == 0 and tq <= SUBLANES

    qcols = N_HEADS * HEAD_DIM
    nmain = qcols + 6 * KV_COLS
    w_main = w_in[:, :nmain].astype(BF16)
    w_gate = jnp.pad(w_in[:, nmain:], ((0, 0), (0, LANES - (w_in.shape[1] - nmain)))).astype(BF16)
    w_out_b = w_out.astype(BF16)
    pe_t = jnp.tile(pe, (1, 1, N_KV_HEADS))
    w1_bd = _block_diag_heads(w1).astype(BF16)
    w2_bd = _block_diag_heads(w2).astype(BF16)
    slopes = _slopes()

    q, kv, gates, ksel, vsel, kwin, vwin, kv_t, win_t = _attn_in_proj(
        xp.reshape(bsz * t, d), gain, w_main, w_gate, seq_len=t)
    sh = lambda a: a.reshape(bsz, t, a.shape[-1])
    cmp_k, cmp_v = _segment_sums(kv.reshape(bsz, t // D_CMP, D_CMP * 4 * KV_COLS), pe_t, w1_bd, w2_bd)
    onehot = jnp.asarray(_expand_matrix(t).T, BF16)
    pool = jnp.asarray(_pool_matrix(t // D_CMP), BF16)
    o = _nsa_prompt_attend(sh(q), sh(gates), cmp_k, cmp_v, sh(ksel), sh(vsel), sh(kwin), sh(vwin), onehot, pool,
                           jnp.asarray(_query_aug(slopes)))
    xp_new = _matmul_res(o.reshape(bsz * t, d), w_out_b, xp.reshape(bsz * t, d)).reshape(bsz, t, d)
    keep = min(WINDOW, t)
    from_t = lambda a, n: jnp.transpose(a.reshape(a.shape[0], n, N_KV_HEADS, HEAD_DIM, a.shape[-1]), (0, 4, 1, 2, 3))
    kv_p = from_t(kv_t, 4)
    win_p = from_t(win_t[:, :, t - keep:], 2)

    cache_t = jnp.transpose(cache_l, (0, 2, 3, 4, 1)).reshape(n_pool, 4 * KV_COLS, page)
    wbuf = cache_win_l.shape[1]
    win_past_t = jnp.transpose(cache_win_l, (0, 2, 3, 4, 1)).reshape(bs, 2 * KV_COLS, wbuf)
    qs, kvs, gates_s, wins = _attn_in_proj(xs.reshape(bs * tq, d), gain, w_main, w_gate)
    shs = lambda a: a.reshape(bs, tq, a.shape[-1])
    a_all, b_all = _pool_segment_sums(cache_t, pe_t, w1_bd, math.gcd(n_pool, 32))
    nseg_s = past_len // D_CMP
    pool_s = jnp.asarray(_pool_matrix(nseg_s), BF16)
    slope_s = jnp.asarray(np.repeat(slopes, tq)[:, None])
    o_cmp, sel = _nsa_sample_cmp(page_table, shs(qs), a_all, b_all, w2_bd, pool_s, slope_s, page, past_len)
    expand_s = jnp.asarray(_expand_matrix(past_len + LANES), BF16)
    o_s, new_win_t = _nsa_sample_attend(page_table, cache_t, shs(qs), shs(gates_s), shs(kvs), shs(wins), win_past_t,
                                        o_cmp, sel, expand_s, slope_s)
    xs_new = _matmul_res(o_s.reshape(bs * tq, d), w_out_b, xs.reshape(bs * tq, d)).reshape(bs, tq, d)
    kv_s = shs(kvs).reshape(bs, tq, 4, N_KV_HEADS, HEAD_DIM)
    win_s = from_t(new_win_t, 2)
    return xp_new, xs_new, kv_p, kv_s, win_p, win_s


def kernel(x_prompt, x_sample, cache_kv, cache_win, state_ssm_re, state_ssm_im, state_conv, page_table,
           p_prompt, p_sample, norm_mix, norm_ffn, norm_ple, norm_final, w_attn_in, w_attn_out,
           cmp_pe, cmp_w1, cmp_w2, ssm_a_re, ssm_a_im, ssm_log_dt, ssm_b_re, ssm_b_im, ssm_c_re, ssm_c_im,
           ssm_d, w_glu, w_ffn_up, ffn_conv_w, ffn_conv_b, w_ffn_down, w_ple_proj, w_ple_gate):
    bsz, t, d = x_prompt.shape
    bs, tq, _ = x_sample.shape
    depth = norm_mix.shape[0]
    f = ffn_conv_w.shape[2]
    g, p = ssm_a_re.shape[1:]
    xp, xs = x_prompt, x_sample
    row = lambda v: v.reshape(1, -1).astype(F32)
    kv_p, kv_s, win_p, win_s = [], [], [], []
    sre_p, sim_p, sre_s, sim_s = [], [], [], []
    cb_p, cb_s = [], []
    for i in range(depth):
        j = i // 2
        if i % 2 == 0:
            xp, xs, kvp, kvs, wp, ws = _nsa_layer(xp, xs, cache_kv[j], cache_win[j], page_table, row(norm_mix[i]),
                                                  w_attn_in[j], w_attn_out[j], cmp_pe[j], cmp_w1[j], cmp_w2[j])
            kv_p.append(kvp)
            kv_s.append(kvs)
            win_p.append(wp)
            win_s.append(ws)
        else:
            prm = _s5_params(ssm_a_re[j], ssm_a_im[j], ssm_log_dt[j], ssm_b_re[j], ssm_b_im[j], ssm_c_re[j],
                             ssm_c_im[j], ssm_d[j], w_glu[j], 256 // SUBLANES)
            zero = jnp.zeros((bsz, g * p), F32)
            xp2, hr, hi = _s5_layer(xp.reshape(bsz * t, d), zero, zero, t, bsz, row(norm_mix[i]), prm)
            xs2, hrs, his = _s5_layer(xs.reshape(bs * tq, d), state_ssm_re[j].reshape(bs, g * p).astype(F32),
                                      state_ssm_im[j].reshape(bs, g * p).astype(F32), tq, bs, row(norm_mix[i]), prm)
            xp, xs = xp2.reshape(bsz, t, d), xs2.reshape(bs, tq, d)
            sre_p.append(hr.reshape(bsz, g, p))
            sim_p.append(hi.reshape(bsz, g, p))
            sre_s.append(hrs.reshape(bs, g, p))
            sim_s.append(his.reshape(bs, g, p))
        last = i == depth - 1
        ffn_w = (row(norm_ffn[i]), w_ffn_up[i].astype(BF16), ffn_conv_w[i].astype(F32), row(ffn_conv_b[i]),
                 w_ffn_down[i].astype(BF16), row(norm_ple[i]), w_ple_proj[i].astype(BF16),
                 w_ple_gate[i].astype(BF16), row(norm_final))
        xp2, bp = _ffn_ple(xp.reshape(bsz * t, d), p_prompt[i].reshape(bsz * t, -1),
                           jnp.zeros((bsz, CONV_W - 1, f), F32), t, bsz, last, *ffn_w)
        xs2, bs_new = _ffn_ple(xs.reshape(bs * tq, d), p_sample[i].reshape(bs * tq, -1), state_conv[i].astype(F32),
                               tq, bs, last, *ffn_w)
        xp, xs = xp2.reshape(bsz, t, d), xs2.reshape(bs, tq, d)
        cb_p.append(bp)
        cb_s.append(bs_new)
    return (xp, xs, jnp.stack(kv_p), jnp.stack(kv_s), jnp.stack(win_p), jnp.stack(win_s),
            jnp.stack(sre_p), jnp.stack(sim_p), jnp.stack(sre_s), jnp.stack(sim_s),
            jnp.stack(cb_p), jnp.stack(cb_s))
```

```python
import functools
import math

import numpy as np
import jax
import jax.numpy as jnp
from jax import lax
from jax.experimental import pallas as pl
from jax.experimental.pallas import tpu as pltpu

F32 = jnp.float32
BF16 = jnp.bfloat16

N_HEADS = 16
HEAD_DIM = 64
N_KV_HEADS = 4
GROUP = N_HEADS // N_KV_HEADS
KV_COLS = N_KV_HEADS * HEAD_DIM
L_CMP = 32
D_CMP = 16
L_SEL = 64
TOP_N = 16
WINDOW = 512
SSM_CH = 16
STATE_P = 64
CONV_W = 3
NORM_EPS = 1e-6
NEG_INF = -1e30
FORCE_BONUS = 1e4
SEL_PER_CMP = L_SEL // D_CMP

LANES = 128
SUBLANES = 8
VMEM_LIMIT = 56 * 1024 * 1024

Q_TILE = 128
SEL_CHUNK = 512
SSM_LANE_CHUNK = 512
SSM_GROUP_BLOCK = 16
FFN_CHUNKS = 2


def _cparams(sem):
    return pltpu.CompilerParams(dimension_semantics=sem, vmem_limit_bytes=VMEM_LIMIT)


def _const_spec(shape):
    nd = len(shape)
    return pl.BlockSpec(shape, lambda *_: (0,) * nd, pipeline_mode=pl.Buffered(1))


def _rms(x, gain):
    y = x * lax.rsqrt(jnp.mean(x * x, axis=-1, keepdims=True) + NORM_EPS)
    return y * gain


def _gelu(x):
    c = math.sqrt(2.0 / math.pi)
    return x * (0.5 * (1.0 + jnp.tanh(c * (x + 0.044715 * (x * x * x)))))


def _sigmoid(x):
    return 1.0 / (1.0 + jnp.exp(-x))


def _dot(a, b):
    return jnp.dot(a, b, preferred_element_type=F32)


def _dot_nt(a, b):
    return lax.dot_general(a, b, (((1,), (1,)), ((), ())), preferred_element_type=F32)


def _masked_softmax(s, mask):
    sm = jnp.where(mask, s, NEG_INF)
    m = jnp.max(sm, axis=-1, keepdims=True)
    e = jnp.where(mask, jnp.exp(sm - m), 0.0)
    den = jnp.sum(e, axis=-1, keepdims=True)
    return e / jnp.where(den > 0.0, den, 1.0)


def _exact_pool(x, pool_bf):
    hi = x.astype(BF16)
    r1 = x - hi.astype(F32)
    mid = r1.astype(BF16)
    lo = (r1 - mid.astype(F32)).astype(BF16)
    return _dot(hi, pool_bf) + _dot(mid, pool_bf) + _dot(lo, pool_bf)


def _exact_rows(sel_bf, x):
    hi = x.astype(BF16)
    r1 = x - hi.astype(F32)
    mid = r1.astype(BF16)
    lo = (r1 - mid.astype(F32)).astype(BF16)
    return _dot(sel_bf, hi) + _dot(sel_bf, mid) + _dot(sel_bf, lo)


def _topk_mask(score, k):
    s = score.T
    cand = lax.broadcasted_iota(jnp.int32, s.shape, 0).astype(F32)
    sel = jnp.zeros(s.shape, F32)
    for _ in range(k):
        m = jnp.max(s, axis=0, keepdims=True)
        first = jnp.min(jnp.where(s == m, cand, float(LANES)), axis=0, keepdims=True)
        hit = cand == first
        sel = jnp.where(hit, 1.0, sel)
        s = jnp.where(hit, -jnp.inf, s)
    return sel.T


def _slot_ids():
    return lax.broadcasted_iota(jnp.int32, (1, KV_COLS), 1) // HEAD_DIM


def _place_slot(x, src_slot, dst_slot):
    shift = (HEAD_DIM * (dst_slot - src_slot)) % KV_COLS
    y = pltpu.roll(x, shift, 1) if shift else x
    return jnp.where(_slot_ids() == dst_slot, y, 0.0)


def _selection_scores(imp, tpos):
    blk = lax.broadcasted_iota(jnp.int32, imp.shape, 1)
    cur = tpos // L_SEL
    forced = (blk == 0) | (blk == cur) | (blk == cur - 1)
    started = blk * L_SEL <= tpos
    return jnp.where(started, imp + jnp.where(forced, FORCE_BONUS, 0.0), NEG_INF)


def _attn_in_kernel(seq_len, x_ref, gain_ref, wm_ref, wg_ref, q_ref, kv_ref, gates_ref, *refs):
    tm, d = x_ref.shape
    u = _rms(x_ref[...], gain_ref[...]).astype(BF16)
    z = _dot(u, wm_ref[...])
    q_ref[...] = z[:, :d]
    kv = z[:, d:d + 4 * KV_COLS]
    kv_ref[...] = kv
    win = z[:, d + 4 * KV_COLS:d + 6 * KV_COLS]
    gates_ref[...] = _sigmoid(_dot(u, wg_ref[...]))
    if seq_len is None:
        (win_ref,) = refs
        win_ref[...] = win
        return
    ksel_ref, vsel_ref, kwin_ref, vwin_ref, kvt_ref, wint_ref = refs
    pos = (pl.program_id(0) % (seq_len // tm)) * tm + lax.broadcasted_iota(jnp.int32, (tm, 1), 0)
    aug = _position_aug(((pos // LANES) * LANES).astype(F32), (pos % LANES).astype(F32))
    lane = lax.broadcasted_iota(jnp.int32, (1, LANES), 1)
    for g in range(N_KV_HEADS):
        sl = slice(g * LANES, (g + 1) * LANES)
        ksel_ref[:, sl] = jnp.where(lane < AUG_LANE, _head_slot(kv[:, 2 * KV_COLS:3 * KV_COLS], g), aug).astype(BF16)
        kwin_ref[:, sl] = jnp.where(lane < AUG_LANE, _head_slot(win[:, :KV_COLS], g), aug).astype(BF16)
    vsel_ref[...] = kv[:, 3 * KV_COLS:4 * KV_COLS].astype(BF16)
    vwin_ref[...] = win[:, KV_COLS:].astype(BF16)
    kvt_ref[0] = kv.T
    wint_ref[0] = win.T


def _attn_in_proj(x2, gain, w_main, w_gate, seq_len=None):
    rows, d = x2.shape
    tm = min(512, rows)
    nmain = w_main.shape[1]
    row = lambda n: pl.BlockSpec((tm, n), lambda i: (i, 0))
    outs = [(d, F32), (4 * KV_COLS, F32), (LANES, F32)]
    if seq_len is None:
        outs += [(2 * KV_COLS, F32)]
    else:
        outs += [(N_KV_HEADS * LANES, BF16), (KV_COLS, BF16), (N_KV_HEADS * LANES, BF16), (KV_COLS, BF16)]
    out_specs = [row(n) for n, _ in outs]
    out_shape = [jax.ShapeDtypeStruct((rows, n), dt) for n, dt in outs]
    if seq_len is not None:
        nt = seq_len // tm
        for n in (4 * KV_COLS, 2 * KV_COLS):
            out_specs.append(pl.BlockSpec((1, n, tm), lambda i: (i // nt, 0, i % nt)))
            out_shape.append(jax.ShapeDtypeStruct((rows // seq_len, n, seq_len), F32))
    return pl.pallas_call(
        functools.partial(_attn_in_kernel, seq_len),
        grid=(rows // tm,),
        in_specs=[row(d), _const_spec((1, d)), _const_spec((d, nmain)), _const_spec((d, LANES))],
        out_specs=out_specs,
        out_shape=out_shape,
        compiler_params=_cparams(("arbitrary",)),
        name="attn_in_proj",
    )(x2, gain, w_main, w_gate)


def _matmul_res_kernel(a_ref, w_ref, x_ref, o_ref):
    o_ref[...] = x_ref[...] + _dot(a_ref[...].astype(BF16), w_ref[...])


def _matmul_res(a, w, x):
    rows, k = a.shape
    n = w.shape[1]
    tm = min(512, rows)
    return pl.pallas_call(
        _matmul_res_kernel,
        grid=(rows // tm,),
        in_specs=[pl.BlockSpec((tm, k), lambda i: (i, 0)), _const_spec((k, n)),
                  pl.BlockSpec((tm, n), lambda i: (i, 0))],
        out_specs=pl.BlockSpec((tm, n), lambda i: (i, 0)),
        out_shape=jax.ShapeDtypeStruct((rows, n), F32),
        compiler_params=_cparams(("arbitrary",)),
        name="matmul_res",
    )(a, w, x)


def _finish_compress(a, b, w2):
    nseg = a.shape[0]
    pre = a + pltpu.roll(b, nseg - 1, 0)
    out = _dot(_gelu(pre).astype(BF16), w2)
    rowi = lax.broadcasted_iota(jnp.int32, (nseg, 1), 0)
    return jnp.where(rowi < nseg - 1, out, 0.0)


def _segment_sums_kernel(x_ref, pea_ref, peb_ref, w1a_ref, w1b_ref, w2_ref, ck_ref, cv_ref, acc_a, acc_b):
    kind = pl.program_id(1)
    l = pl.program_id(2)
    nseg = acc_a.shape[0]

    @pl.when(l == 0)
    def _():
        acc_a[...] = jnp.zeros(acc_a.shape, F32)
        acc_b[...] = jnp.zeros(acc_b.shape, F32)

    xs = x_ref[0]
    acc_a[...] += _dot((xs + pea_ref[0, 0]).astype(BF16), w1a_ref[0, 0])
    acc_b[...] += _dot((xs + peb_ref[0, 0]).astype(BF16), w1b_ref[0, 0])

    @pl.when((l == D_CMP - 1) & (kind == 0))
    def _():
        kc = _finish_compress(acc_a[...], acc_b[...], w2_ref[0])
        n_i = lax.broadcasted_iota(jnp.int32, (nseg, 1), 0)
        hi = ((n_i * D_CMP) // LANES * LANES).astype(F32)
        lo = ((n_i * D_CMP) % LANES).astype(F32) + (L_CMP - 1) / 2
        aug = _position_aug(hi, lo)
        lane = lax.broadcasted_iota(jnp.int32, (1, LANES), 1)
        for g in range(N_KV_HEADS):
            ck_ref[0, :, g * LANES:(g + 1) * LANES] = jnp.where(lane < AUG_LANE, _head_slot(kc, g), aug).astype(BF16)

    @pl.when((l == D_CMP - 1) & (kind == 1))
    def _():
        cv_ref[0] = _finish_compress(acc_a[...], acc_b[...], w2_ref[0]).astype(BF16)


def _segment_sums(rows3, pe_t, w1_bd, w2_bd):
    bsz, nseg, _ = rows3.shape
    pe4 = pe_t.reshape(2, L_CMP, 1, KV_COLS)
    return pl.pallas_call(
        _segment_sums_kernel,
        grid=(bsz, 2, D_CMP),
        in_specs=[pl.BlockSpec((1, nseg, KV_COLS), lambda i, k, l: (i, 0, l * 4 + k)),
                  pl.BlockSpec((1, 1, 1, KV_COLS), lambda i, k, l: (k, l, 0, 0)),
                  pl.BlockSpec((1, 1, 1, KV_COLS), lambda i, k, l: (k, D_CMP + l, 0, 0)),
                  pl.BlockSpec((1, 1, KV_COLS, KV_COLS), lambda i, k, l: (k, l, 0, 0)),
                  pl.BlockSpec((1, 1, KV_COLS, KV_COLS), lambda i, k, l: (k, D_CMP + l, 0, 0)),
                  pl.BlockSpec((1, KV_COLS, KV_COLS), lambda i, k, l: (k, 0, 0))],
        out_specs=[pl.BlockSpec((1, nseg, N_KV_HEADS * LANES), lambda i, k, l: (i, 0, 0)),
                   pl.BlockSpec((1, nseg, KV_COLS), lambda i, k, l: (i, 0, 0))],
        out_shape=[jax.ShapeDtypeStruct((bsz, nseg, N_KV_HEADS * LANES), BF16),
                   jax.ShapeDtypeStruct((bsz, nseg, KV_COLS), BF16)],
        scratch_shapes=[pltpu.VMEM((nseg, KV_COLS), F32), pltpu.VMEM((nseg, KV_COLS), F32)],
        compiler_params=_cparams(("arbitrary", "arbitrary", "arbitrary")),
        name="segment_sums",
    )(rows3, pe4, pe4, w1_bd, w1_bd, w2_bd)


def _pool_sums_kernel(pp, page, x_ref, pe_ref, w1_ref, a_ref, b_ref, rows_s):
    ftiles = 2 * KV_COLS // LANES

    def to_rows(p, carry):
        xt = x_ref[p]
        r0 = pl.multiple_of(p * page, page)
        for c in range(ftiles):
            rows_s[c, pl.ds(r0, page), :] = xt[c * LANES:(c + 1) * LANES, :].T
        return carry

    lax.fori_loop(0, pp, to_rows, 0, unroll=4)
    nseg = pp * page // D_CMP
    tpk = KV_COLS // LANES
    for kind in range(2):
        a = jnp.zeros((nseg, KV_COLS), F32)
        b = jnp.zeros((nseg, KV_COLS), F32)
        for l in range(D_CMP):
            xs = jnp.concatenate([rows_s[kind * tpk + c, pl.ds(l, nseg, stride=D_CMP), :] for c in range(tpk)],
                                 axis=1)
            a = a + _dot((xs + pe_ref[kind, l:l + 1, :]).astype(BF16), w1_ref[kind, l])
            b = b + _dot((xs + pe_ref[kind, D_CMP + l:D_CMP + l + 1, :]).astype(BF16), w1_ref[kind, D_CMP + l])
        a_ref[:, kind * KV_COLS:(kind + 1) * KV_COLS] = a
        b_ref[:, kind * KV_COLS:(kind + 1) * KV_COLS] = b


def _pool_segment_sums(cache_t, pe_t, w1_bd, pp):
    n_pool, _, page = cache_t.shape
    assert page == LANES and n_pool % pp == 0
    nseg = pp * page // D_CMP
    return pl.pallas_call(
        functools.partial(_pool_sums_kernel, pp, page),
        grid=(n_pool // pp,),
        in_specs=[pl.BlockSpec((pp, 2 * KV_COLS, page), lambda i: (i, 0, 0)),
                  _const_spec(pe_t.shape), _const_spec(w1_bd.shape)],
        out_specs=[pl.BlockSpec((nseg, 2 * KV_COLS), lambda i: (i, 0))] * 2,
        out_shape=[jax.ShapeDtypeStruct((n_pool * page // D_CMP, 2 * KV_COLS), F32)] * 2,
        scratch_shapes=[pltpu.VMEM((2 * KV_COLS // LANES, pp * page, LANES), F32)],
        compiler_params=_cparams(("arbitrary",)),
        name="pool_segment_sums",
    )(cache_t, pe_t, w1_bd)


AUG_LANE = HEAD_DIM
N_PIECES = 3
SOFTMAX_ROWS = 32


def _position_aug(hi, lo):
    lane = lax.broadcasted_iota(jnp.int32, (1, LANES), 1)
    in_hi = (lane >= AUG_LANE) & (lane < AUG_LANE + N_PIECES)
    in_lo = (lane >= AUG_LANE + N_PIECES) & (lane < AUG_LANE + 2 * N_PIECES)
    return jnp.where(in_hi, hi, jnp.where(in_lo, lo, 0.0))


def _head_slot(x, g):
    tile = x[:, LANES * (g // 2):LANES * (g // 2 + 1)]
    return pltpu.roll(tile, HEAD_DIM, 1) if g % 2 else tile


def _for_row_blocks(n_rows, body):
    for r0 in range(0, n_rows, SOFTMAX_ROWS):
        body(r0)


def _nsa_prompt_kernel(nc, q_ref, gates_ref, ck_ref, cv_ref, ksel_ref, vsel_ref, kwin_ref, vwin_ref,
                       onehot_ref, pool_ref, qaug_ref, o_ref,
                       qa_s, s_s, s2_s, p_s, p2_s, m_s, l_s, al_s, acc_s, ocmp_s, imp_s, score_s, selb_s):
    tq = Q_TILE
    rows = GROUP * tq
    rb = SOFTMAX_ROWS
    ncp = ck_ref.shape[1]
    qb = pl.program_id(1)
    q0 = qb * tq
    gates = gates_ref[0]
    lane = lax.broadcasted_iota(jnp.int32, (1, LANES), 1)
    tpos1 = q0 + lax.broadcasted_iota(jnp.int32, (tq, 1), 0)
    row_pos = lambda r0: q0 + r0 % tq + lax.broadcasted_iota(jnp.int32, (rb, 1), 0)
    win_len = WINDOW + tq
    w0 = pl.multiple_of(jnp.maximum(q0 - WINDOW, 0), tq)
    n_past_chunks = q0 // SEL_CHUNK
    pair = lambda g: slice(LANES * (g // 2), LANES * (g // 2 + 1))
    slot = lambda g: slice(LANES * g, LANES * (g + 1))

    def softmax_once(r0, width, valid):
        sb = jnp.where(valid, s_s[pl.ds(r0, rb), :width], NEG_INF)
        m = jnp.max(sb, axis=-1, keepdims=True)
        e = jnp.exp(sb - m)
        den = jnp.sum(e, axis=-1, keepdims=True)
        p = e * jnp.where(m > 0.5 * NEG_INF, 1.0 / den, 0.0)
        p_s[pl.ds(r0, rb), :width] = p.astype(BF16)
        return p

    n_i = lax.broadcasted_iota(jnp.int32, (1, ncp), 1)
    c_end = jnp.where(n_i < nc, n_i * D_CMP + (L_CMP - 1), jnp.int32(2 ** 30))

    for g in range(N_KV_HEADS):
        for r in range(GROUP):
            h = g * GROUP + r
            qh = q_ref[0, :, LANES * (h // 2):LANES * (h // 2 + 1)] * (HEAD_DIM ** -0.5)
            if h % 2:
                qh = pltpu.roll(qh, HEAD_DIM, 1)
            qa_s[g, r * tq:(r + 1) * tq, :] = jnp.where(lane < AUG_LANE, qh, qaug_ref[h:h + 1, :]).astype(BF16)

        s_s[:, :ncp] = _dot_nt(qa_s[g], ck_ref[0, :, slot(g)])
        imp_s[...] = jnp.zeros(imp_s.shape, F32)

        def cmp_block(r0):
            p = softmax_once(r0, ncp, c_end <= row_pos(r0))
            imp_s[pl.ds(r0 % tq, rb), :] += p

        _for_row_blocks(rows, cmp_block)
        ocmp_s[g] = _dot(p_s[:, :ncp], cv_ref[0, :, pair(g)])
        score_s[g * tq:(g + 1) * tq, :] = _selection_scores(_exact_pool(imp_s[...], pool_ref[...]), tpos1)

    selb_s[...] = jnp.where(_topk_mask(score_s[...], TOP_N) > 0.5, 0.0, NEG_INF).astype(BF16)

    for g in range(N_KV_HEADS):
        sel_bias = selb_s[g * tq:(g + 1) * tq, :]
        q_full = jnp.concatenate([qa_s[g], jnp.concatenate([sel_bias] * GROUP, axis=0)], axis=1)

        m_s[...] = jnp.full(m_s.shape, NEG_INF, F32)
        l_s[...] = jnp.zeros(l_s.shape, F32)
        acc_s[...] = jnp.zeros(acc_s.shape, F32)

        def sel_scores(c, buf):
            k0 = pl.multiple_of(c * SEL_CHUNK, SEL_CHUNK)
            k_full = jnp.concatenate([ksel_ref[0, pl.ds(k0, SEL_CHUNK), slot(g)],
                                      onehot_ref[pl.ds(k0, SEL_CHUNK), :]], axis=1)
            buf[:, :SEL_CHUNK] = _dot_nt(q_full, k_full)

        def sel_consume(c, buf, causal):
            pbuf = p_s if buf is s_s else p2_s
            k0 = pl.multiple_of(c * SEL_CHUNK, SEL_CHUNK)
            kpos = k0 + lax.broadcasted_iota(jnp.int32, (1, SEL_CHUNK), 1)

            def block(r0):
                rs = pl.ds(r0, rb)
                sb = buf[rs, :SEL_CHUNK]
                if causal:
                    sb = jnp.where(kpos <= row_pos(r0), sb, NEG_INF)
                m_old = m_s[rs, :]
                m_new = jnp.maximum(m_old, jnp.max(sb, axis=-1, keepdims=True))
                alpha = jnp.exp(m_old - m_new)
                p = jnp.exp(sb - jnp.tile(m_new, (1, SEL_CHUNK // LANES)))
                l_s[rs, :] = alpha * l_s[rs, :] + jnp.sum(p, axis=-1, keepdims=True)
                m_s[rs, :] = m_new
                al_s[rs, :] = alpha
                pbuf[rs, :SEL_CHUNK] = p.astype(BF16)

            _for_row_blocks(rows, block)
            acc_s[...] = al_s[...] * acc_s[...] + _dot(pbuf[:, :SEL_CHUNK], vsel_ref[0, pl.ds(k0, SEL_CHUNK), pair(g)])

        sel_scores(0, s_s)

        def past_pair(i, carry):
            sel_scores(2 * i + 1, s2_s)
            sel_consume(2 * i, s_s, False)
            sel_scores(2 * i + 2, s_s)
            sel_consume(2 * i + 1, s2_s, False)
            return carry

        lax.fori_loop(0, n_past_chunks // 2, past_pair, 0)

        @pl.when(n_past_chunks % 2 == 0)
        def _():
            sel_consume(n_past_chunks, s_s, True)

        @pl.when(n_past_chunks % 2 == 1)
        def _():
            sel_scores(n_past_chunks, s2_s)
            sel_consume(n_past_chunks - 1, s_s, False)
            sel_consume(n_past_chunks, s2_s, True)

        o_sel = acc_s[...] / l_s[...]

        s_s[:, :win_len] = _dot_nt(qa_s[g], kwin_ref[0, pl.ds(w0, win_len), slot(g)])
        wpos = w0 + lax.broadcasted_iota(jnp.int32, (1, win_len), 1)

        def win_block(r0):
            wd = row_pos(r0) - wpos
            softmax_once(r0, win_len, (wd >= 0) & (wd < WINDOW))

        _for_row_blocks(rows, win_block)
        o_win = _dot(p_s[:, :win_len], vwin_ref[0, pl.ds(w0, win_len), pair(g)])
        o_cmp = ocmp_s[g]

        halves = []
        for r in range(GROUP):
            c0 = 3 * (g * GROUP + r)
            rs = slice(r * tq, (r + 1) * tq)
            o_r = (gates[:, c0:c0 + 1] * o_cmp[rs] + gates[:, c0 + 1:c0 + 2] * o_sel[rs]
                   + gates[:, c0 + 2:c0 + 3] * o_win[rs])
            halves.append(pltpu.roll(o_r, HEAD_DIM, 1) if (r - g) % 2 else o_r)
        for j in range(GROUP // 2):
            tile = jnp.where(lane < HEAD_DIM, halves[2 * j], halves[2 * j + 1])
            o_ref[0, :, LANES * (2 * g + j):LANES * (2 * g + j + 1)] = tile.astype(o_ref.dtype)


def _nsa_prompt_attend(q, gates, cmp_k, cmp_v, ksel, vsel, kwin, vwin, onehot, pool, qaug):
    bsz, t, d = q.shape
    nc = (t - L_CMP) // D_CMP + 1
    ncp = cmp_k.shape[1]
    rows = GROUP * Q_TILE
    width = max(ncp, SEL_CHUNK, WINDOW + Q_TILE)
    seq = lambda a: pl.BlockSpec((1,) + a.shape[1:], lambda b, i: (b, 0, 0), pipeline_mode=pl.Buffered(1))
    stat = pltpu.VMEM((rows, LANES), F32)
    return pl.pallas_call(
        functools.partial(_nsa_prompt_kernel, nc),
        grid=(bsz, t // Q_TILE),
        in_specs=[pl.BlockSpec((1, Q_TILE, d), lambda b, i: (b, i, 0)),
                  pl.BlockSpec((1, Q_TILE, LANES), lambda b, i: (b, i, 0)),
                  seq(cmp_k), seq(cmp_v), seq(ksel), seq(vsel), seq(kwin), seq(vwin),
                  _const_spec(onehot.shape), _const_spec(pool.shape), _const_spec(qaug.shape)],
        out_specs=pl.BlockSpec((1, Q_TILE, d), lambda b, i: (b, i, 0)),
        out_shape=jax.ShapeDtypeStruct((bsz, t, d), BF16),
        scratch_shapes=[pltpu.VMEM((N_KV_HEADS, rows, LANES), BF16),
                        pltpu.VMEM((rows, width), F32),
                        pltpu.VMEM((rows, SEL_CHUNK), F32),
                        pltpu.VMEM((rows, width), BF16),
                        pltpu.VMEM((rows, SEL_CHUNK), BF16),
                        stat, stat, stat, stat,
                        pltpu.VMEM((N_KV_HEADS, rows, LANES), F32),
                        pltpu.VMEM((Q_TILE, ncp), F32),
                        pltpu.VMEM((N_KV_HEADS * Q_TILE, LANES), F32),
                        pltpu.VMEM((N_KV_HEADS * Q_TILE, LANES), BF16)],
        compiler_params=_cparams(("arbitrary", "arbitrary")),
        name="nsa_prompt_attend",
    )(q, gates, cmp_k, cmp_v, ksel, vsel, kwin, vwin, onehot, pool, qaug)


def _sample_qpad(qf, tq):
    parts = []
    for g in range(N_KV_HEADS):
        piece = qf[:, g * KV_COLS:(g + 1) * KV_COLS]
        for r in range(GROUP):
            parts.append(_place_slot(piece, r, g))
    return jnp.concatenate(parts, axis=0).astype(BF16)


def _sample_rows_to_tokens(o, tq):
    chunks = []
    for g in range(N_KV_HEADS):
        chunk = jnp.zeros((tq, KV_COLS), F32)
        for r in range(GROUP):
            i0 = (g * GROUP + r) * tq
            chunk = chunk + _place_slot(o[i0:i0 + tq], g, r)
        chunks.append(chunk)
    return jnp.concatenate(chunks, axis=1)


def _nsa_sample_cmp_kernel(n_pages, past_len, tq, pt_ref, q_ref, w2_ref, pool_ref, slope_ref, *refs):
    a_refs = refs[:n_pages]
    b_refs = refs[n_pages:2 * n_pages]
    ocmp_ref, sel_ref = refs[2 * n_pages:]
    del pt_ref
    rows = N_HEADS * tq
    a = jnp.concatenate([r[0] for r in a_refs], axis=0)
    b = jnp.concatenate([r[0] for r in b_refs], axis=0)
    nseg = a.shape[0]
    kc = _finish_compress(a[:, :KV_COLS], b[:, :KV_COLS], w2_ref[0]).astype(BF16)
    vc = _finish_compress(a[:, KV_COLS:], b[:, KV_COLS:], w2_ref[1]).astype(BF16)

    qpad = _sample_qpad(q_ref[0] * (HEAD_DIM ** -0.5), tq)
    slope = slope_ref[...]
    rowi = lax.broadcasted_iota(jnp.int32, (rows, 1), 0)
    tpos = past_len + rowi % tq
    tposf = tpos.astype(F32)
    nc = (past_len + tq - L_CMP) // D_CMP + 1
    n_i = lax.broadcasted_iota(jnp.int32, (1, nseg), 1)
    c_end = jnp.where(n_i < nc, n_i * D_CMP + (L_CMP - 1), jnp.int32(2 ** 30))
    center = (n_i * D_CMP).astype(F32) + (L_CMP - 1) / 2
    s = _dot_nt(qpad, kc) - slope * (tposf - center)
    p_cmp = _masked_softmax(s, c_end <= tpos)
    ocmp_ref[0] = _dot(p_cmp.astype(BF16), vc)

    imps = []
    for g in range(N_KV_HEADS):
        i0 = g * GROUP * tq
        imp = p_cmp[i0:i0 + tq]
        for r in range(1, GROUP):
            imp = imp + p_cmp[i0 + r * tq:i0 + (r + 1) * tq]
        imps.append(imp)
    tpos_g = past_len + lax.broadcasted_iota(jnp.int32, (N_KV_HEADS * tq, 1), 0) % tq
    score = _selection_scores(_exact_pool(jnp.concatenate(imps, axis=0), pool_ref[...]), tpos_g)
    n_sel_rows = N_KV_HEADS * tq
    pad = jnp.full((LANES - n_sel_rows, LANES), NEG_INF, F32)
    sel = _topk_mask(jnp.concatenate([score, pad], axis=0), TOP_N)[:n_sel_rows]
    sel_ref[0] = jnp.concatenate([sel[g * tq:(g + 1) * tq] for g in range(N_KV_HEADS) for _ in range(GROUP)],
                                 axis=0).astype(BF16)


def _nsa_sample_cmp(page_table, q, a_all, b_all, w2_bd, pool, slope_rows, page, past_len):
    bs, tq, d = q.shape
    n_pages = page_table.shape[1]
    seg_pp = page // D_CMP
    rows = N_HEADS * tq
    a3 = a_all.reshape(-1, seg_pp, 2 * KV_COLS)
    b3 = b_all.reshape(-1, seg_pp, 2 * KV_COLS)

    def page_spec(p):
        return pl.BlockSpec((1, seg_pp, 2 * KV_COLS), lambda s, pt: (pt[s, p], 0, 0))

    cst = lambda shape: pl.BlockSpec(shape, lambda s, pt: (0,) * len(shape))
    grid_spec = pltpu.PrefetchScalarGridSpec(
        num_scalar_prefetch=1,
        grid=(bs,),
        in_specs=[pl.BlockSpec((1, tq, d), lambda s, pt: (s, 0, 0)),
                  cst(w2_bd.shape), cst(pool.shape), cst(slope_rows.shape)]
                 + [page_spec(p) for p in range(n_pages)] * 2,
        out_specs=[pl.BlockSpec((1, rows, KV_COLS), lambda s, pt: (s, 0, 0)),
                   pl.BlockSpec((1, rows, LANES), lambda s, pt: (s, 0, 0))],
    )
    return pl.pallas_call(
        functools.partial(_nsa_sample_cmp_kernel, n_pages, past_len, tq),
        grid_spec=grid_spec,
        out_shape=[jax.ShapeDtypeStruct((bs, rows, KV_COLS), F32),
                   jax.ShapeDtypeStruct((bs, rows, LANES), BF16)],
        compiler_params=_cparams(("arbitrary",)),
        name="nsa_sample_cmp",
    )(page_table, q, w2_bd, pool, slope_rows, *([a3] * n_pages), *([b3] * n_pages))


def _nsa_sample_attend_kernel(n_pages, page, tq, pt_ref, q_ref, gates_ref, kvn_ref, winn_ref, winp_ref,
                              ocmp_ref, sel_ref, expand_ref, slope_ref, *refs):
    page_refs = refs[:n_pages]
    o_ref, newwin_ref = refs[n_pages:]
    del pt_ref
    rows = N_HEADS * tq
    past_len = n_pages * page
    wbuf = winp_ref.shape[2]
    qpad = _sample_qpad(q_ref[0] * (HEAD_DIM ** -0.5), tq)
    slope = slope_ref[...]
    rowi = lax.broadcasted_iota(jnp.int32, (rows, 1), 0)
    tpos = past_len + rowi % tq
    tposf = tpos.astype(F32)
    newpos = past_len + lax.broadcasted_iota(jnp.int32, (1, LANES), 1)
    pad_rows = lambda a: jnp.concatenate([a, jnp.zeros((LANES - tq, a.shape[1]), F32)], axis=0)

    kvn = pad_rows(kvn_ref[0])
    k_new = kvn[:, 2 * KV_COLS:3 * KV_COLS].astype(BF16)
    v_new = kvn[:, 3 * KV_COLS:4 * KV_COLS].astype(BF16)
    scores = [_dot(qpad, r[0, :KV_COLS, :].astype(BF16)) for r in page_refs]
    scores.append(_dot_nt(qpad, k_new))
    s = jnp.concatenate(scores, axis=1)
    nk = past_len + LANES
    kpos = lax.broadcasted_iota(jnp.int32, (1, nk), 1)
    picked = _dot(sel_ref[0], expand_ref[...]) > 0.5
    s = s - slope * (tposf - kpos.astype(F32))
    p = _masked_softmax(s, picked & (kpos <= tpos)).astype(BF16)
    o_sel = _dot(p[:, past_len:], v_new)
    for i, r in enumerate(page_refs):
        o_sel = o_sel + _dot_nt(p[:, i * page:(i + 1) * page], r[0, KV_COLS:, :].astype(BF16))

    winp = winp_ref[0]
    winn = pad_rows(winn_ref[0])
    winn_t = pltpu.roll(winn.T, LANES - tq, 1)
    lane_w = lax.broadcasted_iota(jnp.int32, (1, wbuf), 1)
    newwin_ref[0] = jnp.where(lane_w >= wbuf - tq, jnp.concatenate([winn_t] * (wbuf // LANES), axis=1),
                              pltpu.roll(winp, wbuf - tq, 1))
    sp = _dot(qpad, winp[:KV_COLS, :].astype(BF16))
    sn = _dot_nt(qpad, winn[:, :KV_COLS].astype(BF16))
    wpos_p = past_len - wbuf + lax.broadcasted_iota(jnp.int32, (1, wbuf), 1)
    wpos = jnp.concatenate([wpos_p, newpos], axis=1)
    wd = tpos - wpos
    s = jnp.concatenate([sp, sn], axis=1) - slope * wd.astype(F32)
    p = _masked_softmax(s, (wd >= 0) & (wd < WINDOW) & (wpos >= 0)).astype(BF16)
    o_win = _dot_nt(p[:, :wbuf], winp[KV_COLS:, :].astype(BF16)) + _dot(p[:, wbuf:], winn[:, KV_COLS:].astype(BF16))

    gates = gates_ref[0]
    o_cmp = ocmp_ref[0]
    merged = []
    for h in range(N_HEADS):
        rs = slice(h * tq, (h + 1) * tq)
        merged.append(gates[:, 3 * h:3 * h + 1] * o_cmp[rs] + gates[:, 3 * h + 1:3 * h + 2] * o_sel[rs]
                      + gates[:, 3 * h + 2:3 * h + 3] * o_win[rs])
    o_ref[0] = _sample_rows_to_tokens(jnp.concatenate(merged, axis=0), tq).astype(o_ref.dtype)


def _nsa_sample_attend(page_table, cache_t, q, gates, kv_new, win_new, win_past_t, o_cmp, sel, expand, slope_rows):
    bs, tq, d = q.shape
    n_pages = page_table.shape[1]
    page = cache_t.shape[2]
    wbuf = win_past_t.shape[2]
    rows = N_HEADS * tq

    def page_spec(p):
        return pl.BlockSpec((1, 2 * KV_COLS, page), lambda s, pt: (pt[s, p], 1, 0))

    per = lambda r, n: pl.BlockSpec((1, r, n), lambda s, pt: (s, 0, 0))
    cst = lambda shape: pl.BlockSpec(shape, lambda s, pt: (0,) * len(shape))
    grid_spec = pltpu.PrefetchScalarGridSpec(
        num_scalar_prefetch=1,
        grid=(bs,),
        in_specs=[per(tq, d), per(tq, LANES), per(tq, 4 * KV_COLS), per(tq, 2 * KV_COLS),
                  per(2 * KV_COLS, wbuf), per(rows, KV_COLS), per(rows, LANES),
                  cst(expand.shape), cst(slope_rows.shape)]
                 + [page_spec(p) for p in range(n_pages)],
        out_specs=[per(tq, d), per(2 * KV_COLS, wbuf)],
    )
    return pl.pallas_call(
        functools.partial(_nsa_sample_attend_kernel, n_pages, page, tq),
        grid_spec=grid_spec,
        out_shape=[jax.ShapeDtypeStruct((bs, tq, d), BF16),
                   jax.ShapeDtypeStruct((bs, 2 * KV_COLS, wbuf), F32)],
        compiler_params=_cparams(("arbitrary",)),
        name="nsa_sample_attend",
    )(page_table, q, gates, kv_new, win_new, win_past_t, o_cmp, sel, expand, slope_rows,
      *([cache_t] * n_pages))


def _cmul_add(ar, ai, hr, hi, xr, xi):
    return ar * hr - ai * hi + xr, ar * hi + ai * hr + xi


def _s5_kernel(mode, tm, x_ref, h0r_ref, h0i_ref, gain_ref, bre_ref, bim_ref, are_ref, aim_ref,
               pwr_ref, pwi_ref, cre_ref, cim_ref, d_ref, wglu_ref, perm_ref, permt_ref,
               xo_ref, hfr_ref, hfi_ref, hr_s, hi_s, cr_s, ci_s):
    d = x_ref.shape[1]
    nstate = cr_s.shape[1]
    nblk = bre_ref.shape[0]
    cb = d // nblk
    sb = nstate // nblk
    tpb = sb // LANES
    x = x_ref[...]
    uf = _rms(x, gain_ref[...])
    ub = uf.astype(BF16)

    if mode == "carry":
        nj = tm // SUBLANES
        ub = _dot(perm_ref[...], ub).astype(BF16)
    for j in range(nblk):
        uj = ub[:, j * cb:(j + 1) * cb]
        br = _dot(uj, bre_ref[j])
        bi = _dot(uj, bim_ref[j])
        for k in range(tpb):
            hr_s[j * tpb + k] = br[:, k * LANES:(k + 1) * LANES]
            hi_s[j * tpb + k] = bi[:, k * LANES:(k + 1) * LANES]

    lc = SSM_LANE_CHUNK
    par = lc // LANES
    if mode == "carry":
        @pl.when(pl.program_id(1) == 0)
        def _():
            cr_s[...] = jnp.broadcast_to(h0r_ref[0], cr_s.shape)
            ci_s[...] = jnp.broadcast_to(h0i_ref[0], ci_s.shape)

        sub = lax.broadcasted_iota(jnp.int32, (SUBLANES, 1), 0)
        for c in range(nstate // lc):
            sl = slice(c * lc, (c + 1) * lc)
            ar = jnp.broadcast_to(are_ref[:, sl], (SUBLANES, lc))
            ai = jnp.broadcast_to(aim_ref[:, sl], (SUBLANES, lc))

            tiles = range(c * par, (c + 1) * par)
            lane = lambda v, k: v[:, k * LANES:(k + 1) * LANES]

            def p1(jj, carry):
                st, mult = carry
                idx = pl.ds(pl.multiple_of(jj * SUBLANES, SUBLANES), SUBLANES)
                out = []
                for k, q in enumerate(tiles):
                    nr, ni = _cmul_add(mult[2 * k], mult[2 * k + 1], st[2 * k], st[2 * k + 1],
                                       hr_s[q, idx, :], hi_s[q, idx, :])
                    hr_s[q, idx, :] = nr
                    hi_s[q, idx, :] = ni
                    out += [nr, ni]
                return tuple(out), mult

            z = jnp.zeros((SUBLANES, LANES), F32)
            mult0 = tuple(lane(v, k) for k in range(par) for v in (ar, ai))
            lf, _ = lax.fori_loop(0, nj, p1, ((z,) * (2 * par), mult0))
            lfr = jnp.concatenate(lf[0::2], axis=1)
            lfi = jnp.concatenate(lf[1::2], axis=1)

            alr = pwr_ref[nj - 1:nj, sl]
            ali = pwi_ref[nj - 1:nj, sl]
            inr = cr_s[0:1, sl]
            ini = ci_s[0:1, sl]
            hin_r = jnp.zeros((SUBLANES, lc), F32)
            hin_i = jnp.zeros((SUBLANES, lc), F32)
            for s in range(SUBLANES):
                hin_r = jnp.where(sub == s, inr, hin_r)
                hin_i = jnp.where(sub == s, ini, hin_i)
                inr, ini = _cmul_add(alr, ali, inr, ini, lfr[s:s + 1], lfi[s:s + 1])
            cr_s[:, sl] = jnp.broadcast_to(inr, (SUBLANES, lc))
            ci_s[:, sl] = jnp.broadcast_to(ini, (SUBLANES, lc))

            def p2(jj, carry):
                idx = pl.ds(pl.multiple_of(jj * SUBLANES, SUBLANES), SUBLANES)
                pr = pwr_ref[pl.ds(jj, 1), sl]
                pi_ = pwi_ref[pl.ds(jj, 1), sl]
                for k, q in enumerate(tiles):
                    nr, ni = _cmul_add(lane(pr, k), lane(pi_, k), lane(hin_r, k), lane(hin_i, k),
                                       hr_s[q, idx, :], hi_s[q, idx, :])
                    hr_s[q, idx, :] = nr
                    hi_s[q, idx, :] = ni
                return carry

            lax.fori_loop(0, nj, p2, 0, unroll=4)

        @pl.when(pl.program_id(1) == pl.num_programs(1) - 1)
        def _():
            hfr_ref[0] = cr_s[0:1, :]
            hfi_ref[0] = ci_s[0:1, :]
    else:
        nseq = tm // mode
        for q in range(nstate // LANES):
            sl = slice(q * LANES, (q + 1) * LANES)
            ar = jnp.broadcast_to(are_ref[:, sl], (nseq, LANES))
            ai = jnp.broadcast_to(aim_ref[:, sl], (nseq, LANES))
            sr = h0r_ref[:, sl]
            si = h0i_ref[:, sl]
            for t in range(mode):
                idx = pl.ds(t, nseq, stride=mode)
                sr, si = _cmul_add(ar, ai, sr, si, hr_s[q, idx, :], hi_s[q, idx, :])
                hr_s[q, idx, :] = sr
                hi_s[q, idx, :] = si
            hfr_ref[:, sl] = sr
            hfi_ref[:, sl] = si

    def block_rows(ref, j):
        return jnp.concatenate([ref[j * tpb + k] for k in range(tpb)], axis=1).astype(BF16)

    ys = []
    for j in range(nblk):
        ys.append(_dot(block_rows(hr_s, j), cre_ref[j]) - _dot(block_rows(hi_s, j), cim_ref[j]))
    y = jnp.concatenate(ys, axis=1)
    if mode == "carry":
        y = _exact_rows(permt_ref[...], y)
    y = y + d_ref[...] * uf
    ab = _dot(_gelu(y).astype(BF16), wglu_ref[...])
    xo_ref[...] = x + ab[:, :d] * _sigmoid(ab[:, d:])


def _s5_layer(x2, h0r, h0i, seq_len, n_seq, gain, prm):
    rows, d = x2.shape
    nstate = h0r.shape[1]
    if seq_len >= 256:
        mode, tm = "carry", 256
        nt = seq_len // tm
        grid = (n_seq, nt)
        xmap = lambda b, i: (b * nt + i, 0)
        h0_spec = pl.BlockSpec((1, 1, nstate), lambda b, i: (b, 0, 0))
        hf_spec = pl.BlockSpec((1, 1, nstate), lambda b, i: (b, 0, 0))
        h0r, h0i = h0r[:, None, :], h0i[:, None, :]
        hf_shape = (n_seq, 1, nstate)
        sem = ("arbitrary", "arbitrary")
        cst = lambda shape: pl.BlockSpec(shape, lambda b, i: (0,) * len(shape), pipeline_mode=pl.Buffered(1))
    else:
        mode = seq_len
        tm = min(256, rows)
        nseq_t = tm // seq_len
        grid = (rows // tm,)
        xmap = lambda i: (i, 0)
        h0_spec = pl.BlockSpec((nseq_t, nstate), lambda i: (i, 0))
        hf_spec = pl.BlockSpec((nseq_t, nstate), lambda i: (i, 0))
        hf_shape = (n_seq, nstate)
        sem = ("arbitrary",)
        cst = lambda shape: pl.BlockSpec(shape, lambda i: (0,) * len(shape), pipeline_mode=pl.Buffered(1))
    pw_r, pw_i = prm["pw_r"], prm["pw_i"]
    src = np.arange(tm).reshape(tm // SUBLANES, SUBLANES)
    perm = np.zeros((tm, tm), np.float32)
    perm[np.arange(tm), ((src % SUBLANES) * (tm // SUBLANES) + src // SUBLANES).reshape(-1)] = 1.0
    consts = [gain, prm["b_r"], prm["b_i"], prm["a_r"], prm["a_i"], pw_r, pw_i, prm["c_r"], prm["c_i"],
              prm["d"], prm["w_glu"], jnp.asarray(perm, BF16), jnp.asarray(perm.T, BF16)]
    outs = pl.pallas_call(
        functools.partial(_s5_kernel, mode, tm),
        grid=grid,
        in_specs=[pl.BlockSpec((tm, d), xmap), h0_spec, h0_spec] + [cst(c.shape) for c in consts],
        out_specs=[pl.BlockSpec((tm, d), xmap), hf_spec, hf_spec],
        out_shape=[jax.ShapeDtypeStruct((rows, d), F32), jax.ShapeDtypeStruct(hf_shape, F32),
                   jax.ShapeDtypeStruct(hf_shape, F32)],
        scratch_shapes=[pltpu.VMEM((nstate // LANES, tm, LANES), F32),
                        pltpu.VMEM((nstate // LANES, tm, LANES), F32),
                        pltpu.VMEM((SUBLANES, nstate), F32), pltpu.VMEM((SUBLANES, nstate), F32)],
        compiler_params=_cparams(sem),
        name="s5_layer",
    )(x2, h0r, h0i, *consts)
    xo, hfr, hfi = outs
    return xo, hfr.reshape(n_seq, nstate), hfi.reshape(n_seq, nstate)


def _s5_params(a_re, a_im, log_dt, b_re, b_im, c_re, c_im, d_skip, w_glu, n_pow):
    g, p = a_re.shape
    ch = b_re.shape[2]
    lr, li = a_re.astype(F32), a_im.astype(F32)
    dt = jnp.exp(log_dt.astype(F32))[:, None]
    mag = jnp.exp(lr * dt)
    ar, ai = mag * jnp.cos(li * dt), mag * jnp.sin(li * dt)
    den = lr * lr + li * li
    kr = ((ar - 1.0) * lr + ai * li) / den
    ki = (ai * lr - (ar - 1.0) * li) / den
    br, bi = b_re.astype(F32), b_im.astype(F32)
    bbar_r = kr[:, :, None] * br - ki[:, :, None] * bi
    bbar_i = kr[:, :, None] * bi + ki[:, :, None] * br
    nblk = g // SSM_GROUP_BLOCK
    eye = jnp.eye(SSM_GROUP_BLOCK, dtype=F32)

    def b_blocks(m):
        m = m.reshape(nblk, SSM_GROUP_BLOCK, p, ch)
        return jnp.einsum("jgpc,gh->jgchp", m, eye).reshape(nblk, SSM_GROUP_BLOCK * ch, SSM_GROUP_BLOCK * p).astype(BF16)

    def c_blocks(m):
        m = m.reshape(nblk, SSM_GROUP_BLOCK, ch, p)
        return jnp.einsum("jgcp,gh->jgphc", m, eye).reshape(nblk, SSM_GROUP_BLOCK * p, SSM_GROUP_BLOCK * ch).astype(BF16)

    ar, ai = ar.reshape(1, g * p), ai.reshape(1, g * p)
    pw_r, pw_i = [ar], [ai]
    for _ in range(n_pow - 1):
        pw_r, pw_i = pw_r + [pw_r[-1] * ar - pw_i[-1] * ai], pw_i + [pw_r[-1] * ai + pw_i[-1] * ar]
    return dict(b_r=b_blocks(bbar_r), b_i=b_blocks(bbar_i), a_r=ar, a_i=ai,
                pw_r=jnp.concatenate(pw_r, axis=0), pw_i=jnp.concatenate(pw_i, axis=0),
                c_r=c_blocks(c_re.astype(F32)), c_i=c_blocks(c_im.astype(F32)),
                d=d_skip.astype(F32).reshape(1, -1), w_glu=w_glu.astype(BF16))


def _ffn_ple_kernel(mode, final_norm, x_ref, p_ref, b1_ref, b2_ref, gffn_ref, wup_ref, cw_ref, cb_ref,
                    wdown_ref, gple_ref, wproj_ref, wgate_ref, gfin_ref, xo_ref, tail_ref, carry_ref):
    tm = x_ref.shape[0]
    f = cw_ref.shape[1]
    x = x_ref[...]
    u = _rms(x, gffn_ref[...]).astype(BF16)
    rowi = lax.broadcasted_iota(jnp.int32, (tm, 1), 0)
    if mode == "carry":
        @pl.when(pl.program_id(1) == 0)
        def _():
            carry_ref[SUBLANES - 2:SUBLANES, :] = b1_ref[0]

    fc = f // FFN_CHUNKS
    x1 = x
    for c in range(FFN_CHUNKS):
        cs = slice(c * fc, (c + 1) * fc)
        h = _dot(u, wup_ref[:, cs])
        gate_branch = _dot(u, wup_ref[:, f + c * fc:f + (c + 1) * fc])
        r1 = pltpu.roll(h, 1, 0)
        r2 = pltpu.roll(h, 2, 0)
        if mode == "carry":
            c0 = carry_ref[SUBLANES - 2:SUBLANES - 1, cs]
            c1 = carry_ref[SUBLANES - 1:SUBLANES, cs]
            hm1 = jnp.where(rowi == 0, c1, r1)
            hm2 = jnp.where(rowi == 0, c0, jnp.where(rowi == 1, c1, r2))
            carry_ref[:, cs] = h[tm - SUBLANES:tm, :]
            tail_ref[:, cs] = h[tm - SUBLANES:tm, :]
        else:
            t = rowi % mode
            hm1 = jnp.where(t == 0, b1_ref[:, cs], r1)
            hm2 = jnp.where(t < 2, b2_ref[:, cs], r2)
            tail_ref[:, cs] = h
        conv = cb_ref[:, cs] + cw_ref[0:1, cs] * hm2 + cw_ref[1:2, cs] * hm1 + cw_ref[2:3, cs] * h
        act = (_gelu(conv) * gate_branch).astype(BF16)
        x1 = x1 + _dot(act, wdown_ref[cs, :])
    gate = _sigmoid(_dot(_rms(x1, gple_ref[...]).astype(BF16), wgate_ref[...]))
    x2 = x1 + _dot(p_ref[...].astype(BF16), wproj_ref[...]) * gate
    xo_ref[...] = _rms(x2, gfin_ref[...]) if final_norm else x2


def _ffn_ple(x2, p2, buf, seq_len, n_seq, final_norm, gffn, wup, cw, cb, wdown, gple, wproj, wgate, gfin):
    rows, d = x2.shape
    f = cw.shape[1]
    ple = p2.shape[1]
    consts = [gffn, wup, cw, cb, wdown, gple, wproj, wgate, gfin]
    if seq_len >= 256:
        mode, tm = "carry", 256
        nt = seq_len // tm
        grid = (n_seq, nt)
        rmap = lambda b, i: (b * nt + i, 0)
        b1, b2 = buf, buf
        bspec = pl.BlockSpec((1, CONV_W - 1, f), lambda b, i: (b, 0, 0))
        tail_rows = SUBLANES
        sem = ("arbitrary", "arbitrary")
        cst = lambda shape: pl.BlockSpec(shape, lambda b, i: (0,) * len(shape), pipeline_mode=pl.Buffered(1))
    else:
        mode = seq_len
        tm = min(256, rows)
        grid = (rows // tm,)
        rmap = lambda i: (i, 0)
        b1 = jnp.pad(buf[:, 1:], ((0, 0), (0, seq_len - 1), (0, 0))).reshape(rows, f)
        b2 = jnp.pad(buf, ((0, 0), (0, seq_len - (CONV_W - 1)), (0, 0))).reshape(rows, f)
        bspec = pl.BlockSpec((tm, f), rmap)
        tail_rows = tm
        sem = ("arbitrary",)
        cst = lambda shape: pl.BlockSpec(shape, lambda i: (0,) * len(shape), pipeline_mode=pl.Buffered(1))
    n_tiles = rows // tm
    xo, tail = pl.pallas_call(
        functools.partial(_ffn_ple_kernel, mode, final_norm),
        grid=grid,
        in_specs=[pl.BlockSpec((tm, d), rmap), pl.BlockSpec((tm, ple), rmap), bspec, bspec]
                 + [cst(c.shape) for c in consts],
        out_specs=[pl.BlockSpec((tm, d), rmap), pl.BlockSpec((tail_rows, f), rmap)],
        out_shape=[jax.ShapeDtypeStruct((rows, d), F32), jax.ShapeDtypeStruct((n_tiles * tail_rows, f), F32)],
        scratch_shapes=[pltpu.VMEM((SUBLANES, f), F32)],
        compiler_params=_cparams(sem),
        name="ffn_ple",
    )(x2, p2, b1, b2, *consts)
    if mode == "carry":
        new_buf = tail.reshape(n_seq, n_tiles // n_seq, SUBLANES, f)[:, -1, SUBLANES - (CONV_W - 1):, :]
    else:
        new_buf = tail.reshape(n_seq, seq_len, f)[:, seq_len - (CONV_W - 1):, :]
    return xo, new_buf


def _block_diag_heads(w):
    eye = jnp.eye(N_KV_HEADS, dtype=w.dtype)
    out = jnp.einsum("...de,gh->...gdhe", w, eye)
    return out.reshape(*w.shape[:-2], KV_COLS, KV_COLS)


def _slopes():
    return np.exp2(-8.0 * np.arange(1, N_HEADS + 1, dtype=np.float64) / N_HEADS).astype(np.float32)


def _query_aug(slopes):
    rest = jnp.asarray(slopes, F32)
    aug = jnp.zeros((N_HEADS, LANES), F32)
    for i in range(N_PIECES):
        piece = rest.astype(BF16).astype(F32)
        rest = rest - piece
        aug = aug.at[:, AUG_LANE + i].set(piece).at[:, AUG_LANE + N_PIECES + i].set(piece)
    return aug


def _expand_matrix(n_keys):
    return (np.arange(LANES)[:, None] == (np.arange(n_keys)[None, :] // L_SEL)).astype(np.float32)


def _pool_matrix(n_cmp):
    return (np.arange(n_cmp)[:, None] // SEL_PER_CMP == np.arange(LANES)[None, :]).astype(np.float32)


def _nsa_layer(xp, xs, cache_l, cache_win_l, page_table, gain, w_in, w_out, pe, w1, w2):
    bsz, t, d = xp.shape
    bs, tq, _ = xs.shape
    page = cache_l.shape[1]
    n_pages = page_table.shape[1]
    past_len = n_pages * page
    n_pool = cache_l.shape[0]
    assert t % SEL_CHUNK == 0 and t // L_SEL <= LANES and t >= WINDOW + Q_TILE
    assert (past_len + LANES) // L_SEL <= LANES and page % D_CMP == 0 and tq <= SUBLANES

    qcols = N_HEADS * HEAD_DIM
    nmain = qcols + 6 * KV_COLS
    w_main = w_in[:, :nmain].astype(BF16)
    w_gate = jnp.pad(w_in[:, nmain:], ((0, 0), (0, LANES - (w_in.shape[1] - nmain)))).astype(BF16)
    w_out_b = w_out.astype(BF16)
    pe_t = jnp.tile(pe, (1, 1, N_KV_HEADS))
    w1_bd = _block_diag_heads(w1).astype(BF16)
    w2_bd = _block_diag_heads(w2).astype(BF16)
    slopes = _slopes()

    q, kv, gates, ksel, vsel, kwin, vwin, kv_t, win_t = _attn_in_proj(
        xp.reshape(bsz * t, d), gain, w_main, w_gate, seq_len=t)
    sh = lambda a: a.reshape(bsz, t, a.shape[-1])
    cmp_k, cmp_v = _segment_sums(kv.reshape(bsz, t // D_CMP, D_CMP * 4 * KV_COLS), pe_t, w1_bd, w2_bd)
    onehot = jnp.asarray(_expand_matrix(t).T, BF16)
    pool = jnp.asarray(_pool_matrix(t // D_CMP), BF16)
    o = _nsa_prompt_attend(sh(q), sh(gates), cmp_k, cmp_v, sh(ksel), sh(vsel), sh(kwin), sh(vwin), onehot, pool,
                           jnp.asarray(_query_aug(slopes)))
    xp_new = _matmul_res(o.reshape(bsz * t, d), w_out_b, xp.reshape(bsz * t, d)).reshape(bsz, t, d)
    keep = min(WINDOW, t)
    from_t = lambda a, n: jnp.transpose(a.reshape(a.shape[0], n, N_KV_HEADS, HEAD_DIM, a.shape[-1]), (0, 4, 1, 2, 3))
    kv_p = from_t(kv_t, 4)
    win_p = from_t(win_t[:, :, t - keep:], 2)

    cache_t = jnp.transpose(cache_l, (0, 2, 3, 4, 1)).reshape(n_pool, 4 * KV_COLS, page)
    wbuf = cache_win_l.shape[1]
    win_past_t = jnp.transpose(cache_win_l, (0, 2, 3, 4, 1)).reshape(bs, 2 * KV_COLS, wbuf)
    qs, kvs, gates_s, wins = _attn_in_proj(xs.reshape(bs * tq, d), gain, w_main, w_gate)
    shs = lambda a: a.reshape(bs, tq, a.shape[-1])
    a_all, b_all = _pool_segment_sums(cache_t, pe_t, w1_bd, math.gcd(n_pool, 32))
    nseg_s = past_len // D_CMP
    pool_s = jnp.asarray(_pool_matrix(nseg_s), BF16)
    slope_s = jnp.asarray(np.repeat(slopes, tq)[:, None])
    o_cmp, sel = _nsa_sample_cmp(page_table, shs(qs), a_all, b_all, w2_bd, pool_s, slope_s, page, past_len)
    expand_s = jnp.asarray(_expand_matrix(past_len + LANES), BF16)
    o_s, new_win_t = _nsa_sample_attend(page_table, cache_t, shs(qs), shs(gates_s), shs(kvs), shs(wins), win_past_t,
                                        o_cmp, sel, expand_s, slope_s)
    xs_new = _matmul_res(o_s.reshape(bs * tq, d), w_out_b, xs.reshape(bs * tq, d)).reshape(bs, tq, d)
    kv_s = shs(kvs).reshape(bs, tq, 4, N_KV_HEADS, HEAD_DIM)
    win_s = from_t(new_win_t, 2)
    return xp_new, xs_new, kv_p, kv_s, win_p, win_s


def kernel(x_prompt, x_sample, cache_kv, cache_win, state_ssm_re, state_ssm_im, state_conv, page_table,
           p_prompt, p_sample, norm_mix, norm_ffn, norm_ple, norm_final, w_attn_in, w_attn_out,
           cmp_pe, cmp_w1, cmp_w2, ssm_a_re, ssm_a_im, ssm_log_dt, ssm_b_re, ssm_b_im, ssm_c_re, ssm_c_im,
           ssm_d, w_glu, w_ffn_up, ffn_conv_w, ffn_conv_b, w_ffn_down, w_ple_proj, w_ple_gate):
    bsz, t, d = x_prompt.shape
    bs, tq, _ = x_sample.shape
    depth = norm_mix.shape[0]
    f = ffn_conv_w.shape[2]
    g, p = ssm_a_re.shape[1:]
    xp, xs = x_prompt, x_sample
    row = lambda v: v.reshape(1, -1).astype(F32)
    kv_p, kv_s, win_p, win_s = [], [], [], []
    sre_p, sim_p, sre_s, sim_s = [], [], [], []
    cb_p, cb_s = [], []
    for i in range(depth):
        j = i // 2
        if i % 2 == 0:
            xp, xs, kvp, kvs, wp, ws = _nsa_layer(xp, xs, cache_kv[j], cache_win[j], page_table, row(norm_mix[i]),
                                                  w_attn_in[j], w_attn_out[j], cmp_pe[j], cmp_w1[j], cmp_w2[j])
            kv_p.append(kvp)
            kv_s.append(kvs)
            win_p.append(wp)
            win_s.append(ws)
        else:
            prm = _s5_params(ssm_a_re[j], ssm_a_im[j], ssm_log_dt[j], ssm_b_re[j], ssm_b_im[j], ssm_c_re[j],
                             ssm_c_im[j], ssm_d[j], w_glu[j], 256 // SUBLANES)
            zero = jnp.zeros((bsz, g * p), F32)
            xp2, hr, hi = _s5_layer(xp.reshape(bsz * t, d), zero, zero, t, bsz, row(norm_mix[i]), prm)
            xs2, hrs, his = _s5_layer(xs.reshape(bs * tq, d), state_ssm_re[j].reshape(bs, g * p).astype(F32),
                                      state_ssm_im[j].reshape(bs, g * p).astype(F32), tq, bs, row(norm_mix[i]), prm)
            xp, xs = xp2.reshape(bsz, t, d), xs2.reshape(bs, tq, d)
            sre_p.append(hr.reshape(bsz, g, p))
            sim_p.append(hi.reshape(bsz, g, p))
            sre_s.append(hrs.reshape(bs, g, p))
            sim_s.append(his.reshape(bs, g, p))
        last = i == depth - 1
        ffn_w = (row(norm_ffn[i]), w_ffn_up[i].astype(BF16), ffn_conv_w[i].astype(F32), row(ffn_conv_b[i]),
                 w_ffn_down[i].astype(BF16), row(norm_ple[i]), w_ple_proj[i].astype(BF16),
                 w_ple_gate[i].astype(BF16), row(norm_final))
        xp2, bp = _ffn_ple(xp.reshape(bsz * t, d), p_prompt[i].reshape(bsz * t, -1),
                           jnp.zeros((bsz, CONV_W - 1, f), F32), t, bsz, last, *ffn_w)
        xs2, bs_new = _ffn_ple(xs.reshape(bs * tq, d), p_sample[i].reshape(bs * tq, -1), state_conv[i].astype(F32),
                               tq, bs, last, *ffn_w)
        xp, xs = xp2.reshape(bsz, t, d), xs2.reshape(bs, tq, d)
        cb_p.append(bp)
        cb_s.append(bs_new)
    return (xp, xs, jnp.stack(kv_p), jnp.stack(kv_s), jnp.stack(win_p), jnp.stack(win_s),
            jnp.stack(sre_p), jnp.stack(sim_p), jnp.stack(sre_s), jnp.stack(sim_s),
            jnp.stack(cb_p), jnp.stack(cb_s))
```

```python
import functools
import math

import numpy as np
import jax
import jax.numpy as jnp
from jax import lax
from jax.experimental import pallas as pl
from jax.experimental.pallas import tpu as pltpu

F32 = jnp.float32
BF16 = jnp.bfloat16

N_HEADS = 16
HEAD_DIM = 64
N_KV_HEADS = 4
GROUP = N_HEADS // N_KV_HEADS
KV_COLS = N_KV_HEADS * HEAD_DIM
L_CMP = 32
D_CMP = 16
L_SEL = 64
TOP_N = 16
WINDOW = 512
SSM_CH = 16
STATE_P = 64
CONV_W = 3
NORM_EPS = 1e-6
NEG_INF = -1e30
FORCE_BONUS = 1e4
SEL_PER_CMP = L_SEL // D_CMP

LANES = 128
SUBLANES = 8
VMEM_LIMIT = 56 * 1024 * 1024

PROJ_ROWS = 512
LAYER_ROWS = 256
POOL_PAGES = 32
Q_TILE = 128
SEL_CHUNK = 512
SEL_UNROLL = 4
SSM_LANE_CHUNK = 512
SSM_GROUP_BLOCK = 16
FFN_CHUNKS = 2


def _cparams(sem):
    return pltpu.CompilerParams(dimension_semantics=sem, vmem_limit_bytes=VMEM_LIMIT)


def _const_spec(shape):
    nd = len(shape)
    return pl.BlockSpec(shape, lambda *_: (0,) * nd, pipeline_mode=pl.Buffered(1))


def _rms(x, gain):
    y = x * lax.rsqrt(jnp.mean(x * x, axis=-1, keepdims=True) + NORM_EPS)
    return y * gain


def _gelu(x):
    c = math.sqrt(2.0 / math.pi)
    return x * (0.5 * (1.0 + jnp.tanh(c * (x + 0.044715 * (x * x * x)))))


def _sigmoid(x):
    return 1.0 / (1.0 + jnp.exp(-x))


def _dot(a, b):
    return jnp.dot(a, b, preferred_element_type=F32)


def _dot_nt(a, b):
    return lax.dot_general(a, b, (((1,), (1,)), ((), ())), preferred_element_type=F32)


def _masked_softmax(s, mask):
    sm = jnp.where(mask, s, NEG_INF)
    m = jnp.max(sm, axis=-1, keepdims=True)
    e = jnp.where(mask, jnp.exp(sm - m), 0.0)
    den = jnp.sum(e, axis=-1, keepdims=True)
    return e / jnp.where(den > 0.0, den, 1.0)


def _exact_pool(x, pool_bf):
    hi = x.astype(BF16)
    r1 = x - hi.astype(F32)
    mid = r1.astype(BF16)
    lo = (r1 - mid.astype(F32)).astype(BF16)
    return _dot(hi, pool_bf) + _dot(mid, pool_bf) + _dot(lo, pool_bf)


def _exact_rows(sel_bf, x):
    hi = x.astype(BF16)
    r1 = x - hi.astype(F32)
    mid = r1.astype(BF16)
    lo = (r1 - mid.astype(F32)).astype(BF16)
    return _dot(sel_bf, hi) + _dot(sel_bf, mid) + _dot(sel_bf, lo)


def _topk_mask(score, k):
    s = score.T
    cand = lax.broadcasted_iota(jnp.int32, s.shape, 0).astype(F32)
    sel = jnp.zeros(s.shape, F32)
    for _ in range(k):
        m = jnp.max(s, axis=0, keepdims=True)
        first = jnp.min(jnp.where(s == m, cand, float(LANES)), axis=0, keepdims=True)
        hit = cand == first
        sel = jnp.where(hit, 1.0, sel)
        s = jnp.where(hit, -jnp.inf, s)
    return sel.T


def _slot_ids():
    return lax.broadcasted_iota(jnp.int32, (1, KV_COLS), 1) // HEAD_DIM


def _place_slot(x, src_slot, dst_slot):
    shift = (HEAD_DIM * (dst_slot - src_slot)) % KV_COLS
    y = pltpu.roll(x, shift, 1) if shift else x
    return jnp.where(_slot_ids() == dst_slot, y, 0.0)


def _selection_scores(imp, tpos):
    blk = lax.broadcasted_iota(jnp.int32, imp.shape, 1)
    cur = tpos // L_SEL
    forced = (blk == 0) | (blk == cur) | (blk == cur - 1)
    started = blk * L_SEL <= tpos
    return jnp.where(started, imp + jnp.where(forced, FORCE_BONUS, 0.0), NEG_INF)


def _attn_in_kernel(seq_len, x_ref, gain_ref, wm_ref, wg_ref, q_ref, kv_ref, gates_ref, *refs):
    tm, d = x_ref.shape
    u = _rms(x_ref[...], gain_ref[...]).astype(BF16)
    z = _dot(u, wm_ref[...])
    q_ref[...] = z[:, :d]
    kv = z[:, d:d + 4 * KV_COLS]
    kv_ref[...] = kv
    win = z[:, d + 4 * KV_COLS:d + 6 * KV_COLS]
    gates_ref[...] = _sigmoid(_dot(u, wg_ref[...]))
    if seq_len is None:
        (win_ref,) = refs
        win_ref[...] = win
        return
    ksel_ref, vsel_ref, kwin_ref, vwin_ref, kvt_ref, wint_ref = refs
    pos = (pl.program_id(0) % (seq_len // tm)) * tm + lax.broadcasted_iota(jnp.int32, (tm, 1), 0)
    aug = _position_aug(((pos // LANES) * LANES).astype(F32), (pos % LANES).astype(F32))
    lane = lax.broadcasted_iota(jnp.int32, (1, LANES), 1)
    for g in range(N_KV_HEADS):
        sl = slice(g * LANES, (g + 1) * LANES)
        ksel_ref[:, sl] = jnp.where(lane < AUG_LANE, _head_slot(kv[:, 2 * KV_COLS:3 * KV_COLS], g), aug).astype(BF16)
        kwin_ref[:, sl] = jnp.where(lane < AUG_LANE, _head_slot(win[:, :KV_COLS], g), aug).astype(BF16)
    vsel_ref[...] = kv[:, 3 * KV_COLS:4 * KV_COLS].astype(BF16)
    vwin_ref[...] = win[:, KV_COLS:].astype(BF16)
    kvt_ref[0] = kv.T
    wint_ref[0] = win.T


def _attn_in_proj(x2, gain, w_main, w_gate, seq_len=None):
    rows, d = x2.shape
    tm = min(PROJ_ROWS, rows)
    nmain = w_main.shape[1]
    row = lambda n: pl.BlockSpec((tm, n), lambda i: (i, 0))
    outs = [(d, F32), (4 * KV_COLS, F32), (LANES, F32)]
    if seq_len is None:
        outs += [(2 * KV_COLS, F32)]
    else:
        outs += [(N_KV_HEADS * LANES, BF16), (KV_COLS, BF16), (N_KV_HEADS * LANES, BF16), (KV_COLS, BF16)]
    out_specs = [row(n) for n, _ in outs]
    out_shape = [jax.ShapeDtypeStruct((rows, n), dt) for n, dt in outs]
    if seq_len is not None:
        nt = seq_len // tm
        for n in (4 * KV_COLS, 2 * KV_COLS):
            out_specs.append(pl.BlockSpec((1, n, tm), lambda i: (i // nt, 0, i % nt)))
            out_shape.append(jax.ShapeDtypeStruct((rows // seq_len, n, seq_len), F32))
    return pl.pallas_call(
        functools.partial(_attn_in_kernel, seq_len),
        grid=(rows // tm,),
        in_specs=[row(d), _const_spec((1, d)), _const_spec((d, nmain)), _const_spec((d, LANES))],
        out_specs=out_specs,
        out_shape=out_shape,
        compiler_params=_cparams(("arbitrary",)),
        name="attn_in_proj",
    )(x2, gain, w_main, w_gate)


def _matmul_res_kernel(a_ref, w_ref, x_ref, o_ref):
    o_ref[...] = x_ref[...] + _dot(a_ref[...].astype(BF16), w_ref[...])


def _matmul_res(a, w, x):
    rows, k = a.shape
    n = w.shape[1]
    tm = min(PROJ_ROWS, rows)
    return pl.pallas_call(
        _matmul_res_kernel,
        grid=(rows // tm,),
        in_specs=[pl.BlockSpec((tm, k), lambda i: (i, 0)), _const_spec((k, n)),
                  pl.BlockSpec((tm, n), lambda i: (i, 0))],
        out_specs=pl.BlockSpec((tm, n), lambda i: (i, 0)),
        out_shape=jax.ShapeDtypeStruct((rows, n), F32),
        compiler_params=_cparams(("arbitrary",)),
        name="matmul_res",
    )(a, w, x)


def _finish_compress(a, b, w2):
    nseg = a.shape[0]
    pre = a + pltpu.roll(b, nseg - 1, 0)
    out = _dot(_gelu(pre).astype(BF16), w2)
    rowi = lax.broadcasted_iota(jnp.int32, (nseg, 1), 0)
    return jnp.where(rowi < nseg - 1, out, 0.0)


def _segment_sums_kernel(x_ref, pea_ref, peb_ref, w1a_ref, w1b_ref, w2_ref, ck_ref, cv_ref, acc_a, acc_b):
    kind = pl.program_id(1)
    l = pl.program_id(2)
    nseg = acc_a.shape[0]

    @pl.when(l == 0)
    def _():
        acc_a[...] = jnp.zeros(acc_a.shape, F32)
        acc_b[...] = jnp.zeros(acc_b.shape, F32)

    xs = x_ref[0]
    acc_a[...] += _dot((xs + pea_ref[0, 0]).astype(BF16), w1a_ref[0, 0])
    acc_b[...] += _dot((xs + peb_ref[0, 0]).astype(BF16), w1b_ref[0, 0])

    @pl.when((l == D_CMP - 1) & (kind == 0))
    def _():
        kc = _finish_compress(acc_a[...], acc_b[...], w2_ref[0])
        n_i = lax.broadcasted_iota(jnp.int32, (nseg, 1), 0)
        hi = ((n_i * D_CMP) // LANES * LANES).astype(F32)
        lo = ((n_i * D_CMP) % LANES).astype(F32) + (L_CMP - 1) / 2
        aug = _position_aug(hi, lo)
        lane = lax.broadcasted_iota(jnp.int32, (1, LANES), 1)
        for g in range(N_KV_HEADS):
            ck_ref[0, :, g * LANES:(g + 1) * LANES] = jnp.where(lane < AUG_LANE, _head_slot(kc, g), aug).astype(BF16)

    @pl.when((l == D_CMP - 1) & (kind == 1))
    def _():
        cv_ref[0] = _finish_compress(acc_a[...], acc_b[...], w2_ref[0]).astype(BF16)


def _segment_sums(rows3, pe_t, w1_bd, w2_bd):
    bsz, nseg, _ = rows3.shape
    pe4 = pe_t.reshape(2, L_CMP, 1, KV_COLS)
    return pl.pallas_call(
        _segment_sums_kernel,
        grid=(bsz, 2, D_CMP),
        in_specs=[pl.BlockSpec((1, nseg, KV_COLS), lambda i, k, l: (i, 0, l * 4 + k)),
                  pl.BlockSpec((1, 1, 1, KV_COLS), lambda i, k, l: (k, l, 0, 0)),
                  pl.BlockSpec((1, 1, 1, KV_COLS), lambda i, k, l: (k, D_CMP + l, 0, 0)),
                  pl.BlockSpec((1, 1, KV_COLS, KV_COLS), lambda i, k, l: (k, l, 0, 0)),
                  pl.BlockSpec((1, 1, KV_COLS, KV_COLS), lambda i, k, l: (k, D_CMP + l, 0, 0)),
                  pl.BlockSpec((1, KV_COLS, KV_COLS), lambda i, k, l: (k, 0, 0))],
        out_specs=[pl.BlockSpec((1, nseg, N_KV_HEADS * LANES), lambda i, k, l: (i, 0, 0)),
                   pl.BlockSpec((1, nseg, KV_COLS), lambda i, k, l: (i, 0, 0))],
        out_shape=[jax.ShapeDtypeStruct((bsz, nseg, N_KV_HEADS * LANES), BF16),
                   jax.ShapeDtypeStruct((bsz, nseg, KV_COLS), BF16)],
        scratch_shapes=[pltpu.VMEM((nseg, KV_COLS), F32), pltpu.VMEM((nseg, KV_COLS), F32)],
        compiler_params=_cparams(("arbitrary", "arbitrary", "arbitrary")),
        name="segment_sums",
    )(rows3, pe4, pe4, w1_bd, w1_bd, w2_bd)


def _pool_sums_kernel(pp, page, x_ref, pe_ref, w1_ref, a_ref, b_ref, rows_s):
    ftiles = 2 * KV_COLS // LANES

    def to_rows(p, carry):
        xt = x_ref[p]
        r0 = pl.multiple_of(p * page, page)
        for c in range(ftiles):
            rows_s[c, pl.ds(r0, page), :] = xt[c * LANES:(c + 1) * LANES, :].T
        return carry

    lax.fori_loop(0, pp, to_rows, 0, unroll=4)
    nseg = pp * page // D_CMP
    tpk = KV_COLS // LANES
    for kind in range(2):
        a = jnp.zeros((nseg, KV_COLS), F32)
        b = jnp.zeros((nseg, KV_COLS), F32)
        for l in range(D_CMP):
            xs = jnp.concatenate([rows_s[kind * tpk + c, pl.ds(l, nseg, stride=D_CMP), :] for c in range(tpk)],
                                 axis=1)
            a = a + _dot((xs + pe_ref[kind, l:l + 1, :]).astype(BF16), w1_ref[kind, l])
            b = b + _dot((xs + pe_ref[kind, D_CMP + l:D_CMP + l + 1, :]).astype(BF16), w1_ref[kind, D_CMP + l])
        a_ref[:, kind * KV_COLS:(kind + 1) * KV_COLS] = a
        b_ref[:, kind * KV_COLS:(kind + 1) * KV_COLS] = b


def _pool_segment_sums(cache_t, pe_t, w1_bd, pp):
    n_pool, _, page = cache_t.shape
    assert page == LANES and n_pool % pp == 0
    nseg = pp * page // D_CMP
    return pl.pallas_call(
        functools.partial(_pool_sums_kernel, pp, page),
        grid=(n_pool // pp,),
        in_specs=[pl.BlockSpec((pp, 2 * KV_COLS, page), lambda i: (i, 0, 0)),
                  _const_spec(pe_t.shape), _const_spec(w1_bd.shape)],
        out_specs=[pl.BlockSpec((nseg, 2 * KV_COLS), lambda i: (i, 0))] * 2,
        out_shape=[jax.ShapeDtypeStruct((n_pool * page // D_CMP, 2 * KV_COLS), F32)] * 2,
        scratch_shapes=[pltpu.VMEM((2 * KV_COLS // LANES, pp * page, LANES), F32)],
        compiler_params=_cparams(("arbitrary",)),
        name="pool_segment_sums",
    )(cache_t, pe_t, w1_bd)


AUG_LANE = HEAD_DIM
N_PIECES = 3
SOFTMAX_ROWS = 32


def _position_aug(hi, lo):
    lane = lax.broadcasted_iota(jnp.int32, (1, LANES), 1)
    in_hi = (lane >= AUG_LANE) & (lane < AUG_LANE + N_PIECES)
    in_lo = (lane >= AUG_LANE + N_PIECES) & (lane < AUG_LANE + 2 * N_PIECES)
    return jnp.where(in_hi, hi, jnp.where(in_lo, lo, 0.0))


def _head_slot(x, g):
    tile = x[:, LANES * (g // 2):LANES * (g // 2 + 1)]
    return pltpu.roll(tile, HEAD_DIM, 1) if g % 2 else tile


def _for_row_blocks(n_rows, body):
    for r0 in range(0, n_rows, SOFTMAX_ROWS):
        body(r0)


def _nsa_prompt_kernel(nc, q_ref, gates_ref, ck_ref, cv_ref, ksel_ref, vsel_ref, kwin_ref, vwin_ref,
                       onehot_ref, pool_ref, qaug_ref, o_ref,
                       qa_s, s_s, s2_s, p_s, p2_s, m_s, l_s, al_s, acc_s, ocmp_s, imp_s, score_s, selb_s):
    tq = Q_TILE
    rows = GROUP * tq
    rb = SOFTMAX_ROWS
    ncp = ck_ref.shape[1]
    qb = pl.program_id(1)
    q0 = qb * tq
    gates = gates_ref[0]
    lane = lax.broadcasted_iota(jnp.int32, (1, LANES), 1)
    tpos1 = q0 + lax.broadcasted_iota(jnp.int32, (tq, 1), 0)
    row_pos = lambda r0: q0 + r0 % tq + lax.broadcasted_iota(jnp.int32, (rb, 1), 0)
    win_len = WINDOW + tq
    w0 = pl.multiple_of(jnp.maximum(q0 - WINDOW, 0), tq)
    n_past_chunks = q0 // SEL_CHUNK
    pair = lambda g: slice(LANES * (g // 2), LANES * (g // 2 + 1))
    slot = lambda g: slice(LANES * g, LANES * (g + 1))

    def softmax_once(r0, width, valid, sbuf, pbuf):
        sb = jnp.where(valid, sbuf[pl.ds(r0, rb), :width], NEG_INF)
        m = jnp.max(sb, axis=-1, keepdims=True)
        e = jnp.exp(sb - m)
        den = jnp.sum(e, axis=-1, keepdims=True)
        p = e * jnp.where(m > 0.5 * NEG_INF, 1.0 / den, 0.0)
        pbuf[pl.ds(r0, rb), :width] = p.astype(BF16)
        return p

    n_i = lax.broadcasted_iota(jnp.int32, (1, ncp), 1)
    c_end = jnp.where(n_i < nc, n_i * D_CMP + (L_CMP - 1), jnp.int32(2 ** 30))

    for g in range(N_KV_HEADS):
        for r in range(GROUP):
            h = g * GROUP + r
            qh = q_ref[0, :, LANES * (h // 2):LANES * (h // 2 + 1)] * (HEAD_DIM ** -0.5)
            if h % 2:
                qh = pltpu.roll(qh, HEAD_DIM, 1)
            qa_s[g, r * tq:(r + 1) * tq, :] = jnp.where(lane < AUG_LANE, qh, qaug_ref[h:h + 1, :]).astype(BF16)

        s_s[:, :ncp] = _dot_nt(qa_s[g], ck_ref[0, :, slot(g)])
        imp_s[...] = jnp.zeros(imp_s.shape, F32)

        def cmp_block(r0):
            p = softmax_once(r0, ncp, c_end <= row_pos(r0), s_s, p_s)
            imp_s[pl.ds(r0 % tq, rb), :] += p

        _for_row_blocks(rows, cmp_block)
        ocmp_s[g] = _dot(p_s[:, :ncp], cv_ref[0, :, pair(g)])
        score_s[g * tq:(g + 1) * tq, :] = _selection_scores(_exact_pool(imp_s[...], pool_ref[...]), tpos1)

    selb_s[...] = jnp.where(_topk_mask(score_s[...], TOP_N) > 0.5, 0.0, NEG_INF).astype(BF16)

    for g in range(N_KV_HEADS):
        sel_bias = selb_s[g * tq:(g + 1) * tq, :]
        q_full = jnp.concatenate([qa_s[g], jnp.concatenate([sel_bias] * GROUP, axis=0)], axis=1)

        m_s[...] = jnp.full(m_s.shape, NEG_INF, F32)
        l_s[...] = jnp.zeros(l_s.shape, F32)
        acc_s[...] = jnp.zeros(acc_s.shape, F32)

        def sel_scores(c, buf):
            k0 = pl.multiple_of(c * SEL_CHUNK, SEL_CHUNK)
            k_full = jnp.concatenate([ksel_ref[0, pl.ds(k0, SEL_CHUNK), slot(g)],
                                      onehot_ref[pl.ds(k0, SEL_CHUNK), :]], axis=1)
            buf[:, :SEL_CHUNK] = _dot_nt(q_full, k_full)

        def sel_consume(c, buf, causal):
            pbuf = p_s if buf is s_s else p2_s
            k0 = pl.multiple_of(c * SEL_CHUNK, SEL_CHUNK)
            kpos = k0 + lax.broadcasted_iota(jnp.int32, (1, SEL_CHUNK), 1)

            def block(r0):
                rs = pl.ds(r0, rb)
                sb = buf[rs, :SEL_CHUNK]
                if causal:
                    sb = jnp.where(kpos <= row_pos(r0), sb, NEG_INF)
                m_old = m_s[rs, :]
                m_new = jnp.maximum(m_old, jnp.max(sb, axis=-1, keepdims=True))
                alpha = jnp.exp(m_old - m_new)
                p = jnp.exp(sb - jnp.tile(m_new, (1, SEL_CHUNK // LANES)))
                l_s[rs, :] = alpha * l_s[rs, :] + jnp.sum(p, axis=-1, keepdims=True)
                m_s[rs, :] = m_new
                al_s[rs, :] = alpha
                pbuf[rs, :SEL_CHUNK] = p.astype(BF16)

            _for_row_blocks(rows, block)
            acc_s[...] = al_s[...] * acc_s[...] + _dot(pbuf[:, :SEL_CHUNK], vsel_ref[0, pl.ds(k0, SEL_CHUNK), pair(g)])

        def run_chunks(first, count, ends_causal):
            bufs = (s_s, s2_s)
            for k in range(count):
                closes = ends_causal and k == count - 1
                if not closes:
                    sel_scores(first + k + 1, bufs[(k + 1) % 2])
                sel_consume(first + k, bufs[k % 2], closes)

        sel_scores(0, s_s)

        def past_group(i, carry):
            run_chunks(i * SEL_UNROLL, SEL_UNROLL, False)
            return carry

        n_groups = n_past_chunks // SEL_UNROLL
        lax.fori_loop(0, n_groups, past_group, 0)
        for rem in range(SEL_UNROLL):
            @pl.when(n_past_chunks - n_groups * SEL_UNROLL == rem)
            def _():
                run_chunks(n_groups * SEL_UNROLL, rem + 1, True)

        o_sel = acc_s[...] / l_s[...]

        s_s[:, :win_len] = _dot_nt(qa_s[g], kwin_ref[0, pl.ds(w0, win_len), slot(g)])
        wpos = w0 + lax.broadcasted_iota(jnp.int32, (1, win_len), 1)

        def win_block(r0):
            wd = row_pos(r0) - wpos
            softmax_once(r0, win_len, (wd >= 0) & (wd < WINDOW), s_s, p_s)

        _for_row_blocks(rows, win_block)
        o_win = _dot(p_s[:, :win_len], vwin_ref[0, pl.ds(w0, win_len), pair(g)])
        o_cmp = ocmp_s[g]

        halves = []
        for r in range(GROUP):
            c0 = 3 * (g * GROUP + r)
            rs = slice(r * tq, (r + 1) * tq)
            o_r = (gates[:, c0:c0 + 1] * o_cmp[rs] + gates[:, c0 + 1:c0 + 2] * o_sel[rs]
                   + gates[:, c0 + 2:c0 + 3] * o_win[rs])
            halves.append(pltpu.roll(o_r, HEAD_DIM, 1) if (r - g) % 2 else o_r)
        for j in range(GROUP // 2):
            tile = jnp.where(lane < HEAD_DIM, halves[2 * j], halves[2 * j + 1])
            o_ref[0, :, LANES * (2 * g + j):LANES * (2 * g + j + 1)] = tile.astype(o_ref.dtype)


def _nsa_prompt_attend(q, gates, cmp_k, cmp_v, ksel, vsel, kwin, vwin, onehot, pool, qaug):
    bsz, t, d = q.shape
    nc = (t - L_CMP) // D_CMP + 1
    ncp = cmp_k.shape[1]
    rows = GROUP * Q_TILE
    width = max(ncp, SEL_CHUNK, WINDOW + Q_TILE)
    seq = lambda a: pl.BlockSpec((1,) + a.shape[1:], lambda b, i: (b, 0, 0), pipeline_mode=pl.Buffered(1))
    stat = pltpu.VMEM((rows, LANES), F32)
    return pl.pallas_call(
        functools.partial(_nsa_prompt_kernel, nc),
        grid=(bsz, t // Q_TILE),
        in_specs=[pl.BlockSpec((1, Q_TILE, d), lambda b, i: (b, i, 0)),
                  pl.BlockSpec((1, Q_TILE, LANES), lambda b, i: (b, i, 0)),
                  seq(cmp_k), seq(cmp_v), seq(ksel), seq(vsel), seq(kwin), seq(vwin),
                  _const_spec(onehot.shape), _const_spec(pool.shape), _const_spec(qaug.shape)],
        out_specs=pl.BlockSpec((1, Q_TILE, d), lambda b, i: (b, i, 0)),
        out_shape=jax.ShapeDtypeStruct((bsz, t, d), BF16),
        scratch_shapes=[pltpu.VMEM((N_KV_HEADS, rows, LANES), BF16),
                        pltpu.VMEM((rows, width), F32),
                        pltpu.VMEM((rows, SEL_CHUNK), F32),
                        pltpu.VMEM((rows, width), BF16),
                        pltpu.VMEM((rows, SEL_CHUNK), BF16),
                        stat, stat, stat, stat,
                        pltpu.VMEM((N_KV_HEADS, rows, LANES), F32),
                        pltpu.VMEM((Q_TILE, ncp), F32),
                        pltpu.VMEM((N_KV_HEADS * Q_TILE, LANES), F32),
                        pltpu.VMEM((N_KV_HEADS * Q_TILE, LANES), BF16)],
        compiler_params=_cparams(("arbitrary", "arbitrary")),
        name="nsa_prompt_attend",
    )(q, gates, cmp_k, cmp_v, ksel, vsel, kwin, vwin, onehot, pool, qaug)


def _sample_qpad(qf, tq):
    parts = []
    for g in range(N_KV_HEADS):
        piece = qf[:, g * KV_COLS:(g + 1) * KV_COLS]
        for r in range(GROUP):
            parts.append(_place_slot(piece, r, g))
    return jnp.concatenate(parts, axis=0).astype(BF16)


def _sample_rows_to_tokens(o, tq):
    chunks = []
    for g in range(N_KV_HEADS):
        chunk = jnp.zeros((tq, KV_COLS), F32)
        for r in range(GROUP):
            i0 = (g * GROUP + r) * tq
            chunk = chunk + _place_slot(o[i0:i0 + tq], g, r)
        chunks.append(chunk)
    return jnp.concatenate(chunks, axis=1)


def _nsa_sample_cmp_kernel(n_pages, past_len, tq, pt_ref, q_ref, w2_ref, pool_ref, slope_ref, *refs):
    a_refs = refs[:n_pages]
    b_refs = refs[n_pages:2 * n_pages]
    ocmp_ref, sel_ref = refs[2 * n_pages:]
    del pt_ref
    rows = N_HEADS * tq
    a = jnp.concatenate([r[0] for r in a_refs], axis=0)
    b = jnp.concatenate([r[0] for r in b_refs], axis=0)
    nseg = a.shape[0]
    kc = _finish_compress(a[:, :KV_COLS], b[:, :KV_COLS], w2_ref[0]).astype(BF16)
    vc = _finish_compress(a[:, KV_COLS:], b[:, KV_COLS:], w2_ref[1]).astype(BF16)

    qpad = _sample_qpad(q_ref[0] * (HEAD_DIM ** -0.5), tq)
    slope = slope_ref[...]
    rowi = lax.broadcasted_iota(jnp.int32, (rows, 1), 0)
    tpos = past_len + rowi % tq
    tposf = tpos.astype(F32)
    nc = (past_len + tq - L_CMP) // D_CMP + 1
    n_i = lax.broadcasted_iota(jnp.int32, (1, nseg), 1)
    c_end = jnp.where(n_i < nc, n_i * D_CMP + (L_CMP - 1), jnp.int32(2 ** 30))
    center = (n_i * D_CMP).astype(F32) + (L_CMP - 1) / 2
    s = _dot_nt(qpad, kc) - slope * (tposf - center)
    p_cmp = _masked_softmax(s, c_end <= tpos)
    ocmp_ref[0] = _dot(p_cmp.astype(BF16), vc)

    imps = []
    for g in range(N_KV_HEADS):
        i0 = g * GROUP * tq
        imp = p_cmp[i0:i0 + tq]
        for r in range(1, GROUP):
            imp = imp + p_cmp[i0 + r * tq:i0 + (r + 1) * tq]
        imps.append(imp)
    tpos_g = past_len + lax.broadcasted_iota(jnp.int32, (N_KV_HEADS * tq, 1), 0) % tq
    score = _selection_scores(_exact_pool(jnp.concatenate(imps, axis=0), pool_ref[...]), tpos_g)
    n_sel_rows = N_KV_HEADS * tq
    pad = jnp.full((LANES - n_sel_rows, LANES), NEG_INF, F32)
    sel = _topk_mask(jnp.concatenate([score, pad], axis=0), TOP_N)[:n_sel_rows]
    sel_ref[0] = jnp.concatenate([sel[g * tq:(g + 1) * tq] for g in range(N_KV_HEADS) for _ in range(GROUP)],
                                 axis=0).astype(BF16)


def _nsa_sample_cmp(page_table, q, a_all, b_all, w2_bd, pool, slope_rows, page, past_len):
    bs, tq, d = q.shape
    n_pages = page_table.shape[1]
    seg_pp = page // D_CMP
    rows = N_HEADS * tq
    a3 = a_all.reshape(-1, seg_pp, 2 * KV_COLS)
    b3 = b_all.reshape(-1, seg_pp, 2 * KV_COLS)

    def page_spec(p):
        return pl.BlockSpec((1, seg_pp, 2 * KV_COLS), lambda s, pt: (pt[s, p], 0, 0))

    cst = lambda shape: pl.BlockSpec(shape, lambda s, pt: (0,) * len(shape))
    grid_spec = pltpu.PrefetchScalarGridSpec(
        num_scalar_prefetch=1,
        grid=(bs,),
        in_specs=[pl.BlockSpec((1, tq, d), lambda s, pt: (s, 0, 0)),
                  cst(w2_bd.shape), cst(pool.shape), cst(slope_rows.shape)]
                 + [page_spec(p) for p in range(n_pages)] * 2,
        out_specs=[pl.BlockSpec((1, rows, KV_COLS), lambda s, pt: (s, 0, 0)),
                   pl.BlockSpec((1, rows, LANES), lambda s, pt: (s, 0, 0))],
    )
    return pl.pallas_call(
        functools.partial(_nsa_sample_cmp_kernel, n_pages, past_len, tq),
        grid_spec=grid_spec,
        out_shape=[jax.ShapeDtypeStruct((bs, rows, KV_COLS), F32),
                   jax.ShapeDtypeStruct((bs, rows, LANES), BF16)],
        compiler_params=_cparams(("arbitrary",)),
        name="nsa_sample_cmp",
    )(page_table, q, w2_bd, pool, slope_rows, *([a3] * n_pages), *([b3] * n_pages))


def _nsa_sample_attend_kernel(n_pages, page, tq, pt_ref, q_ref, gates_ref, kvn_ref, winn_ref, winp_ref,
                              ocmp_ref, sel_ref, expand_ref, slope_ref, *refs):
    page_refs = refs[:n_pages]
    o_ref, newwin_ref = refs[n_pages:]
    del pt_ref
    rows = N_HEADS * tq
    past_len = n_pages * page
    wbuf = winp_ref.shape[2]
    qpad = _sample_qpad(q_ref[0] * (HEAD_DIM ** -0.5), tq)
    slope = slope_ref[...]
    rowi = lax.broadcasted_iota(jnp.int32, (rows, 1), 0)
    tpos = past_len + rowi % tq
    tposf = tpos.astype(F32)
    newpos = past_len + lax.broadcasted_iota(jnp.int32, (1, LANES), 1)
    pad_rows = lambda a: jnp.concatenate([a, jnp.zeros((LANES - tq, a.shape[1]), F32)], axis=0)

    kvn = pad_rows(kvn_ref[0])
    k_new = kvn[:, 2 * KV_COLS:3 * KV_COLS].astype(BF16)
    v_new = kvn[:, 3 * KV_COLS:4 * KV_COLS].astype(BF16)
    scores = [_dot(qpad, r[0, :KV_COLS, :].astype(BF16)) for r in page_refs]
    scores.append(_dot_nt(qpad, k_new))
    s = jnp.concatenate(scores, axis=1)
    nk = past_len + LANES
    kpos = lax.broadcasted_iota(jnp.int32, (1, nk), 1)
    picked = _dot(sel_ref[0], expand_ref[...]) > 0.5
    s = s - slope * (tposf - kpos.astype(F32))
    p = _masked_softmax(s, picked & (kpos <= tpos)).astype(BF16)
    o_sel = _dot(p[:, past_len:], v_new)
    for i, r in enumerate(page_refs):
        o_sel = o_sel + _dot_nt(p[:, i * page:(i + 1) * page], r[0, KV_COLS:, :].astype(BF16))

    winp = winp_ref[0]
    winn = pad_rows(winn_ref[0])
    winn_t = pltpu.roll(winn.T, LANES - tq, 1)
    lane_w = lax.broadcasted_iota(jnp.int32, (1, wbuf), 1)
    newwin_ref[0] = jnp.where(lane_w >= wbuf - tq, jnp.concatenate([winn_t] * (wbuf // LANES), axis=1),
                              pltpu.roll(winp, wbuf - tq, 1))
    sp = _dot(qpad, winp[:KV_COLS, :].astype(BF16))
    sn = _dot_nt(qpad, winn[:, :KV_COLS].astype(BF16))
    wpos_p = past_len - wbuf + lax.broadcasted_iota(jnp.int32, (1, wbuf), 1)
    wpos = jnp.concatenate([wpos_p, newpos], axis=1)
    wd = tpos - wpos
    s = jnp.concatenate([sp, sn], axis=1) - slope * wd.astype(F32)
    p = _masked_softmax(s, (wd >= 0) & (wd < WINDOW) & (wpos >= 0)).astype(BF16)
    o_win = _dot_nt(p[:, :wbuf], winp[KV_COLS:, :].astype(BF16)) + _dot(p[:, wbuf:], winn[:, KV_COLS:].astype(BF16))

    gates = gates_ref[0]
    o_cmp = ocmp_ref[0]
    merged = []
    for h in range(N_HEADS):
        rs = slice(h * tq, (h + 1) * tq)
        merged.append(gates[:, 3 * h:3 * h + 1] * o_cmp[rs] + gates[:, 3 * h + 1:3 * h + 2] * o_sel[rs]
                      + gates[:, 3 * h + 2:3 * h + 3] * o_win[rs])
    o_ref[0] = _sample_rows_to_tokens(jnp.concatenate(merged, axis=0), tq).astype(o_ref.dtype)


def _nsa_sample_attend(page_table, cache_t, q, gates, kv_new, win_new, win_past_t, o_cmp, sel, expand, slope_rows):
    bs, tq, d = q.shape
    n_pages = page_table.shape[1]
    page = cache_t.shape[2]
    wbuf = win_past_t.shape[2]
    rows = N_HEADS * tq

    def page_spec(p):
        return pl.BlockSpec((1, 2 * KV_COLS, page), lambda s, pt: (pt[s, p], 1, 0))

    per = lambda r, n: pl.BlockSpec((1, r, n), lambda s, pt: (s, 0, 0))
    cst = lambda shape: pl.BlockSpec(shape, lambda s, pt: (0,) * len(shape))
    grid_spec = pltpu.PrefetchScalarGridSpec(
        num_scalar_prefetch=1,
        grid=(bs,),
        in_specs=[per(tq, d), per(tq, LANES), per(tq, 4 * KV_COLS), per(tq, 2 * KV_COLS),
                  per(2 * KV_COLS, wbuf), per(rows, KV_COLS), per(rows, LANES),
                  cst(expand.shape), cst(slope_rows.shape)]
                 + [page_spec(p) for p in range(n_pages)],
        out_specs=[per(tq, d), per(2 * KV_COLS, wbuf)],
    )
    return pl.pallas_call(
        functools.partial(_nsa_sample_attend_kernel, n_pages, page, tq),
        grid_spec=grid_spec,
        out_shape=[jax.ShapeDtypeStruct((bs, tq, d), BF16),
                   jax.ShapeDtypeStruct((bs, 2 * KV_COLS, wbuf), F32)],
        compiler_params=_cparams(("arbitrary",)),
        name="nsa_sample_attend",
    )(page_table, q, gates, kv_new, win_new, win_past_t, o_cmp, sel, expand, slope_rows,
      *([cache_t] * n_pages))


def _cmul_add(ar, ai, hr, hi, xr, xi):
    return ar * hr - ai * hi + xr, ar * hi + ai * hr + xi


def _s5_kernel(mode, tm, x_ref, h0r_ref, h0i_ref, gain_ref, bre_ref, bim_ref, are_ref, aim_ref,
               pwr_ref, pwi_ref, cre_ref, cim_ref, d_ref, wglu_ref, perm_ref, permt_ref,
               xo_ref, hfr_ref, hfi_ref, hr_s, hi_s, cr_s, ci_s):
    d = x_ref.shape[1]
    nstate = cr_s.shape[1]
    nblk = bre_ref.shape[0]
    cb = d // nblk
    sb = nstate // nblk
    tpb = sb // LANES
    x = x_ref[...]
    uf = _rms(x, gain_ref[...])
    ub = uf.astype(BF16)

    if mode == "carry":
        nj = tm // SUBLANES
        ub = _dot(perm_ref[...], ub).astype(BF16)
    for j in range(nblk):
        uj = ub[:, j * cb:(j + 1) * cb]
        br = _dot(uj, bre_ref[j])
        bi = _dot(uj, bim_ref[j])
        for k in range(tpb):
            hr_s[j * tpb + k] = br[:, k * LANES:(k + 1) * LANES]
            hi_s[j * tpb + k] = bi[:, k * LANES:(k + 1) * LANES]

    lc = SSM_LANE_CHUNK
    par = lc // LANES
    if mode == "carry":
        @pl.when(pl.program_id(1) == 0)
        def _():
            cr_s[...] = jnp.broadcast_to(h0r_ref[0], cr_s.shape)
            ci_s[...] = jnp.broadcast_to(h0i_ref[0], ci_s.shape)

        sub = lax.broadcasted_iota(jnp.int32, (SUBLANES, 1), 0)
        for c in range(nstate // lc):
            sl = slice(c * lc, (c + 1) * lc)
            ar = jnp.broadcast_to(are_ref[:, sl], (SUBLANES, lc))
            ai = jnp.broadcast_to(aim_ref[:, sl], (SUBLANES, lc))

            tiles = range(c * par, (c + 1) * par)
            lane = lambda v, k: v[:, k * LANES:(k + 1) * LANES]

            def p1(jj, carry):
                st, mult = carry
                idx = pl.ds(pl.multiple_of(jj * SUBLANES, SUBLANES), SUBLANES)
                out = []
                for k, q in enumerate(tiles):
                    nr, ni = _cmul_add(mult[2 * k], mult[2 * k + 1], st[2 * k], st[2 * k + 1],
                                       hr_s[q, idx, :], hi_s[q, idx, :])
                    hr_s[q, idx, :] = nr
                    hi_s[q, idx, :] = ni
                    out += [nr, ni]
                return tuple(out), mult

            z = jnp.zeros((SUBLANES, LANES), F32)
            mult0 = tuple(lane(v, k) for k in range(par) for v in (ar, ai))
            lf, _ = lax.fori_loop(0, nj, p1, ((z,) * (2 * par), mult0))
            lfr = jnp.concatenate(lf[0::2], axis=1)
            lfi = jnp.concatenate(lf[1::2], axis=1)

            alr = pwr_ref[nj - 1:nj, sl]
            ali = pwi_ref[nj - 1:nj, sl]
            inr = cr_s[0:1, sl]
            ini = ci_s[0:1, sl]
            hin_r = jnp.zeros((SUBLANES, lc), F32)
            hin_i = jnp.zeros((SUBLANES, lc), F32)
            for s in range(SUBLANES):
                hin_r = jnp.where(sub == s, inr, hin_r)
                hin_i = jnp.where(sub == s, ini, hin_i)
                inr, ini = _cmul_add(alr, ali, inr, ini, lfr[s:s + 1], lfi[s:s + 1])
            cr_s[:, sl] = jnp.broadcast_to(inr, (SUBLANES, lc))
            ci_s[:, sl] = jnp.broadcast_to(ini, (SUBLANES, lc))

            def p2(jj, carry):
                idx = pl.ds(pl.multiple_of(jj * SUBLANES, SUBLANES), SUBLANES)
                pr = pwr_ref[pl.ds(jj, 1), sl]
                pi_ = pwi_ref[pl.ds(jj, 1), sl]
                for k, q in enumerate(tiles):
                    nr, ni = _cmul_add(lane(pr, k), lane(pi_, k), lane(hin_r, k), lane(hin_i, k),
                                       hr_s[q, idx, :], hi_s[q, idx, :])
                    hr_s[q, idx, :] = nr
                    hi_s[q, idx, :] = ni
                return carry

            lax.fori_loop(0, nj, p2, 0, unroll=4)

        @pl.when(pl.program_id(1) == pl.num_programs(1) - 1)
        def _():
            hfr_ref[0] = cr_s[0:1, :]
            hfi_ref[0] = ci_s[0:1, :]
    else:
        nseq = tm // mode
        for q in range(nstate // LANES):
            sl = slice(q * LANES, (q + 1) * LANES)
            ar = jnp.broadcast_to(are_ref[:, sl], (nseq, LANES))
            ai = jnp.broadcast_to(aim_ref[:, sl], (nseq, LANES))
            sr = h0r_ref[:, sl]
            si = h0i_ref[:, sl]
            for t in range(mode):
                idx = pl.ds(t, nseq, stride=mode)
                sr, si = _cmul_add(ar, ai, sr, si, hr_s[q, idx, :], hi_s[q, idx, :])
                hr_s[q, idx, :] = sr
                hi_s[q, idx, :] = si
            hfr_ref[:, sl] = sr
            hfi_ref[:, sl] = si

    def block_rows(ref, j):
        return jnp.concatenate([ref[j * tpb + k] for k in range(tpb)], axis=1).astype(BF16)

    ys = []
    for j in range(nblk):
        ys.append(_dot(block_rows(hr_s, j), cre_ref[j]) - _dot(block_rows(hi_s, j), cim_ref[j]))
    y = jnp.concatenate(ys, axis=1)
    if mode == "carry":
        y = _exact_rows(permt_ref[...], y)
    y = y + d_ref[...] * uf
    ab = _dot(_gelu(y).astype(BF16), wglu_ref[...])
    xo_ref[...] = x + ab[:, :d] * _sigmoid(ab[:, d:])


def _s5_layer(x2, h0r, h0i, seq_len, n_seq, gain, prm):
    rows, d = x2.shape
    nstate = h0r.shape[1]
    if seq_len >= LAYER_ROWS:
        mode, tm = "carry", LAYER_ROWS
        nt = seq_len // tm
        grid = (n_seq, nt)
        xmap = lambda b, i: (b * nt + i, 0)
        h0_spec = pl.BlockSpec((1, 1, nstate), lambda b, i: (b, 0, 0))
        hf_spec = pl.BlockSpec((1, 1, nstate), lambda b, i: (b, 0, 0))
        h0r, h0i = h0r[:, None, :], h0i[:, None, :]
        hf_shape = (n_seq, 1, nstate)
        sem = ("arbitrary", "arbitrary")
        cst = lambda shape: pl.BlockSpec(shape, lambda b, i: (0,) * len(shape), pipeline_mode=pl.Buffered(1))
    else:
        mode = seq_len
        tm = min(LAYER_ROWS, rows)
        nseq_t = tm // seq_len
        grid = (rows // tm,)
        xmap = lambda i: (i, 0)
        h0_spec = pl.BlockSpec((nseq_t, nstate), lambda i: (i, 0))
        hf_spec = pl.BlockSpec((nseq_t, nstate), lambda i: (i, 0))
        hf_shape = (n_seq, nstate)
        sem = ("arbitrary",)
        cst = lambda shape: pl.BlockSpec(shape, lambda i: (0,) * len(shape), pipeline_mode=pl.Buffered(1))
    pw_r, pw_i = prm["pw_r"], prm["pw_i"]
    src = np.arange(tm).reshape(tm // SUBLANES, SUBLANES)
    perm = np.zeros((tm, tm), np.float32)
    perm[np.arange(tm), ((src % SUBLANES) * (tm // SUBLANES) + src // SUBLANES).reshape(-1)] = 1.0
    consts = [gain, prm["b_r"], prm["b_i"], prm["a_r"], prm["a_i"], pw_r, pw_i, prm["c_r"], prm["c_i"],
              prm["d"], prm["w_glu"], jnp.asarray(perm, BF16), jnp.asarray(perm.T, BF16)]
    outs = pl.pallas_call(
        functools.partial(_s5_kernel, mode, tm),
        grid=grid,
        in_specs=[pl.BlockSpec((tm, d), xmap), h0_spec, h0_spec] + [cst(c.shape) for c in consts],
        out_specs=[pl.BlockSpec((tm, d), xmap), hf_spec, hf_spec],
        out_shape=[jax.ShapeDtypeStruct((rows, d), F32), jax.ShapeDtypeStruct(hf_shape, F32),
                   jax.ShapeDtypeStruct(hf_shape, F32)],
        scratch_shapes=[pltpu.VMEM((nstate // LANES, tm, LANES), F32),
                        pltpu.VMEM((nstate // LANES, tm, LANES), F32),
                        pltpu.VMEM((SUBLANES, nstate), F32), pltpu.VMEM((SUBLANES, nstate), F32)],
        compiler_params=_cparams(sem),
        name="s5_layer",
    )(x2, h0r, h0i, *consts)
    xo, hfr, hfi = outs
    return xo, hfr.reshape(n_seq, nstate), hfi.reshape(n_seq, nstate)


def _s5_params(a_re, a_im, log_dt, b_re, b_im, c_re, c_im, d_skip, w_glu, n_pow):
    g, p = a_re.shape
    ch = b_re.shape[2]
    lr, li = a_re.astype(F32), a_im.astype(F32)
    dt = jnp.exp(log_dt.astype(F32))[:, None]
    mag = jnp.exp(lr * dt)
    ar, ai = mag * jnp.cos(li * dt), mag * jnp.sin(li * dt)
    den = lr * lr + li * li
    kr = ((ar - 1.0) * lr + ai * li) / den
    ki = (ai * lr - (ar - 1.0) * li) / den
    br, bi = b_re.astype(F32), b_im.astype(F32)
    bbar_r = kr[:, :, None] * br - ki[:, :, None] * bi
    bbar_i = kr[:, :, None] * bi + ki[:, :, None] * br
    nblk = g // SSM_GROUP_BLOCK
    eye = jnp.eye(SSM_GROUP_BLOCK, dtype=F32)

    def b_blocks(m):
        m = m.reshape(nblk, SSM_GROUP_BLOCK, p, ch)
        return jnp.einsum("jgpc,gh->jgchp", m, eye).reshape(nblk, SSM_GROUP_BLOCK * ch, SSM_GROUP_BLOCK * p).astype(BF16)

    def c_blocks(m):
        m = m.reshape(nblk, SSM_GROUP_BLOCK, ch, p)
        return jnp.einsum("jgcp,gh->jgphc", m, eye).reshape(nblk, SSM_GROUP_BLOCK * p, SSM_GROUP_BLOCK * ch).astype(BF16)

    ar, ai = ar.reshape(1, g * p), ai.reshape(1, g * p)
    pw_r, pw_i = [ar], [ai]
    for _ in range(n_pow - 1):
        pw_r, pw_i = pw_r + [pw_r[-1] * ar - pw_i[-1] * ai], pw_i + [pw_r[-1] * ai + pw_i[-1] * ar]
    return dict(b_r=b_blocks(bbar_r), b_i=b_blocks(bbar_i), a_r=ar, a_i=ai,
                pw_r=jnp.concatenate(pw_r, axis=0), pw_i=jnp.concatenate(pw_i, axis=0),
                c_r=c_blocks(c_re.astype(F32)), c_i=c_blocks(c_im.astype(F32)),
                d=d_skip.astype(F32).reshape(1, -1), w_glu=w_glu.astype(BF16))


def _ffn_ple_kernel(mode, final_norm, x_ref, p_ref, b1_ref, b2_ref, gffn_ref, wup_ref, cw_ref, cb_ref,
                    wdown_ref, gple_ref, wproj_ref, wgate_ref, gfin_ref, xo_ref, tail_ref, carry_ref):
    tm = x_ref.shape[0]
    f = cw_ref.shape[1]
    x = x_ref[...]
    u = _rms(x, gffn_ref[...]).astype(BF16)
    rowi = lax.broadcasted_iota(jnp.int32, (tm, 1), 0)
    if mode == "carry":
        @pl.when(pl.program_id(1) == 0)
        def _():
            carry_ref[SUBLANES - 2:SUBLANES, :] = b1_ref[0]

    fc = f // FFN_CHUNKS
    x1 = x
    for c in range(FFN_CHUNKS):
        cs = slice(c * fc, (c + 1) * fc)
        h = _dot(u, wup_ref[:, cs])
        gate_branch = _dot(u, wup_ref[:, f + c * fc:f + (c + 1) * fc])
        r1 = pltpu.roll(h, 1, 0)
        r2 = pltpu.roll(h, 2, 0)
        if mode == "carry":
            c0 = carry_ref[SUBLANES - 2:SUBLANES - 1, cs]
            c1 = carry_ref[SUBLANES - 1:SUBLANES, cs]
            hm1 = jnp.where(rowi == 0, c1, r1)
            hm2 = jnp.where(rowi == 0, c0, jnp.where(rowi == 1, c1, r2))
            carry_ref[:, cs] = h[tm - SUBLANES:tm, :]
            tail_ref[:, cs] = h[tm - SUBLANES:tm, :]
        else:
            t = rowi % mode
            hm1 = jnp.where(t == 0, b1_ref[:, cs], r1)
            hm2 = jnp.where(t < 2, b2_ref[:, cs], r2)
            tail_ref[:, cs] = h
        conv = cb_ref[:, cs] + cw_ref[0:1, cs] * hm2 + cw_ref[1:2, cs] * hm1 + cw_ref[2:3, cs] * h
        act = (_gelu(conv) * gate_branch).astype(BF16)
        x1 = x1 + _dot(act, wdown_ref[cs, :])
    gate = _sigmoid(_dot(_rms(x1, gple_ref[...]).astype(BF16), wgate_ref[...]))
    x2 = x1 + _dot(p_ref[...].astype(BF16), wproj_ref[...]) * gate
    xo_ref[...] = _rms(x2, gfin_ref[...]) if final_norm else x2


def _ffn_ple(x2, p2, buf, seq_len, n_seq, final_norm, gffn, wup, cw, cb, wdown, gple, wproj, wgate, gfin):
    rows, d = x2.shape
    f = cw.shape[1]
    ple = p2.shape[1]
    consts = [gffn, wup, cw, cb, wdown, gple, wproj, wgate, gfin]
    if seq_len >= LAYER_ROWS:
        mode, tm = "carry", LAYER_ROWS
        nt = seq_len // tm
        grid = (n_seq, nt)
        rmap = lambda b, i: (b * nt + i, 0)
        b1, b2 = buf, buf
        bspec = pl.BlockSpec((1, CONV_W - 1, f), lambda b, i: (b, 0, 0))
        tail_rows = SUBLANES
        sem = ("arbitrary", "arbitrary")
        cst = lambda shape: pl.BlockSpec(shape, lambda b, i: (0,) * len(shape), pipeline_mode=pl.Buffered(1))
    else:
        mode = seq_len
        tm = min(LAYER_ROWS, rows)
        grid = (rows // tm,)
        rmap = lambda i: (i, 0)
        b1 = jnp.pad(buf[:, 1:], ((0, 0), (0, seq_len - 1), (0, 0))).reshape(rows, f)
        b2 = jnp.pad(buf, ((0, 0), (0, seq_len - (CONV_W - 1)), (0, 0))).reshape(rows, f)
        bspec = pl.BlockSpec((tm, f), rmap)
        tail_rows = tm
        sem = ("arbitrary",)
        cst = lambda shape: pl.BlockSpec(shape, lambda i: (0,) * len(shape), pipeline_mode=pl.Buffered(1))
    n_tiles = rows // tm
    xo, tail = pl.pallas_call(
        functools.partial(_ffn_ple_kernel, mode, final_norm),
        grid=grid,
        in_specs=[pl.BlockSpec((tm, d), rmap), pl.BlockSpec((tm, ple), rmap), bspec, bspec]
                 + [cst(c.shape) for c in consts],
        out_specs=[pl.BlockSpec((tm, d), rmap), pl.BlockSpec((tail_rows, f), rmap)],
        out_shape=[jax.ShapeDtypeStruct((rows, d), F32), jax.ShapeDtypeStruct((n_tiles * tail_rows, f), F32)],
        scratch_shapes=[pltpu.VMEM((SUBLANES, f), F32)],
        compiler_params=_cparams(sem),
        name="ffn_ple",
    )(x2, p2, b1, b2, *consts)
    if mode == "carry":
        new_buf = tail.reshape(n_seq, n_tiles // n_seq, SUBLANES, f)[:, -1, SUBLANES - (CONV_W - 1):, :]
    else:
        new_buf = tail.reshape(n_seq, seq_len, f)[:, seq_len - (CONV_W - 1):, :]
    return xo, new_buf


def _block_diag_heads(w):
    eye = jnp.eye(N_KV_HEADS, dtype=w.dtype)
    out = jnp.einsum("...de,gh->...gdhe", w, eye)
    return out.reshape(*w.shape[:-2], KV_COLS, KV_COLS)


def _slopes():
    return np.exp2(-8.0 * np.arange(1, N_HEADS + 1, dtype=np.float64) / N_HEADS).astype(np.float32)


def _query_aug(slopes):
    rest = jnp.asarray(slopes, F32)
    aug = jnp.zeros((N_HEADS, LANES), F32)
    for i in range(N_PIECES):
        piece = rest.astype(BF16).astype(F32)
        rest = rest - piece
        aug = aug.at[:, AUG_LANE + i].set(piece).at[:, AUG_LANE + N_PIECES + i].set(piece)
    return aug


def _expand_matrix(n_keys):
    return (np.arange(LANES)[:, None] == (np.arange(n_keys)[None, :] // L_SEL)).astype(np.float32)


def _pool_matrix(n_cmp):
    return (np.arange(n_cmp)[:, None] // SEL_PER_CMP == np.arange(LANES)[None, :]).astype(np.float32)


def _nsa_layer(xp, xs, cache_l, cache_win_l, page_table, gain, w_in, w_out, pe, w1, w2):
    bsz, t, d = xp.shape
    bs, tq, _ = xs.shape
    page = cache_l.shape[1]
    n_pages = page_table.shape[1]
    past_len = n_pages * page
    n_pool = cache_l.shape[0]
    assert t % SEL_CHUNK == 0 and t // L_SEL <= LANES and t >= WINDOW + Q_TILE
    assert (past_len + LANES) // L_SEL <= LANES and page % D_CMP == 0 and tq <= SUBLANES

    qcols = N_HEADS * HEAD_DIM
    nmain = qcols + 6 * KV_COLS
    w_main = w_in[:, :nmain].astype(BF16)
    w_gate = jnp.pad(w_in[:, nmain:], ((0, 0), (0, LANES - (w_in.shape[1] - nmain)))).astype(BF16)
    w_out_b = w_out.astype(BF16)
    pe_t = jnp.tile(pe, (1, 1, N_KV_HEADS))
    w1_bd = _block_diag_heads(w1).astype(BF16)
    w2_bd = _block_diag_heads(w2).astype(BF16)
    slopes = _slopes()

    q, kv, gates, ksel, vsel, kwin, vwin, kv_t, win_t = _attn_in_proj(
        xp.reshape(bsz * t, d), gain, w_main, w_gate, seq_len=t)
    sh = lambda a: a.reshape(bsz, t, a.shape[-1])
    cmp_k, cmp_v = _segment_sums(kv.reshape(bsz, t // D_CMP, D_CMP * 4 * KV_COLS), pe_t, w1_bd, w2_bd)
    onehot = jnp.asarray(_expand_matrix(t).T, BF16)
    pool = jnp.asarray(_pool_matrix(t // D_CMP), BF16)
    o = _nsa_prompt_attend(sh(q), sh(gates), cmp_k, cmp_v, sh(ksel), sh(vsel), sh(kwin), sh(vwin), onehot, pool,
                           jnp.asarray(_query_aug(slopes)))
    xp_new = _matmul_res(o.reshape(bsz * t, d), w_out_b, xp.reshape(bsz * t, d)).reshape(bsz, t, d)
    keep = min(WINDOW, t)
    from_t = lambda a, n: jnp.transpose(a.reshape(a.shape[0], n, N_KV_HEADS, HEAD_DIM, a.shape[-1]), (0, 4, 1, 2, 3))
    kv_p = from_t(kv_t, 4)
    win_p = from_t(win_t[:, :, t - keep:], 2)

    cache_t = jnp.transpose(cache_l, (0, 2, 3, 4, 1)).reshape(n_pool, 4 * KV_COLS, page)
    wbuf = cache_win_l.shape[1]
    win_past_t = jnp.transpose(cache_win_l, (0, 2, 3, 4, 1)).reshape(bs, 2 * KV_COLS, wbuf)
    qs, kvs, gates_s, wins = _attn_in_proj(xs.reshape(bs * tq, d), gain, w_main, w_gate)
    shs = lambda a: a.reshape(bs, tq, a.shape[-1])
    a_all, b_all = _pool_segment_sums(cache_t, pe_t, w1_bd, math.gcd(n_pool, POOL_PAGES))
    nseg_s = past_len // D_CMP
    pool_s = jnp.asarray(_pool_matrix(nseg_s), BF16)
    slope_s = jnp.asarray(np.repeat(slopes, tq)[:, None])
    o_cmp, sel = _nsa_sample_cmp(page_table, shs(qs), a_all, b_all, w2_bd, pool_s, slope_s, page, past_len)
    expand_s = jnp.asarray(_expand_matrix(past_len + LANES), BF16)
    o_s, new_win_t = _nsa_sample_attend(page_table, cache_t, shs(qs), shs(gates_s), shs(kvs), shs(wins), win_past_t,
                                        o_cmp, sel, expand_s, slope_s)
    xs_new = _matmul_res(o_s.reshape(bs * tq, d), w_out_b, xs.reshape(bs * tq, d)).reshape(bs, tq, d)
    kv_s = shs(kvs).reshape(bs, tq, 4, N_KV_HEADS, HEAD_DIM)
    win_s = from_t(new_win_t, 2)
    return xp_new, xs_new, kv_p, kv_s, win_p, win_s


def kernel(x_prompt, x_sample, cache_kv, cache_win, state_ssm_re, state_ssm_im, state_conv, page_table,
           p_prompt, p_sample, norm_mix, norm_ffn, norm_ple, norm_final, w_attn_in, w_attn_out,
           cmp_pe, cmp_w1, cmp_w2, ssm_a_re, ssm_a_im, ssm_log_dt, ssm_b_re, ssm_b_im, ssm_c_re, ssm_c_im,
           ssm_d, w_glu, w_ffn_up, ffn_conv_w, ffn_conv_b, w_ffn_down, w_ple_proj, w_ple_gate):
    bsz, t, d = x_prompt.shape
    bs, tq, _ = x_sample.shape
    depth = norm_mix.shape[0]
    f = ffn_conv_w.shape[2]
    g, p = ssm_a_re.shape[1:]
    xp, xs = x_prompt, x_sample
    row = lambda v: v.reshape(1, -1).astype(F32)
    kv_p, kv_s, win_p, win_s = [], [], [], []
    sre_p, sim_p, sre_s, sim_s = [], [], [], []
    cb_p, cb_s = [], []
    for i in range(depth):
        j = i // 2
        if i % 2 == 0:
            xp, xs, kvp, kvs, wp, ws = _nsa_layer(xp, xs, cache_kv[j], cache_win[j], page_table, row(norm_mix[i]),
                                                  w_attn_in[j], w_attn_out[j], cmp_pe[j], cmp_w1[j], cmp_w2[j])
            kv_p.append(kvp)
            kv_s.append(kvs)
            win_p.append(wp)
            win_s.append(ws)
        else:
            prm = _s5_params(ssm_a_re[j], ssm_a_im[j], ssm_log_dt[j], ssm_b_re[j], ssm_b_im[j], ssm_c_re[j],
                             ssm_c_im[j], ssm_d[j], w_glu[j], LAYER_ROWS // SUBLANES)
            zero = jnp.zeros((bsz, g * p), F32)
            xp2, hr, hi = _s5_layer(xp.reshape(bsz * t, d), zero, zero, t, bsz, row(norm_mix[i]), prm)
            xs2, hrs, his = _s5_layer(xs.reshape(bs * tq, d), state_ssm_re[j].reshape(bs, g * p).astype(F32),
                                      state_ssm_im[j].reshape(bs, g * p).astype(F32), tq, bs, row(norm_mix[i]), prm)
            xp, xs = xp2.reshape(bsz, t, d), xs2.reshape(bs, tq, d)
            sre_p.append(hr.reshape(bsz, g, p))
            sim_p.append(hi.reshape(bsz, g, p))
            sre_s.append(hrs.reshape(bs, g, p))
            sim_s.append(his.reshape(bs, g, p))
        last = i == depth - 1
        ffn_w = (row(norm_ffn[i]), w_ffn_up[i].astype(BF16), ffn_conv_w[i].astype(F32), row(ffn_conv_b[i]),
                 w_ffn_down[i].astype(BF16), row(norm_ple[i]), w_ple_proj[i].astype(BF16),
                 w_ple_gate[i].astype(BF16), row(norm_final))
        xp2, bp = _ffn_ple(xp.reshape(bsz * t, d), p_prompt[i].reshape(bsz * t, -1),
                           jnp.zeros((bsz, CONV_W - 1, f), F32), t, bsz, last, *ffn_w)
        xs2, bs_new = _ffn_ple(xs.reshape(bs * tq, d), p_sample[i].reshape(bs * tq, -1), state_conv[i].astype(F32),
                               tq, bs, last, *ffn_w)
        xp, xs = xp2.reshape(bsz, t, d), xs2.reshape(bs, tq, d)
        cb_p.append(bp)
        cb_s.append(bs_new)
    return (xp, xs, jnp.stack(kv_p), jnp.stack(kv_s), jnp.stack(win_p), jnp.stack(win_s),
            jnp.stack(sre_p), jnp.stack(sim_p), jnp.stack(sre_s), jnp.stack(sim_s),
            jnp.stack(cb_p), jnp.stack(cb_s))
```

```python
import functools
import math

import numpy as np
import jax
import jax.numpy as jnp
from jax import lax
from jax.experimental import pallas as pl
from jax.experimental.pallas import tpu as pltpu

F32 = jnp.float32
BF16 = jnp.bfloat16

N_HEADS = 16
HEAD_DIM = 64
N_KV_HEADS = 4
GROUP = N_HEADS // N_KV_HEADS
KV_COLS = N_KV_HEADS * HEAD_DIM
L_CMP = 32
D_CMP = 16
L_SEL = 64
TOP_N = 16
WINDOW = 512
SSM_CH = 16
STATE_P = 64
CONV_W = 3
NORM_EPS = 1e-6
NEG_INF = -1e30
FORCE_BONUS = 1e4
SEL_PER_CMP = L_SEL // D_CMP

LANES = 128
SUBLANES = 8
VMEM_LIMIT = 56 * 1024 * 1024

PROJ_ROWS = 512
LAYER_ROWS = 256
POOL_PAGES = 32
Q_TILE = 128
SEL_CHUNK = 512
SEL_UNROLL = 4
SSM_LANE_CHUNK = 512
SSM_GROUP_BLOCK = 16
FFN_CHUNKS = 2


def _cparams(sem):
    return pltpu.CompilerParams(dimension_semantics=sem, vmem_limit_bytes=VMEM_LIMIT)


def _const_spec(shape):
    nd = len(shape)
    return pl.BlockSpec(shape, lambda *_: (0,) * nd, pipeline_mode=pl.Buffered(1))


def _rms(x, gain):
    y = x * lax.rsqrt(jnp.mean(x * x, axis=-1, keepdims=True) + NORM_EPS)
    return y * gain


def _gelu(x):
    c = math.sqrt(2.0 / math.pi)
    return x * (0.5 * (1.0 + jnp.tanh(c * (x + 0.044715 * (x * x * x)))))


def _sigmoid(x):
    return 1.0 / (1.0 + jnp.exp(-x))


def _dot(a, b):
    return jnp.dot(a, b, preferred_element_type=F32)


def _dot_nt(a, b):
    return lax.dot_general(a, b, (((1,), (1,)), ((), ())), preferred_element_type=F32)


def _masked_softmax(s, mask):
    sm = jnp.where(mask, s, NEG_INF)
    m = jnp.max(sm, axis=-1, keepdims=True)
    e = jnp.where(mask, jnp.exp(sm - m), 0.0)
    den = jnp.sum(e, axis=-1, keepdims=True)
    return e / jnp.where(den > 0.0, den, 1.0)


def _exact_pool(x, pool_bf):
    hi = x.astype(BF16)
    r1 = x - hi.astype(F32)
    mid = r1.astype(BF16)
    lo = (r1 - mid.astype(F32)).astype(BF16)
    return _dot(hi, pool_bf) + _dot(mid, pool_bf) + _dot(lo, pool_bf)


def _exact_rows(sel_bf, x):
    hi = x.astype(BF16)
    r1 = x - hi.astype(F32)
    mid = r1.astype(BF16)
    lo = (r1 - mid.astype(F32)).astype(BF16)
    return _dot(sel_bf, hi) + _dot(sel_bf, mid) + _dot(sel_bf, lo)


def _topk_mask(score, k):
    s = score.T
    cand = lax.broadcasted_iota(jnp.int32, s.shape, 0).astype(F32)
    sel = jnp.zeros(s.shape, F32)
    for _ in range(k):
        m = jnp.max(s, axis=0, keepdims=True)
        first = jnp.min(jnp.where(s == m, cand, float(LANES)), axis=0, keepdims=True)
        hit = cand == first
        sel = jnp.where(hit, 1.0, sel)
        s = jnp.where(hit, -jnp.inf, s)
    return sel.T


def _slot_ids():
    return lax.broadcasted_iota(jnp.int32, (1, KV_COLS), 1) // HEAD_DIM


def _place_slot(x, src_slot, dst_slot):
    shift = (HEAD_DIM * (dst_slot - src_slot)) % KV_COLS
    y = pltpu.roll(x, shift, 1) if shift else x
    return jnp.where(_slot_ids() == dst_slot, y, 0.0)


def _selection_scores(imp, tpos):
    blk = lax.broadcasted_iota(jnp.int32, imp.shape, 1)
    cur = tpos // L_SEL
    forced = (blk == 0) | (blk == cur) | (blk == cur - 1)
    started = blk * L_SEL <= tpos
    return jnp.where(started, imp + jnp.where(forced, FORCE_BONUS, 0.0), NEG_INF)


def _attn_in_kernel(seq_len, x_ref, gain_ref, wm_ref, wg_ref, q_ref, gates_ref, *refs):
    tm, d = x_ref.shape
    u = _rms(x_ref[...], gain_ref[...]).astype(BF16)
    z = _dot(u, wm_ref[...])
    q_ref[...] = z[:, :d]
    kv = z[:, d:d + 4 * KV_COLS]
    win = z[:, d + 4 * KV_COLS:d + 6 * KV_COLS]
    gates_ref[...] = _sigmoid(_dot(u, wg_ref[...]))
    if seq_len is None:
        kv_ref, win_ref = refs
        kv_ref[...] = kv
        win_ref[...] = win
        return
    ksel_ref, vsel_ref, kwin_ref, vwin_ref, kvt_ref, wint_ref, pages_ref = refs
    pos = (pl.program_id(0) % (seq_len // tm)) * tm + lax.broadcasted_iota(jnp.int32, (tm, 1), 0)
    aug = _position_aug(((pos // LANES) * LANES).astype(F32), (pos % LANES).astype(F32))
    lane = lax.broadcasted_iota(jnp.int32, (1, LANES), 1)
    for g in range(N_KV_HEADS):
        sl = slice(g * LANES, (g + 1) * LANES)
        ksel_ref[:, sl] = jnp.where(lane < AUG_LANE, _head_slot(kv[:, 2 * KV_COLS:3 * KV_COLS], g), aug).astype(BF16)
        kwin_ref[:, sl] = jnp.where(lane < AUG_LANE, _head_slot(win[:, :KV_COLS], g), aug).astype(BF16)
    vsel_ref[...] = kv[:, 3 * KV_COLS:4 * KV_COLS].astype(BF16)
    vwin_ref[...] = win[:, KV_COLS:].astype(BF16)
    kv_t = kv.T
    kvt_ref[0] = kv_t
    wint_ref[0] = win.T
    for p in range(tm // LANES):
        pages_ref[p] = kv_t[:2 * KV_COLS, p * LANES:(p + 1) * LANES]


def _attn_in_proj(x2, gain, w_main, w_gate, seq_len=None):
    rows, d = x2.shape
    tm = min(PROJ_ROWS, rows)
    nmain = w_main.shape[1]
    row = lambda n: pl.BlockSpec((tm, n), lambda i: (i, 0))
    outs = [(d, F32), (LANES, F32)]
    if seq_len is None:
        outs += [(4 * KV_COLS, F32), (2 * KV_COLS, F32)]
    else:
        outs += [(N_KV_HEADS * LANES, BF16), (KV_COLS, BF16), (N_KV_HEADS * LANES, BF16), (KV_COLS, BF16)]
    out_specs = [row(n) for n, _ in outs]
    out_shape = [jax.ShapeDtypeStruct((rows, n), dt) for n, dt in outs]
    if seq_len is not None:
        nt = seq_len // tm
        for n in (4 * KV_COLS, 2 * KV_COLS):
            out_specs.append(pl.BlockSpec((1, n, tm), lambda i: (i // nt, 0, i % nt)))
            out_shape.append(jax.ShapeDtypeStruct((rows // seq_len, n, seq_len), F32))
        out_specs.append(pl.BlockSpec((tm // LANES, 2 * KV_COLS, LANES), lambda i: (i, 0, 0)))
        out_shape.append(jax.ShapeDtypeStruct((rows // LANES, 2 * KV_COLS, LANES), F32))
    return pl.pallas_call(
        functools.partial(_attn_in_kernel, seq_len),
        grid=(rows // tm,),
        in_specs=[row(d), _const_spec((1, d)), _const_spec((d, nmain)), _const_spec((d, LANES))],
        out_specs=out_specs,
        out_shape=out_shape,
        compiler_params=_cparams(("arbitrary",)),
        name="attn_in_proj",
    )(x2, gain, w_main, w_gate)


def _matmul_res_kernel(a_ref, w_ref, x_ref, o_ref):
    o_ref[...] = x_ref[...] + _dot(a_ref[...].astype(BF16), w_ref[...])


def _matmul_res(a, w, x):
    rows, k = a.shape
    n = w.shape[1]
    tm = min(PROJ_ROWS, rows)
    return pl.pallas_call(
        _matmul_res_kernel,
        grid=(rows // tm,),
        in_specs=[pl.BlockSpec((tm, k), lambda i: (i, 0)), _const_spec((k, n)),
                  pl.BlockSpec((tm, n), lambda i: (i, 0))],
        out_specs=pl.BlockSpec((tm, n), lambda i: (i, 0)),
        out_shape=jax.ShapeDtypeStruct((rows, n), F32),
        compiler_params=_cparams(("arbitrary",)),
        name="matmul_res",
    )(a, w, x)


def _finish_compress(a, b, w2):
    nseg = a.shape[0]
    pre = a + pltpu.roll(b, nseg - 1, 0)
    out = _dot(_gelu(pre).astype(BF16), w2)
    rowi = lax.broadcasted_iota(jnp.int32, (nseg, 1), 0)
    return jnp.where(rowi < nseg - 1, out, 0.0)


def _prompt_summaries_kernel(a_ref, b_ref, w2_ref, ck_ref, cv_ref):
    nseg = a_ref.shape[1]
    a = a_ref[0]
    b = b_ref[0]
    kc = _finish_compress(a[:, :KV_COLS], b[:, :KV_COLS], w2_ref[0])
    n_i = lax.broadcasted_iota(jnp.int32, (nseg, 1), 0)
    hi = ((n_i * D_CMP) // LANES * LANES).astype(F32)
    lo = ((n_i * D_CMP) % LANES).astype(F32) + (L_CMP - 1) / 2
    aug = _position_aug(hi, lo)
    lane = lax.broadcasted_iota(jnp.int32, (1, LANES), 1)
    for g in range(N_KV_HEADS):
        ck_ref[0, :, g * LANES:(g + 1) * LANES] = jnp.where(lane < AUG_LANE, _head_slot(kc, g), aug).astype(BF16)
    cv_ref[0] = _finish_compress(a[:, KV_COLS:], b[:, KV_COLS:], w2_ref[1]).astype(BF16)


def _prompt_summaries(a3, b3, w2_bd):
    bsz, nseg, _ = a3.shape
    seq = lambda n: pl.BlockSpec((1, nseg, n), lambda i: (i, 0, 0))
    return pl.pallas_call(
        _prompt_summaries_kernel,
        grid=(bsz,),
        in_specs=[seq(2 * KV_COLS), seq(2 * KV_COLS), _const_spec(w2_bd.shape)],
        out_specs=[seq(N_KV_HEADS * LANES), seq(KV_COLS)],
        out_shape=[jax.ShapeDtypeStruct((bsz, nseg, N_KV_HEADS * LANES), BF16),
                   jax.ShapeDtypeStruct((bsz, nseg, KV_COLS), BF16)],
        compiler_params=_cparams(("arbitrary",)),
        name="prompt_summaries",
    )(a3, b3, w2_bd)


def _pool_sums_kernel(pp, page, x_ref, pe_ref, w1_ref, a_ref, b_ref, rows_s):
    ftiles = 2 * KV_COLS // LANES

    def to_rows(p, carry):
        xt = x_ref[p]
        r0 = pl.multiple_of(p * page, page)
        for c in range(ftiles):
            rows_s[c, pl.ds(r0, page), :] = xt[c * LANES:(c + 1) * LANES, :].T
        return carry

    lax.fori_loop(0, pp, to_rows, 0, unroll=4)
    nseg = pp * page // D_CMP
    tpk = KV_COLS // LANES
    for kind in range(2):
        a = jnp.zeros((nseg, KV_COLS), F32)
        b = jnp.zeros((nseg, KV_COLS), F32)
        for l in range(D_CMP):
            xs = jnp.concatenate([rows_s[kind * tpk + c, pl.ds(l, nseg, stride=D_CMP), :] for c in range(tpk)],
                                 axis=1)
            a = a + _dot((xs + pe_ref[kind, l:l + 1, :]).astype(BF16), w1_ref[kind, l])
            b = b + _dot((xs + pe_ref[kind, D_CMP + l:D_CMP + l + 1, :]).astype(BF16), w1_ref[kind, D_CMP + l])
        a_ref[:, kind * KV_COLS:(kind + 1) * KV_COLS] = a
        b_ref[:, kind * KV_COLS:(kind + 1) * KV_COLS] = b


def _pool_segment_sums(cache_t, pe_t, w1_bd, pp):
    n_pool, _, page = cache_t.shape
    assert page == LANES and n_pool % pp == 0
    nseg = pp * page // D_CMP
    return pl.pallas_call(
        functools.partial(_pool_sums_kernel, pp, page),
        grid=(n_pool // pp,),
        in_specs=[pl.BlockSpec((pp, 2 * KV_COLS, page), lambda i: (i, 0, 0)),
                  _const_spec(pe_t.shape), _const_spec(w1_bd.shape)],
        out_specs=[pl.BlockSpec((nseg, 2 * KV_COLS), lambda i: (i, 0))] * 2,
        out_shape=[jax.ShapeDtypeStruct((n_pool * page // D_CMP, 2 * KV_COLS), F32)] * 2,
        scratch_shapes=[pltpu.VMEM((2 * KV_COLS // LANES, pp * page, LANES), F32)],
        compiler_params=_cparams(("arbitrary",)),
        name="pool_segment_sums",
    )(cache_t, pe_t, w1_bd)


AUG_LANE = HEAD_DIM
N_PIECES = 3
SOFTMAX_ROWS = 32


def _position_aug(hi, lo):
    lane = lax.broadcasted_iota(jnp.int32, (1, LANES), 1)
    in_hi = (lane >= AUG_LANE) & (lane < AUG_LANE + N_PIECES)
    in_lo = (lane >= AUG_LANE + N_PIECES) & (lane < AUG_LANE + 2 * N_PIECES)
    return jnp.where(in_hi, hi, jnp.where(in_lo, lo, 0.0))


def _head_slot(x, g):
    tile = x[:, LANES * (g // 2):LANES * (g // 2 + 1)]
    return pltpu.roll(tile, HEAD_DIM, 1) if g % 2 else tile


def _for_row_blocks(n_rows, body):
    for r0 in range(0, n_rows, SOFTMAX_ROWS):
        body(r0)


def _nsa_prompt_kernel(nc, q_ref, gates_ref, ck_ref, cv_ref, ksel_ref, vsel_ref, kwin_ref, vwin_ref,
                       onehot_ref, pool_ref, qaug_ref, o_ref,
                       qa_s, s_s, s2_s, p_s, p2_s, m_s, l_s, al_s, acc_s, ocmp_s, imp_s, score_s, selb_s):
    tq = Q_TILE
    rows = GROUP * tq
    rb = SOFTMAX_ROWS
    ncp = ck_ref.shape[1]
    qb = pl.program_id(1)
    q0 = qb * tq
    gates = gates_ref[0]
    lane = lax.broadcasted_iota(jnp.int32, (1, LANES), 1)
    tpos1 = q0 + lax.broadcasted_iota(jnp.int32, (tq, 1), 0)
    row_pos = lambda r0: q0 + r0 % tq + lax.broadcasted_iota(jnp.int32, (rb, 1), 0)
    win_len = WINDOW + tq
    w0 = pl.multiple_of(jnp.maximum(q0 - WINDOW, 0), tq)
    n_past_chunks = q0 // SEL_CHUNK
    pair = lambda g: slice(LANES * (g // 2), LANES * (g // 2 + 1))
    slot = lambda g: slice(LANES * g, LANES * (g + 1))

    def softmax_once(r0, width, valid, sbuf, pbuf):
        sb = jnp.where(valid, sbuf[pl.ds(r0, rb), :width], NEG_INF)
        m = jnp.max(sb, axis=-1, keepdims=True)
        e = jnp.exp(sb - m)
        den = jnp.sum(e, axis=-1, keepdims=True)
        p = e * jnp.where(m > 0.5 * NEG_INF, 1.0 / den, 0.0)
        pbuf[pl.ds(r0, rb), :width] = p.astype(BF16)
        return p

    n_i = lax.broadcasted_iota(jnp.int32, (1, ncp), 1)
    c_end = jnp.where(n_i < nc, n_i * D_CMP + (L_CMP - 1), jnp.int32(2 ** 30))

    for g in range(N_KV_HEADS):
        for r in range(GROUP):
            h = g * GROUP + r
            qh = q_ref[0, :, LANES * (h // 2):LANES * (h // 2 + 1)] * (HEAD_DIM ** -0.5)
            if h % 2:
                qh = pltpu.roll(qh, HEAD_DIM, 1)
            qa_s[g, r * tq:(r + 1) * tq, :] = jnp.where(lane < AUG_LANE, qh, qaug_ref[h:h + 1, :]).astype(BF16)

        s_s[:, :ncp] = _dot_nt(qa_s[g], ck_ref[0, :, slot(g)])
        imp_s[...] = jnp.zeros(imp_s.shape, F32)

        def cmp_block(r0):
            p = softmax_once(r0, ncp, c_end <= row_pos(r0), s_s, p_s)
            imp_s[pl.ds(r0 % tq, rb), :] += p

        _for_row_blocks(rows, cmp_block)
        ocmp_s[g] = _dot(p_s[:, :ncp], cv_ref[0, :, pair(g)])
        score_s[g * tq:(g + 1) * tq, :] = _selection_scores(_exact_pool(imp_s[...], pool_ref[...]), tpos1)

    selb_s[...] = jnp.where(_topk_mask(score_s[...], TOP_N) > 0.5, 0.0, NEG_INF).astype(BF16)

    for g in range(N_KV_HEADS):
        sel_bias = selb_s[g * tq:(g + 1) * tq, :]
        q_full = jnp.concatenate([qa_s[g], jnp.concatenate([sel_bias] * GROUP, axis=0)], axis=1)

        m_s[...] = jnp.full(m_s.shape, NEG_INF, F32)
        l_s[...] = jnp.zeros(l_s.shape, F32)
        acc_s[...] = jnp.zeros(acc_s.shape, F32)

        def sel_scores(c, buf):
            k0 = pl.multiple_of(c * SEL_CHUNK, SEL_CHUNK)
            k_full = jnp.concatenate([ksel_ref[0, pl.ds(k0, SEL_CHUNK), slot(g)],
                                      onehot_ref[pl.ds(k0, SEL_CHUNK), :]], axis=1)
            buf[:, :SEL_CHUNK] = _dot_nt(q_full, k_full)

        def sel_consume(c, buf, causal):
            pbuf = p_s if buf is s_s else p2_s
            k0 = pl.multiple_of(c * SEL_CHUNK, SEL_CHUNK)
            kpos = k0 + lax.broadcasted_iota(jnp.int32, (1, SEL_CHUNK), 1)

            def block(r0):
                rs = pl.ds(r0, rb)
                sb = buf[rs, :SEL_CHUNK]
                if causal:
                    sb = jnp.where(kpos <= row_pos(r0), sb, NEG_INF)
                m_old = m_s[rs, :]
                m_new = jnp.maximum(m_old, jnp.max(sb, axis=-1, keepdims=True))
                alpha = jnp.exp(m_old - m_new)
                p = jnp.exp(sb - jnp.tile(m_new, (1, SEL_CHUNK // LANES)))
                l_s[rs, :] = alpha * l_s[rs, :] + jnp.sum(p, axis=-1, keepdims=True)
                m_s[rs, :] = m_new
                al_s[rs, :] = alpha
                pbuf[rs, :SEL_CHUNK] = p.astype(BF16)

            _for_row_blocks(rows, block)
            acc_s[...] = al_s[...] * acc_s[...] + _dot(pbuf[:, :SEL_CHUNK], vsel_ref[0, pl.ds(k0, SEL_CHUNK), pair(g)])

        def run_chunks(first, count, ends_causal):
            bufs = (s_s, s2_s)
            for k in range(count):
                closes = ends_causal and k == count - 1
                if not closes:
                    sel_scores(first + k + 1, bufs[(k + 1) % 2])
                sel_consume(first + k, bufs[k % 2], closes)

        sel_scores(0, s_s)

        def past_group(i, carry):
            run_chunks(i * SEL_UNROLL, SEL_UNROLL, False)
            return carry

        n_groups = n_past_chunks // SEL_UNROLL
        lax.fori_loop(0, n_groups, past_group, 0)
        for rem in range(SEL_UNROLL):
            @pl.when(n_past_chunks - n_groups * SEL_UNROLL == rem)
            def _():
                run_chunks(n_groups * SEL_UNROLL, rem + 1, True)

        o_sel = acc_s[...] / l_s[...]

        s_s[:, :win_len] = _dot_nt(qa_s[g], kwin_ref[0, pl.ds(w0, win_len), slot(g)])
        wpos = w0 + lax.broadcasted_iota(jnp.int32, (1, win_len), 1)

        def win_block(r0):
            wd = row_pos(r0) - wpos
            softmax_once(r0, win_len, (wd >= 0) & (wd < WINDOW), s_s, p_s)

        _for_row_blocks(rows, win_block)
        o_win = _dot(p_s[:, :win_len], vwin_ref[0, pl.ds(w0, win_len), pair(g)])
        o_cmp = ocmp_s[g]

        halves = []
        for r in range(GROUP):
            c0 = 3 * (g * GROUP + r)
            rs = slice(r * tq, (r + 1) * tq)
            o_r = (gates[:, c0:c0 + 1] * o_cmp[rs] + gates[:, c0 + 1:c0 + 2] * o_sel[rs]
                   + gates[:, c0 + 2:c0 + 3] * o_win[rs])
            halves.append(pltpu.roll(o_r, HEAD_DIM, 1) if (r - g) % 2 else o_r)
        for j in range(GROUP // 2):
            tile = jnp.where(lane < HEAD_DIM, halves[2 * j], halves[2 * j + 1])
            o_ref[0, :, LANES * (2 * g + j):LANES * (2 * g + j + 1)] = tile.astype(o_ref.dtype)


def _nsa_prompt_attend(q, gates, cmp_k, cmp_v, ksel, vsel, kwin, vwin, onehot, pool, qaug):
    bsz, t, d = q.shape
    nc = (t - L_CMP) // D_CMP + 1
    ncp = cmp_k.shape[1]
    rows = GROUP * Q_TILE
    width = max(ncp, SEL_CHUNK, WINDOW + Q_TILE)
    seq = lambda a: pl.BlockSpec((1,) + a.shape[1:], lambda b, i: (b, 0, 0), pipeline_mode=pl.Buffered(1))
    stat = pltpu.VMEM((rows, LANES), F32)
    return pl.pallas_call(
        functools.partial(_nsa_prompt_kernel, nc),
        grid=(bsz, t // Q_TILE),
        in_specs=[pl.BlockSpec((1, Q_TILE, d), lambda b, i: (b, i, 0)),
                  pl.BlockSpec((1, Q_TILE, LANES), lambda b, i: (b, i, 0)),
                  seq(cmp_k), seq(cmp_v), seq(ksel), seq(vsel), seq(kwin), seq(vwin),
                  _const_spec(onehot.shape), _const_spec(pool.shape), _const_spec(qaug.shape)],
        out_specs=pl.BlockSpec((1, Q_TILE, d), lambda b, i: (b, i, 0)),
        out_shape=jax.ShapeDtypeStruct((bsz, t, d), BF16),
        scratch_shapes=[pltpu.VMEM((N_KV_HEADS, rows, LANES), BF16),
                        pltpu.VMEM((rows, width), F32),
                        pltpu.VMEM((rows, SEL_CHUNK), F32),
                        pltpu.VMEM((rows, width), BF16),
                        pltpu.VMEM((rows, SEL_CHUNK), BF16),
                        stat, stat, stat, stat,
                        pltpu.VMEM((N_KV_HEADS, rows, LANES), F32),
                        pltpu.VMEM((Q_TILE, ncp), F32),
                        pltpu.VMEM((N_KV_HEADS * Q_TILE, LANES), F32),
                        pltpu.VMEM((N_KV_HEADS * Q_TILE, LANES), BF16)],
        compiler_params=_cparams(("arbitrary", "arbitrary")),
        name="nsa_prompt_attend",
    )(q, gates, cmp_k, cmp_v, ksel, vsel, kwin, vwin, onehot, pool, qaug)


def _sample_qpad(qf, tq):
    parts = []
    for g in range(N_KV_HEADS):
        piece = qf[:, g * KV_COLS:(g + 1) * KV_COLS]
        for r in range(GROUP):
            parts.append(_place_slot(piece, r, g))
    return jnp.concatenate(parts, axis=0).astype(BF16)


def _sample_rows_to_tokens(o, tq):
    chunks = []
    for g in range(N_KV_HEADS):
        chunk = jnp.zeros((tq, KV_COLS), F32)
        for r in range(GROUP):
            i0 = (g * GROUP + r) * tq
            chunk = chunk + _place_slot(o[i0:i0 + tq], g, r)
        chunks.append(chunk)
    return jnp.concatenate(chunks, axis=1)


def _nsa_sample_cmp_kernel(n_pages, past_len, tq, pt_ref, q_ref, w2_ref, pool_ref, slope_ref, *refs):
    a_refs = refs[:n_pages]
    b_refs = refs[n_pages:2 * n_pages]
    ocmp_ref, sel_ref = refs[2 * n_pages:]
    del pt_ref
    rows = N_HEADS * tq
    a = jnp.concatenate([r[0] for r in a_refs], axis=0)
    b = jnp.concatenate([r[0] for r in b_refs], axis=0)
    nseg = a.shape[0]
    kc = _finish_compress(a[:, :KV_COLS], b[:, :KV_COLS], w2_ref[0]).astype(BF16)
    vc = _finish_compress(a[:, KV_COLS:], b[:, KV_COLS:], w2_ref[1]).astype(BF16)

    qpad = _sample_qpad(q_ref[0] * (HEAD_DIM ** -0.5), tq)
    slope = slope_ref[...]
    rowi = lax.broadcasted_iota(jnp.int32, (rows, 1), 0)
    tpos = past_len + rowi % tq
    tposf = tpos.astype(F32)
    nc = (past_len + tq - L_CMP) // D_CMP + 1
    n_i = lax.broadcasted_iota(jnp.int32, (1, nseg), 1)
    c_end = jnp.where(n_i < nc, n_i * D_CMP + (L_CMP - 1), jnp.int32(2 ** 30))
    center = (n_i * D_CMP).astype(F32) + (L_CMP - 1) / 2
    s = _dot_nt(qpad, kc) - slope * (tposf - center)
    p_cmp = _masked_softmax(s, c_end <= tpos)
    ocmp_ref[0] = _dot(p_cmp.astype(BF16), vc)

    imps = []
    for g in range(N_KV_HEADS):
        i0 = g * GROUP * tq
        imp = p_cmp[i0:i0 + tq]
        for r in range(1, GROUP):
            imp = imp + p_cmp[i0 + r * tq:i0 + (r + 1) * tq]
        imps.append(imp)
    tpos_g = past_len + lax.broadcasted_iota(jnp.int32, (N_KV_HEADS * tq, 1), 0) % tq
    score = _selection_scores(_exact_pool(jnp.concatenate(imps, axis=0), pool_ref[...]), tpos_g)
    n_sel_rows = N_KV_HEADS * tq
    pad = jnp.full((LANES - n_sel_rows, LANES), NEG_INF, F32)
    sel = _topk_mask(jnp.concatenate([score, pad], axis=0), TOP_N)[:n_sel_rows]
    sel_ref[0] = jnp.concatenate([sel[g * tq:(g + 1) * tq] for g in range(N_KV_HEADS) for _ in range(GROUP)],
                                 axis=0).astype(BF16)


def _nsa_sample_cmp(page_table, q, a_all, b_all, w2_bd, pool, slope_rows, page, past_len):
    bs, tq, d = q.shape
    n_pages = page_table.shape[1]
    seg_pp = page // D_CMP
    rows = N_HEADS * tq
    a3 = a_all.reshape(-1, seg_pp, 2 * KV_COLS)
    b3 = b_all.reshape(-1, seg_pp, 2 * KV_COLS)

    def page_spec(p):
        return pl.BlockSpec((1, seg_pp, 2 * KV_COLS), lambda s, pt: (pt[s, p], 0, 0))

    cst = lambda shape: pl.BlockSpec(shape, lambda s, pt: (0,) * len(shape))
    grid_spec = pltpu.PrefetchScalarGridSpec(
        num_scalar_prefetch=1,
        grid=(bs,),
        in_specs=[pl.BlockSpec((1, tq, d), lambda s, pt: (s, 0, 0)),
                  cst(w2_bd.shape), cst(pool.shape), cst(slope_rows.shape)]
                 + [page_spec(p) for p in range(n_pages)] * 2,
        out_specs=[pl.BlockSpec((1, rows, KV_COLS), lambda s, pt: (s, 0, 0)),
                   pl.BlockSpec((1, rows, LANES), lambda s, pt: (s, 0, 0))],
    )
    return pl.pallas_call(
        functools.partial(_nsa_sample_cmp_kernel, n_pages, past_len, tq),
        grid_spec=grid_spec,
        out_shape=[jax.ShapeDtypeStruct((bs, rows, KV_COLS), F32),
                   jax.ShapeDtypeStruct((bs, rows, LANES), BF16)],
        compiler_params=_cparams(("arbitrary",)),
        name="nsa_sample_cmp",
    )(page_table, q, w2_bd, pool, slope_rows, *([a3] * n_pages), *([b3] * n_pages))


def _nsa_sample_attend_kernel(n_pages, page, tq, pt_ref, q_ref, gates_ref, kvn_ref, winn_ref, winp_ref,
                              ocmp_ref, sel_ref, expand_ref, slope_ref, *refs):
    page_refs = refs[:n_pages]
    o_ref, newwin_ref = refs[n_pages:]
    del pt_ref
    rows = N_HEADS * tq
    past_len = n_pages * page
    wbuf = winp_ref.shape[2]
    qpad = _sample_qpad(q_ref[0] * (HEAD_DIM ** -0.5), tq)
    slope = slope_ref[...]
    rowi = lax.broadcasted_iota(jnp.int32, (rows, 1), 0)
    tpos = past_len + rowi % tq
    tposf = tpos.astype(F32)
    newpos = past_len + lax.broadcasted_iota(jnp.int32, (1, LANES), 1)
    pad_rows = lambda a: jnp.concatenate([a, jnp.zeros((LANES - tq, a.shape[1]), F32)], axis=0)

    kvn = pad_rows(kvn_ref[0])
    k_new = kvn[:, 2 * KV_COLS:3 * KV_COLS].astype(BF16)
    v_new = kvn[:, 3 * KV_COLS:4 * KV_COLS].astype(BF16)
    scores = [_dot(qpad, r[0, :KV_COLS, :].astype(BF16)) for r in page_refs]
    scores.append(_dot_nt(qpad, k_new))
    s = jnp.concatenate(scores, axis=1)
    nk = past_len + LANES
    kpos = lax.broadcasted_iota(jnp.int32, (1, nk), 1)
    picked = _dot(sel_ref[0], expand_ref[...]) > 0.5
    s = s - slope * (tposf - kpos.astype(F32))
    p = _masked_softmax(s, picked & (kpos <= tpos)).astype(BF16)
    o_sel = _dot(p[:, past_len:], v_new)
    for i, r in enumerate(page_refs):
        o_sel = o_sel + _dot_nt(p[:, i * page:(i + 1) * page], r[0, KV_COLS:, :].astype(BF16))

    winp = winp_ref[0]
    winn = pad_rows(winn_ref[0])
    winn_t = pltpu.roll(winn.T, LANES - tq, 1)
    lane_w = lax.broadcasted_iota(jnp.int32, (1, wbuf), 1)
    newwin_ref[0] = jnp.where(lane_w >= wbuf - tq, jnp.concatenate([winn_t] * (wbuf // LANES), axis=1),
                              pltpu.roll(winp, wbuf - tq, 1))
    sp = _dot(qpad, winp[:KV_COLS, :].astype(BF16))
    sn = _dot_nt(qpad, winn[:, :KV_COLS].astype(BF16))
    wpos_p = past_len - wbuf + lax.broadcasted_iota(jnp.int32, (1, wbuf), 1)
    wpos = jnp.concatenate([wpos_p, newpos], axis=1)
    wd = tpos - wpos
    s = jnp.concatenate([sp, sn], axis=1) - slope * wd.astype(F32)
    p = _masked_softmax(s, (wd >= 0) & (wd < WINDOW) & (wpos >= 0)).astype(BF16)
    o_win = _dot_nt(p[:, :wbuf], winp[KV_COLS:, :].astype(BF16)) + _dot(p[:, wbuf:], winn[:, KV_COLS:].astype(BF16))

    gates = gates_ref[0]
    o_cmp = ocmp_ref[0]
    merged = []
    for h in range(N_HEADS):
        rs = slice(h * tq, (h + 1) * tq)
        merged.append(gates[:, 3 * h:3 * h + 1] * o_cmp[rs] + gates[:, 3 * h + 1:3 * h + 2] * o_sel[rs]
                      + gates[:, 3 * h + 2:3 * h + 3] * o_win[rs])
    o_ref[0] = _sample_rows_to_tokens(jnp.concatenate(merged, axis=0), tq).astype(o_ref.dtype)


def _nsa_sample_attend(page_table, cache_t, q, gates, kv_new, win_new, win_past_t, o_cmp, sel, expand, slope_rows):
    bs, tq, d = q.shape
    n_pages = page_table.shape[1]
    page = cache_t.shape[2]
    wbuf = win_past_t.shape[2]
    rows = N_HEADS * tq

    def page_spec(p):
        return pl.BlockSpec((1, 2 * KV_COLS, page), lambda s, pt: (pt[s, p], 1, 0))

    per = lambda r, n: pl.BlockSpec((1, r, n), lambda s, pt: (s, 0, 0))
    cst = lambda shape: pl.BlockSpec(shape, lambda s, pt: (0,) * len(shape))
    grid_spec = pltpu.PrefetchScalarGridSpec(
        num_scalar_prefetch=1,
        grid=(bs,),
        in_specs=[per(tq, d), per(tq, LANES), per(tq, 4 * KV_COLS), per(tq, 2 * KV_COLS),
                  per(2 * KV_COLS, wbuf), per(rows, KV_COLS), per(rows, LANES),
                  cst(expand.shape), cst(slope_rows.shape)]
                 + [page_spec(p) for p in range(n_pages)],
        out_specs=[per(tq, d), per(2 * KV_COLS, wbuf)],
    )
    return pl.pallas_call(
        functools.partial(_nsa_sample_attend_kernel, n_pages, page, tq),
        grid_spec=grid_spec,
        out_shape=[jax.ShapeDtypeStruct((bs, tq, d), BF16),
                   jax.ShapeDtypeStruct((bs, 2 * KV_COLS, wbuf), F32)],
        compiler_params=_cparams(("arbitrary",)),
        name="nsa_sample_attend",
    )(page_table, q, gates, kv_new, win_new, win_past_t, o_cmp, sel, expand, slope_rows,
      *([cache_t] * n_pages))


def _cmul_add(ar, ai, hr, hi, xr, xi):
    return ar * hr - ai * hi + xr, ar * hi + ai * hr + xi


def _s5_kernel(mode, tm, x_ref, h0r_ref, h0i_ref, gain_ref, bre_ref, bim_ref, are_ref, aim_ref,
               pwr_ref, pwi_ref, cre_ref, cim_ref, d_ref, wglu_ref, perm_ref, permt_ref,
               xo_ref, hfr_ref, hfi_ref, hr_s, hi_s, cr_s, ci_s):
    d = x_ref.shape[1]
    nstate = cr_s.shape[1]
    nblk = bre_ref.shape[0]
    cb = d // nblk
    sb = nstate // nblk
    tpb = sb // LANES
    x = x_ref[...]
    uf = _rms(x, gain_ref[...])
    ub = uf.astype(BF16)

    if mode == "carry":
        nj = tm // SUBLANES
        ub = _dot(perm_ref[...], ub).astype(BF16)
    for j in range(nblk):
        uj = ub[:, j * cb:(j + 1) * cb]
        br = _dot(uj, bre_ref[j])
        bi = _dot(uj, bim_ref[j])
        for k in range(tpb):
            hr_s[j * tpb + k] = br[:, k * LANES:(k + 1) * LANES]
            hi_s[j * tpb + k] = bi[:, k * LANES:(k + 1) * LANES]

    lc = SSM_LANE_CHUNK
    par = lc // LANES
    if mode == "carry":
        @pl.when(pl.program_id(1) == 0)
        def _():
            cr_s[...] = jnp.broadcast_to(h0r_ref[0], cr_s.shape)
            ci_s[...] = jnp.broadcast_to(h0i_ref[0], ci_s.shape)

        sub = lax.broadcasted_iota(jnp.int32, (SUBLANES, 1), 0)
        for c in range(nstate // lc):
            sl = slice(c * lc, (c + 1) * lc)
            ar = jnp.broadcast_to(are_ref[:, sl], (SUBLANES, lc))
            ai = jnp.broadcast_to(aim_ref[:, sl], (SUBLANES, lc))

            tiles = range(c * par, (c + 1) * par)
            lane = lambda v, k: v[:, k * LANES:(k + 1) * LANES]

            def p1(jj, carry):
                st, mult = carry
                idx = pl.ds(pl.multiple_of(jj * SUBLANES, SUBLANES), SUBLANES)
                out = []
                for k, q in enumerate(tiles):
                    nr, ni = _cmul_add(mult[2 * k], mult[2 * k + 1], st[2 * k], st[2 * k + 1],
                                       hr_s[q, idx, :], hi_s[q, idx, :])
                    hr_s[q, idx, :] = nr
                    hi_s[q, idx, :] = ni
                    out += [nr, ni]
                return tuple(out), mult

            z = jnp.zeros((SUBLANES, LANES), F32)
            mult0 = tuple(lane(v, k) for k in range(par) for v in (ar, ai))
            lf, _ = lax.fori_loop(0, nj, p1, ((z,) * (2 * par), mult0))
            lfr = jnp.concatenate(lf[0::2], axis=1)
            lfi = jnp.concatenate(lf[1::2], axis=1)

            alr = pwr_ref[nj - 1:nj, sl]
            ali = pwi_ref[nj - 1:nj, sl]
            inr = cr_s[0:1, sl]
            ini = ci_s[0:1, sl]
            hin_r = jnp.zeros((SUBLANES, lc), F32)
            hin_i = jnp.zeros((SUBLANES, lc), F32)
            for s in range(SUBLANES):
                hin_r = jnp.where(sub == s, inr, hin_r)
                hin_i = jnp.where(sub == s, ini, hin_i)
                inr, ini = _cmul_add(alr, ali, inr, ini, lfr[s:s + 1], lfi[s:s + 1])
            cr_s[:, sl] = jnp.broadcast_to(inr, (SUBLANES, lc))
            ci_s[:, sl] = jnp.broadcast_to(ini, (SUBLANES, lc))

            def p2(jj, carry):
                idx = pl.ds(pl.multiple_of(jj * SUBLANES, SUBLANES), SUBLANES)
                pr = pwr_ref[pl.ds(jj, 1), sl]
                pi_ = pwi_ref[pl.ds(jj, 1), sl]
                for k, q in enumerate(tiles):
                    nr, ni = _cmul_add(lane(pr, k), lane(pi_, k), lane(hin_r, k), lane(hin_i, k),
                                       hr_s[q, idx, :], hi_s[q, idx, :])
                    hr_s[q, idx, :] = nr
                    hi_s[q, idx, :] = ni
                return carry

            lax.fori_loop(0, nj, p2, 0, unroll=4)

        @pl.when(pl.program_id(1) == pl.num_programs(1) - 1)
        def _():
            hfr_ref[0] = cr_s[0:1, :]
            hfi_ref[0] = ci_s[0:1, :]
    else:
        nseq = tm // mode
        for q in range(nstate // LANES):
            sl = slice(q * LANES, (q + 1) * LANES)
            ar = jnp.broadcast_to(are_ref[:, sl], (nseq, LANES))
            ai = jnp.broadcast_to(aim_ref[:, sl], (nseq, LANES))
            sr = h0r_ref[:, sl]
            si = h0i_ref[:, sl]
            for t in range(mode):
                idx = pl.ds(t, nseq, stride=mode)
                sr, si = _cmul_add(ar, ai, sr, si, hr_s[q, idx, :], hi_s[q, idx, :])
                hr_s[q, idx, :] = sr
                hi_s[q, idx, :] = si
            hfr_ref[:, sl] = sr
            hfi_ref[:, sl] = si

    def block_rows(ref, j):
        return jnp.concatenate([ref[j * tpb + k] for k in range(tpb)], axis=1).astype(BF16)

    ys = []
    for j in range(nblk):
        ys.append(_dot(block_rows(hr_s, j), cre_ref[j]) - _dot(block_rows(hi_s, j), cim_ref[j]))
    y = jnp.concatenate(ys, axis=1)
    if mode == "carry":
        y = _exact_rows(permt_ref[...], y)
    y = y + d_ref[...] * uf
    ab = _dot(_gelu(y).astype(BF16), wglu_ref[...])
    xo_ref[...] = x + ab[:, :d] * _sigmoid(ab[:, d:])


def _s5_layer(x2, h0r, h0i, seq_len, n_seq, gain, prm):
    rows, d = x2.shape
    nstate = h0r.shape[1]
    if seq_len >= LAYER_ROWS:
        mode, tm = "carry", LAYER_ROWS
        nt = seq_len // tm
        grid = (n_seq, nt)
        xmap = lambda b, i: (b * nt + i, 0)
        h0_spec = pl.BlockSpec((1, 1, nstate), lambda b, i: (b, 0, 0))
        hf_spec = pl.BlockSpec((1, 1, nstate), lambda b, i: (b, 0, 0))
        h0r, h0i = h0r[:, None, :], h0i[:, None, :]
        hf_shape = (n_seq, 1, nstate)
        sem = ("arbitrary", "arbitrary")
        cst = lambda shape: pl.BlockSpec(shape, lambda b, i: (0,) * len(shape), pipeline_mode=pl.Buffered(1))
    else:
        mode = seq_len
        tm = min(LAYER_ROWS, rows)
        nseq_t = tm // seq_len
        grid = (rows // tm,)
        xmap = lambda i: (i, 0)
        h0_spec = pl.BlockSpec((nseq_t, nstate), lambda i: (i, 0))
        hf_spec = pl.BlockSpec((nseq_t, nstate), lambda i: (i, 0))
        hf_shape = (n_seq, nstate)
        sem = ("arbitrary",)
        cst = lambda shape: pl.BlockSpec(shape, lambda i: (0,) * len(shape), pipeline_mode=pl.Buffered(1))
    pw_r, pw_i = prm["pw_r"], prm["pw_i"]
    src = np.arange(tm).reshape(tm // SUBLANES, SUBLANES)
    perm = np.zeros((tm, tm), np.float32)
    perm[np.arange(tm), ((src % SUBLANES) * (tm // SUBLANES) + src // SUBLANES).reshape(-1)] = 1.0
    consts = [gain, prm["b_r"], prm["b_i"], prm["a_r"], prm["a_i"], pw_r, pw_i, prm["c_r"], prm["c_i"],
              prm["d"], prm["w_glu"], jnp.asarray(perm, BF16), jnp.asarray(perm.T, BF16)]
    outs = pl.pallas_call(
        functools.partial(_s5_kernel, mode, tm),
        grid=grid,
        in_specs=[pl.BlockSpec((tm, d), xmap), h0_spec, h0_spec] + [cst(c.shape) for c in consts],
        out_specs=[pl.BlockSpec((tm, d), xmap), hf_spec, hf_spec],
        out_shape=[jax.ShapeDtypeStruct((rows, d), F32), jax.ShapeDtypeStruct(hf_shape, F32),
                   jax.ShapeDtypeStruct(hf_shape, F32)],
        scratch_shapes=[pltpu.VMEM((nstate // LANES, tm, LANES), F32),
                        pltpu.VMEM((nstate // LANES, tm, LANES), F32),
                        pltpu.VMEM((SUBLANES, nstate), F32), pltpu.VMEM((SUBLANES, nstate), F32)],
        compiler_params=_cparams(sem),
        name="s5_layer",
    )(x2, h0r, h0i, *consts)
    xo, hfr, hfi = outs
    return xo, hfr.reshape(n_seq, nstate), hfi.reshape(n_seq, nstate)


def _s5_params(a_re, a_im, log_dt, b_re, b_im, c_re, c_im, d_skip, w_glu, n_pow):
    g, p = a_re.shape
    ch = b_re.shape[2]
    lr, li = a_re.astype(F32), a_im.astype(F32)
    dt = jnp.exp(log_dt.astype(F32))[:, None]
    mag = jnp.exp(lr * dt)
    ar, ai = mag * jnp.cos(li * dt), mag * jnp.sin(li * dt)
    den = lr * lr + li * li
    kr = ((ar - 1.0) * lr + ai * li) / den
    ki = (ai * lr - (ar - 1.0) * li) / den
    br, bi = b_re.astype(F32), b_im.astype(F32)
    bbar_r = kr[:, :, None] * br - ki[:, :, None] * bi
    bbar_i = kr[:, :, None] * bi + ki[:, :, None] * br
    nblk = g // SSM_GROUP_BLOCK
    eye = jnp.eye(SSM_GROUP_BLOCK, dtype=F32)

    def b_blocks(m):
        m = m.reshape(nblk, SSM_GROUP_BLOCK, p, ch)
        return jnp.einsum("jgpc,gh->jgchp", m, eye).reshape(nblk, SSM_GROUP_BLOCK * ch, SSM_GROUP_BLOCK * p).astype(BF16)

    def c_blocks(m):
        m = m.reshape(nblk, SSM_GROUP_BLOCK, ch, p)
        return jnp.einsum("jgcp,gh->jgphc", m, eye).reshape(nblk, SSM_GROUP_BLOCK * p, SSM_GROUP_BLOCK * ch).astype(BF16)

    ar, ai = ar.reshape(1, g * p), ai.reshape(1, g * p)
    pw_r, pw_i = [ar], [ai]
    for _ in range(n_pow - 1):
        pw_r, pw_i = pw_r + [pw_r[-1] * ar - pw_i[-1] * ai], pw_i + [pw_r[-1] * ai + pw_i[-1] * ar]
    return dict(b_r=b_blocks(bbar_r), b_i=b_blocks(bbar_i), a_r=ar, a_i=ai,
                pw_r=jnp.concatenate(pw_r, axis=0), pw_i=jnp.concatenate(pw_i, axis=0),
                c_r=c_blocks(c_re.astype(F32)), c_i=c_blocks(c_im.astype(F32)),
                d=d_skip.astype(F32).reshape(1, -1), w_glu=w_glu.astype(BF16))


def _ffn_ple_kernel(mode, final_norm, x_ref, p_ref, b1_ref, b2_ref, gffn_ref, wup_ref, cw_ref, cb_ref,
                    wdown_ref, gple_ref, wproj_ref, wgate_ref, gfin_ref, xo_ref, tail_ref, carry_ref):
    tm = x_ref.shape[0]
    f = cw_ref.shape[1]
    x = x_ref[...]
    u = _rms(x, gffn_ref[...]).astype(BF16)
    rowi = lax.broadcasted_iota(jnp.int32, (tm, 1), 0)
    if mode == "carry":
        @pl.when(pl.program_id(1) == 0)
        def _():
            carry_ref[SUBLANES - 2:SUBLANES, :] = b1_ref[0]

    fc = f // FFN_CHUNKS
    x1 = x
    for c in range(FFN_CHUNKS):
        cs = slice(c * fc, (c + 1) * fc)
        h = _dot(u, wup_ref[:, cs])
        gate_branch = _dot(u, wup_ref[:, f + c * fc:f + (c + 1) * fc])
        r1 = pltpu.roll(h, 1, 0)
        r2 = pltpu.roll(h, 2, 0)
        if mode == "carry":
            c0 = carry_ref[SUBLANES - 2:SUBLANES - 1, cs]
            c1 = carry_ref[SUBLANES - 1:SUBLANES, cs]
            hm1 = jnp.where(rowi == 0, c1, r1)
            hm2 = jnp.where(rowi == 0, c0, jnp.where(rowi == 1, c1, r2))
            carry_ref[:, cs] = h[tm - SUBLANES:tm, :]
            tail_ref[:, cs] = h[tm - SUBLANES:tm, :]
        else:
            t = rowi % mode
            hm1 = jnp.where(t == 0, b1_ref[:, cs], r1)
            hm2 = jnp.where(t < 2, b2_ref[:, cs], r2)
            tail_ref[:, cs] = h
        conv = cb_ref[:, cs] + cw_ref[0:1, cs] * hm2 + cw_ref[1:2, cs] * hm1 + cw_ref[2:3, cs] * h
        act = (_gelu(conv) * gate_branch).astype(BF16)
        x1 = x1 + _dot(act, wdown_ref[cs, :])
    gate = _sigmoid(_dot(_rms(x1, gple_ref[...]).astype(BF16), wgate_ref[...]))
    x2 = x1 + _dot(p_ref[...].astype(BF16), wproj_ref[...]) * gate
    xo_ref[...] = _rms(x2, gfin_ref[...]) if final_norm else x2


def _ffn_ple(x2, p2, buf, seq_len, n_seq, final_norm, gffn, wup, cw, cb, wdown, gple, wproj, wgate, gfin):
    rows, d = x2.shape
    f = cw.shape[1]
    ple = p2.shape[1]
    consts = [gffn, wup, cw, cb, wdown, gple, wproj, wgate, gfin]
    if seq_len >= LAYER_ROWS:
        mode, tm = "carry", LAYER_ROWS
        nt = seq_len // tm
        grid = (n_seq, nt)
        rmap = lambda b, i: (b * nt + i, 0)
        b1, b2 = buf, buf
        bspec = pl.BlockSpec((1, CONV_W - 1, f), lambda b, i: (b, 0, 0))
        tail_rows = SUBLANES
        sem = ("arbitrary", "arbitrary")
        cst = lambda shape: pl.BlockSpec(shape, lambda b, i: (0,) * len(shape), pipeline_mode=pl.Buffered(1))
    else:
        mode = seq_len
        tm = min(LAYER_ROWS, rows)
        grid = (rows // tm,)
        rmap = lambda i: (i, 0)
        b1 = jnp.pad(buf[:, 1:], ((0, 0), (0, seq_len - 1), (0, 0))).reshape(rows, f)
        b2 = jnp.pad(buf, ((0, 0), (0, seq_len - (CONV_W - 1)), (0, 0))).reshape(rows, f)
        bspec = pl.BlockSpec((tm, f), rmap)
        tail_rows = tm
        sem = ("arbitrary",)
        cst = lambda shape: pl.BlockSpec(shape, lambda i: (0,) * len(shape), pipeline_mode=pl.Buffered(1))
    n_tiles = rows // tm
    xo, tail = pl.pallas_call(
        functools.partial(_ffn_ple_kernel, mode, final_norm),
        grid=grid,
        in_specs=[pl.BlockSpec((tm, d), rmap), pl.BlockSpec((tm, ple), rmap), bspec, bspec]
                 + [cst(c.shape) for c in consts],
        out_specs=[pl.BlockSpec((tm, d), rmap), pl.BlockSpec((tail_rows, f), rmap)],
        out_shape=[jax.ShapeDtypeStruct((rows, d), F32), jax.ShapeDtypeStruct((n_tiles * tail_rows, f), F32)],
        scratch_shapes=[pltpu.VMEM((SUBLANES, f), F32)],
        compiler_params=_cparams(sem),
        name="ffn_ple",
    )(x2, p2, b1, b2, *consts)
    if mode == "carry":
        new_buf = tail.reshape(n_seq, n_tiles // n_seq, SUBLANES, f)[:, -1, SUBLANES - (CONV_W - 1):, :]
    else:
        new_buf = tail.reshape(n_seq, seq_len, f)[:, seq_len - (CONV_W - 1):, :]
    return xo, new_buf


def _block_diag_heads(w):
    eye = jnp.eye(N_KV_HEADS, dtype=w.dtype)
    out = jnp.einsum("...de,gh->...gdhe", w, eye)
    return out.reshape(*w.shape[:-2], KV_COLS, KV_COLS)


def _slopes():
    return np.exp2(-8.0 * np.arange(1, N_HEADS + 1, dtype=np.float64) / N_HEADS).astype(np.float32)


def _query_aug(slopes):
    rest = jnp.asarray(slopes, F32)
    aug = jnp.zeros((N_HEADS, LANES), F32)
    for i in range(N_PIECES):
        piece = rest.astype(BF16).astype(F32)
        rest = rest - piece
        aug = aug.at[:, AUG_LANE + i].set(piece).at[:, AUG_LANE + N_PIECES + i].set(piece)
    return aug


def _expand_matrix(n_keys):
    return (np.arange(LANES)[:, None] == (np.arange(n_keys)[None, :] // L_SEL)).astype(np.float32)


def _pool_matrix(n_cmp):
    return (np.arange(n_cmp)[:, None] // SEL_PER_CMP == np.arange(LANES)[None, :]).astype(np.float32)


def _nsa_layer(xp, xs, cache_l, cache_win_l, page_table, gain, w_in, w_out, pe, w1, w2):
    bsz, t, d = xp.shape
    bs, tq, _ = xs.shape
    page = cache_l.shape[1]
    n_pages = page_table.shape[1]
    past_len = n_pages * page
    n_pool = cache_l.shape[0]
    assert t % SEL_CHUNK == 0 and t // L_SEL <= LANES and t >= WINDOW + Q_TILE
    assert (past_len + LANES) // L_SEL <= LANES and page % D_CMP == 0 and tq <= SUBLANES

    qcols = N_HEADS * HEAD_DIM
    nmain = qcols + 6 * KV_COLS
    w_main = w_in[:, :nmain].astype(BF16)
    w_gate = jnp.pad(w_in[:, nmain:], ((0, 0), (0, LANES - (w_in.shape[1] - nmain)))).astype(BF16)
    w_out_b = w_out.astype(BF16)
    pe_t = jnp.tile(pe, (1, 1, N_KV_HEADS))
    w1_bd = _block_diag_heads(w1).astype(BF16)
    w2_bd = _block_diag_heads(w2).astype(BF16)
    slopes = _slopes()

    q, gates, ksel, vsel, kwin, vwin, kv_t, win_t, kv_pages = _attn_in_proj(
        xp.reshape(bsz * t, d), gain, w_main, w_gate, seq_len=t)
    sh = lambda a: a.reshape(bsz, t, a.shape[-1])
    a_p, b_p = _pool_segment_sums(kv_pages, pe_t, w1_bd, math.gcd(kv_pages.shape[0], POOL_PAGES))
    seg3 = lambda a: a.reshape(bsz, t // D_CMP, 2 * KV_COLS)
    cmp_k, cmp_v = _prompt_summaries(seg3(a_p), seg3(b_p), w2_bd)
    onehot = jnp.asarray(_expand_matrix(t).T, BF16)
    pool = jnp.asarray(_pool_matrix(t // D_CMP), BF16)
    o = _nsa_prompt_attend(sh(q), sh(gates), cmp_k, cmp_v, sh(ksel), sh(vsel), sh(kwin), sh(vwin), onehot, pool,
                           jnp.asarray(_query_aug(slopes)))
    xp_new = _matmul_res(o.reshape(bsz * t, d), w_out_b, xp.reshape(bsz * t, d)).reshape(bsz, t, d)
    keep = min(WINDOW, t)
    from_t = lambda a, n: jnp.transpose(a.reshape(a.shape[0], n, N_KV_HEADS, HEAD_DIM, a.shape[-1]), (0, 4, 1, 2, 3))
    kv_p = from_t(kv_t, 4)
    win_p = from_t(win_t[:, :, t - keep:], 2)

    cache_t = jnp.transpose(cache_l, (0, 2, 3, 4, 1)).reshape(n_pool, 4 * KV_COLS, page)
    wbuf = cache_win_l.shape[1]
    win_past_t = jnp.transpose(cache_win_l, (0, 2, 3, 4, 1)).reshape(bs, 2 * KV_COLS, wbuf)
    qs, gates_s, kvs, wins = _attn_in_proj(xs.reshape(bs * tq, d), gain, w_main, w_gate)
    shs = lambda a: a.reshape(bs, tq, a.shape[-1])
    a_all, b_all = _pool_segment_sums(cache_t, pe_t, w1_bd, math.gcd(n_pool, POOL_PAGES))
    nseg_s = past_len // D_CMP
    pool_s = jnp.asarray(_pool_matrix(nseg_s), BF16)
    slope_s = jnp.asarray(np.repeat(slopes, tq)[:, None])
    o_cmp, sel = _nsa_sample_cmp(page_table, shs(qs), a_all, b_all, w2_bd, pool_s, slope_s, page, past_len)
    expand_s = jnp.asarray(_expand_matrix(past_len + LANES), BF16)
    o_s, new_win_t = _nsa_sample_attend(page_table, cache_t, shs(qs), shs(gates_s), shs(kvs), shs(wins), win_past_t,
                                        o_cmp, sel, expand_s, slope_s)
    xs_new = _matmul_res(o_s.reshape(bs * tq, d), w_out_b, xs.reshape(bs * tq, d)).reshape(bs, tq, d)
    kv_s = shs(kvs).reshape(bs, tq, 4, N_KV_HEADS, HEAD_DIM)
    win_s = from_t(new_win_t, 2)
    return xp_new, xs_new, kv_p, kv_s, win_p, win_s


def kernel(x_prompt, x_sample, cache_kv, cache_win, state_ssm_re, state_ssm_im, state_conv, page_table,
           p_prompt, p_sample, norm_mix, norm_ffn, norm_ple, norm_final, w_attn_in, w_attn_out,
           cmp_pe, cmp_w1, cmp_w2, ssm_a_re, ssm_a_im, ssm_log_dt, ssm_b_re, ssm_b_im, ssm_c_re, ssm_c_im,
           ssm_d, w_glu, w_ffn_up, ffn_conv_w, ffn_conv_b, w_ffn_down, w_ple_proj, w_ple_gate):
    bsz, t, d = x_prompt.shape
    bs, tq, _ = x_sample.shape
    depth = norm_mix.shape[0]
    f = ffn_conv_w.shape[2]
    g, p = ssm_a_re.shape[1:]
    xp, xs = x_prompt, x_sample
    row = lambda v: v.reshape(1, -1).astype(F32)
    kv_p, kv_s, win_p, win_s = [], [], [], []
    sre_p, sim_p, sre_s, sim_s = [], [], [], []
    cb_p, cb_s = [], []
    for i in range(depth):
        j = i // 2
        if i % 2 == 0:
            xp, xs, kvp, kvs, wp, ws = _nsa_layer(xp, xs, cache_kv[j], cache_win[j], page_table, row(norm_mix[i]),
                                                  w_attn_in[j], w_attn_out[j], cmp_pe[j], cmp_w1[j], cmp_w2[j])
            kv_p.append(kvp)
            kv_s.append(kvs)
            win_p.append(wp)
            win_s.append(ws)
        else:
            prm = _s5_params(ssm_a_re[j], ssm_a_im[j], ssm_log_dt[j], ssm_b_re[j], ssm_b_im[j], ssm_c_re[j],
                             ssm_c_im[j], ssm_d[j], w_glu[j], LAYER_ROWS // SUBLANES)
            zero = jnp.zeros((bsz, g * p), F32)
            xp2, hr, hi = _s5_layer(xp.reshape(bsz * t, d), zero, zero, t, bsz, row(norm_mix[i]), prm)
            xs2, hrs, his = _s5_layer(xs.reshape(bs * tq, d), state_ssm_re[j].reshape(bs, g * p).astype(F32),
                                      state_ssm_im[j].reshape(bs, g * p).astype(F32), tq, bs, row(norm_mix[i]), prm)
            xp, xs = xp2.reshape(bsz, t, d), xs2.reshape(bs, tq, d)
            sre_p.append(hr.reshape(bsz, g, p))
            sim_p.append(hi.reshape(bsz, g, p))
            sre_s.append(hrs.reshape(bs, g, p))
            sim_s.append(his.reshape(bs, g, p))
        last = i == depth - 1
        ffn_w = (row(norm_ffn[i]), w_ffn_up[i].astype(BF16), ffn_conv_w[i].astype(F32), row(ffn_conv_b[i]),
                 w_ffn_down[i].astype(BF16), row(norm_ple[i]), w_ple_proj[i].astype(BF16),
                 w_ple_gate[i].astype(BF16), row(norm_final))
        xp2, bp = _ffn_ple(xp.reshape(bsz * t, d), p_prompt[i].reshape(bsz * t, -1),
                           jnp.zeros((bsz, CONV_W - 1, f), F32), t, bsz, last, *ffn_w)
        xs2, bs_new = _ffn_ple(xs.reshape(bs * tq, d), p_sample[i].reshape(bs * tq, -1), state_conv[i].astype(F32),
                               tq, bs, last, *ffn_w)
        xp, xs = xp2.reshape(bsz, t, d), xs2.reshape(bs, tq, d)
        cb_p.append(bp)
        cb_s.append(bs_new)
    return (xp, xs, jnp.stack(kv_p), jnp.stack(kv_s), jnp.stack(win_p), jnp.stack(win_s),
            jnp.stack(sre_p), jnp.stack(sim_p), jnp.stack(sre_s), jnp.stack(sim_s),
            jnp.stack(cb_p), jnp.stack(cb_s))
```

```python
import functools
import math

import numpy as np
import jax
import jax.numpy as jnp
from jax import lax
from jax.experimental import pallas as pl
from jax.experimental.pallas import tpu as pltpu

F32 = jnp.float32
BF16 = jnp.bfloat16

N_HEADS = 16
HEAD_DIM = 64
N_KV_HEADS = 4
GROUP = N_HEADS // N_KV_HEADS
KV_COLS = N_KV_HEADS * HEAD_DIM
L_CMP = 32
D_CMP = 16
L_SEL = 64
TOP_N = 16
WINDOW = 512
SSM_CH = 16
STATE_P = 64
CONV_W = 3
NORM_EPS = 1e-6
NEG_INF = -1e30
FORCE_BONUS = 1e4
SEL_PER_CMP = L_SEL // D_CMP

LANES = 128
SUBLANES = 8
VMEM_LIMIT = 56 * 1024 * 1024

PROJ_ROWS = 512
LAYER_ROWS = 256
POOL_PAGES = 32
Q_TILE = 128
SEL_CHUNK = 512
SEL_UNROLL = 4
SSM_LANE_CHUNK = 512
SSM_GROUP_BLOCK = 16
FFN_CHUNKS = 2


def _cparams(sem):
    return pltpu.CompilerParams(dimension_semantics=sem, vmem_limit_bytes=VMEM_LIMIT)


def _const_spec(shape):
    nd = len(shape)
    return pl.BlockSpec(shape, lambda *_: (0,) * nd, pipeline_mode=pl.Buffered(1))


def _rms(x, gain):
    y = x * lax.rsqrt(jnp.mean(x * x, axis=-1, keepdims=True) + NORM_EPS)
    return y * gain


def _gelu(x):
    c = math.sqrt(2.0 / math.pi)
    return x * (0.5 * (1.0 + jnp.tanh(c * (x + 0.044715 * (x * x * x)))))


def _sigmoid(x):
    return 1.0 / (1.0 + jnp.exp(-x))


def _dot(a, b):
    return jnp.dot(a, b, preferred_element_type=F32)


def _dot_nt(a, b):
    return lax.dot_general(a, b, (((1,), (1,)), ((), ())), preferred_element_type=F32)


def _masked_softmax(s, mask):
    sm = jnp.where(mask, s, NEG_INF)
    m = jnp.max(sm, axis=-1, keepdims=True)
    e = jnp.where(mask, jnp.exp(sm - m), 0.0)
    den = jnp.sum(e, axis=-1, keepdims=True)
    return e / jnp.where(den > 0.0, den, 1.0)


def _exact_pool(x, pool_bf):
    hi = x.astype(BF16)
    r1 = x - hi.astype(F32)
    mid = r1.astype(BF16)
    lo = (r1 - mid.astype(F32)).astype(BF16)
    return _dot(hi, pool_bf) + _dot(mid, pool_bf) + _dot(lo, pool_bf)


def _exact_rows(sel_bf, x):
    hi = x.astype(BF16)
    r1 = x - hi.astype(F32)
    mid = r1.astype(BF16)
    lo = (r1 - mid.astype(F32)).astype(BF16)
    return _dot(sel_bf, hi) + _dot(sel_bf, mid) + _dot(sel_bf, lo)


def _topk_mask(score, k):
    s = score.T
    cand = lax.broadcasted_iota(jnp.int32, s.shape, 0).astype(F32)
    sel = jnp.zeros(s.shape, F32)
    for _ in range(k):
        m = jnp.max(s, axis=0, keepdims=True)
        first = jnp.min(jnp.where(s == m, cand, float(LANES)), axis=0, keepdims=True)
        hit = cand == first
        sel = jnp.where(hit, 1.0, sel)
        s = jnp.where(hit, -jnp.inf, s)
    return sel.T


def _slot_ids():
    return lax.broadcasted_iota(jnp.int32, (1, KV_COLS), 1) // HEAD_DIM


def _place_slot(x, src_slot, dst_slot):
    shift = (HEAD_DIM * (dst_slot - src_slot)) % KV_COLS
    y = pltpu.roll(x, shift, 1) if shift else x
    return jnp.where(_slot_ids() == dst_slot, y, 0.0)


def _selection_scores(imp, tpos):
    blk = lax.broadcasted_iota(jnp.int32, imp.shape, 1)
    cur = tpos // L_SEL
    forced = (blk == 0) | (blk == cur) | (blk == cur - 1)
    started = blk * L_SEL <= tpos
    return jnp.where(started, imp + jnp.where(forced, FORCE_BONUS, 0.0), NEG_INF)


def _attn_in_kernel(seq_len, x_ref, gain_ref, wm_ref, wg_ref, q_ref, gates_ref, *refs):
    tm, d = x_ref.shape
    u = _rms(x_ref[...], gain_ref[...]).astype(BF16)
    z = _dot(u, wm_ref[...])
    q_ref[...] = z[:, :d]
    kv = z[:, d:d + 4 * KV_COLS]
    win = z[:, d + 4 * KV_COLS:d + 6 * KV_COLS]
    gates_ref[...] = _sigmoid(_dot(u, wg_ref[...]))
    if seq_len is None:
        kv_ref, win_ref = refs
        kv_ref[...] = kv
        win_ref[...] = win
        return
    ksel_ref, vsel_ref, kwin_ref, vwin_ref, kvt_ref, wint_ref, pages_ref = refs
    pos = (pl.program_id(0) % (seq_len // tm)) * tm + lax.broadcasted_iota(jnp.int32, (tm, 1), 0)
    aug = _position_aug(((pos // LANES) * LANES).astype(F32), (pos % LANES).astype(F32))
    lane = lax.broadcasted_iota(jnp.int32, (1, LANES), 1)
    for g in range(N_KV_HEADS):
        sl = slice(g * LANES, (g + 1) * LANES)
        ksel_ref[:, sl] = jnp.where(lane < AUG_LANE, _head_slot(kv[:, 2 * KV_COLS:3 * KV_COLS], g), aug).astype(BF16)
        kwin_ref[:, sl] = jnp.where(lane < AUG_LANE, _head_slot(win[:, :KV_COLS], g), aug).astype(BF16)
    vsel_ref[...] = kv[:, 3 * KV_COLS:4 * KV_COLS].astype(BF16)
    vwin_ref[...] = win[:, KV_COLS:].astype(BF16)
    kv_t = kv.T
    kvt_ref[0] = kv_t
    wint_ref[0] = win.T
    for p in range(tm // LANES):
        pages_ref[p] = kv_t[:2 * KV_COLS, p * LANES:(p + 1) * LANES]


def _attn_in_proj(x2, gain, w_main, w_gate, seq_len=None):
    rows, d = x2.shape
    tm = min(PROJ_ROWS, rows)
    nmain = w_main.shape[1]
    row = lambda n: pl.BlockSpec((tm, n), lambda i: (i, 0))
    outs = [(d, F32), (LANES, F32)]
    if seq_len is None:
        outs += [(4 * KV_COLS, F32), (2 * KV_COLS, F32)]
    else:
        outs += [(N_KV_HEADS * LANES, BF16), (KV_COLS, BF16), (N_KV_HEADS * LANES, BF16), (KV_COLS, BF16)]
    out_specs = [row(n) for n, _ in outs]
    out_shape = [jax.ShapeDtypeStruct((rows, n), dt) for n, dt in outs]
    if seq_len is not None:
        nt = seq_len // tm
        for n in (4 * KV_COLS, 2 * KV_COLS):
            out_specs.append(pl.BlockSpec((1, n, tm), lambda i: (i // nt, 0, i % nt)))
            out_shape.append(jax.ShapeDtypeStruct((rows // seq_len, n, seq_len), F32))
        out_specs.append(pl.BlockSpec((tm // LANES, 2 * KV_COLS, LANES), lambda i: (i, 0, 0)))
        out_shape.append(jax.ShapeDtypeStruct((rows // LANES, 2 * KV_COLS, LANES), F32))
    return pl.pallas_call(
        functools.partial(_attn_in_kernel, seq_len),
        grid=(rows // tm,),
        in_specs=[row(d), _const_spec((1, d)), _const_spec((d, nmain)), _const_spec((d, LANES))],
        out_specs=out_specs,
        out_shape=out_shape,
        compiler_params=_cparams(("arbitrary",)),
        name="attn_in_proj",
    )(x2, gain, w_main, w_gate)


def _matmul_res_kernel(a_ref, w_ref, x_ref, o_ref):
    o_ref[...] = x_ref[...] + _dot(a_ref[...].astype(BF16), w_ref[...])


def _matmul_res(a, w, x):
    rows, k = a.shape
    n = w.shape[1]
    tm = min(PROJ_ROWS, rows)
    return pl.pallas_call(
        _matmul_res_kernel,
        grid=(rows // tm,),
        in_specs=[pl.BlockSpec((tm, k), lambda i: (i, 0)), _const_spec((k, n)),
                  pl.BlockSpec((tm, n), lambda i: (i, 0))],
        out_specs=pl.BlockSpec((tm, n), lambda i: (i, 0)),
        out_shape=jax.ShapeDtypeStruct((rows, n), F32),
        compiler_params=_cparams(("arbitrary",)),
        name="matmul_res",
    )(a, w, x)


def _finish_compress(a, b, w2):
    nseg = a.shape[0]
    pre = a + pltpu.roll(b, nseg - 1, 0)
    out = _dot(_gelu(pre).astype(BF16), w2)
    rowi = lax.broadcasted_iota(jnp.int32, (nseg, 1), 0)
    return jnp.where(rowi < nseg - 1, out, 0.0)


def _prompt_summaries_kernel(a_ref, b_ref, w2_ref, ck_ref, cv_ref):
    nseg = a_ref.shape[1]
    a = a_ref[0]
    b = b_ref[0]
    kc = _finish_compress(a[:, :KV_COLS], b[:, :KV_COLS], w2_ref[0])
    n_i = lax.broadcasted_iota(jnp.int32, (nseg, 1), 0)
    hi = ((n_i * D_CMP) // LANES * LANES).astype(F32)
    lo = ((n_i * D_CMP) % LANES).astype(F32) + (L_CMP - 1) / 2
    aug = _position_aug(hi, lo)
    lane = lax.broadcasted_iota(jnp.int32, (1, LANES), 1)
    for g in range(N_KV_HEADS):
        ck_ref[0, :, g * LANES:(g + 1) * LANES] = jnp.where(lane < AUG_LANE, _head_slot(kc, g), aug).astype(BF16)
    cv_ref[0] = _finish_compress(a[:, KV_COLS:], b[:, KV_COLS:], w2_ref[1]).astype(BF16)


def _prompt_summaries(a3, b3, w2_bd):
    bsz, nseg, _ = a3.shape
    seq = lambda n: pl.BlockSpec((1, nseg, n), lambda i: (i, 0, 0))
    return pl.pallas_call(
        _prompt_summaries_kernel,
        grid=(bsz,),
        in_specs=[seq(2 * KV_COLS), seq(2 * KV_COLS), _const_spec(w2_bd.shape)],
        out_specs=[seq(N_KV_HEADS * LANES), seq(KV_COLS)],
        out_shape=[jax.ShapeDtypeStruct((bsz, nseg, N_KV_HEADS * LANES), BF16),
                   jax.ShapeDtypeStruct((bsz, nseg, KV_COLS), BF16)],
        compiler_params=_cparams(("arbitrary",)),
        name="prompt_summaries",
    )(a3, b3, w2_bd)


def _pool_sums_kernel(pp, page, x_ref, pe_ref, w1_ref, a_ref, b_ref, rows_s):
    ftiles = 2 * KV_COLS // LANES

    def to_rows(p, carry):
        xt = x_ref[p]
        r0 = pl.multiple_of(p * page, page)
        for c in range(ftiles):
            rows_s[c, pl.ds(r0, page), :] = xt[c * LANES:(c + 1) * LANES, :].T
        return carry

    lax.fori_loop(0, pp, to_rows, 0, unroll=4)
    nseg = pp * page // D_CMP
    tpk = KV_COLS // LANES
    for kind in range(2):
        a = jnp.zeros((nseg, KV_COLS), F32)
        b = jnp.zeros((nseg, KV_COLS), F32)
        for l in range(D_CMP):
            xs = jnp.concatenate([rows_s[kind * tpk + c, pl.ds(l, nseg, stride=D_CMP), :] for c in range(tpk)],
                                 axis=1)
            a = a + _dot((xs + pe_ref[kind, l:l + 1, :]).astype(BF16), w1_ref[kind, l])
            b = b + _dot((xs + pe_ref[kind, D_CMP + l:D_CMP + l + 1, :]).astype(BF16), w1_ref[kind, D_CMP + l])
        a_ref[:, kind * KV_COLS:(kind + 1) * KV_COLS] = a
        b_ref[:, kind * KV_COLS:(kind + 1) * KV_COLS] = b


def _pool_segment_sums(cache_t, pe_t, w1_bd, pp):
    n_pool, _, page = cache_t.shape
    assert page == LANES and n_pool % pp == 0
    nseg = pp * page // D_CMP
    return pl.pallas_call(
        functools.partial(_pool_sums_kernel, pp, page),
        grid=(n_pool // pp,),
        in_specs=[pl.BlockSpec((pp, 2 * KV_COLS, page), lambda i: (i, 0, 0)),
                  _const_spec(pe_t.shape), _const_spec(w1_bd.shape)],
        out_specs=[pl.BlockSpec((nseg, 2 * KV_COLS), lambda i: (i, 0))] * 2,
        out_shape=[jax.ShapeDtypeStruct((n_pool * page // D_CMP, 2 * KV_COLS), F32)] * 2,
        scratch_shapes=[pltpu.VMEM((2 * KV_COLS // LANES, pp * page, LANES), F32)],
        compiler_params=_cparams(("arbitrary",)),
        name="pool_segment_sums",
    )(cache_t, pe_t, w1_bd)


AUG_LANE = HEAD_DIM
N_PIECES = 3
SOFTMAX_ROWS = 32


def _position_aug(hi, lo):
    lane = lax.broadcasted_iota(jnp.int32, (1, LANES), 1)
    in_hi = (lane >= AUG_LANE) & (lane < AUG_LANE + N_PIECES)
    in_lo = (lane >= AUG_LANE + N_PIECES) & (lane < AUG_LANE + 2 * N_PIECES)
    return jnp.where(in_hi, hi, jnp.where(in_lo, lo, 0.0))


def _head_slot(x, g):
    tile = x[:, LANES * (g // 2):LANES * (g // 2 + 1)]
    return pltpu.roll(tile, HEAD_DIM, 1) if g % 2 else tile


def _for_row_blocks(n_rows, body):
    for r0 in range(0, n_rows, SOFTMAX_ROWS):
        body(r0)


def _nsa_prompt_kernel(nc, q_ref, gates_ref, ck_ref, cv_ref, ksel_ref, vsel_ref, kwin_ref, vwin_ref,
                       onehot_ref, pool_ref, qaug_ref, x_ref, wout_ref, xo_ref,
                       qa_s, s_s, s2_s, p_s, p2_s, m_s, l_s, al_s, acc_s, ocmp_s, imp_s, score_s, selb_s, o_s):
    tq = Q_TILE
    rows = GROUP * tq
    rb = SOFTMAX_ROWS
    ncp = ck_ref.shape[1]
    qb = pl.program_id(1)
    q0 = qb * tq
    gates = gates_ref[0]
    lane = lax.broadcasted_iota(jnp.int32, (1, LANES), 1)
    tpos1 = q0 + lax.broadcasted_iota(jnp.int32, (tq, 1), 0)
    row_pos = lambda r0: q0 + r0 % tq + lax.broadcasted_iota(jnp.int32, (rb, 1), 0)
    win_len = WINDOW + tq
    w0 = pl.multiple_of(jnp.maximum(q0 - WINDOW, 0), tq)
    n_past_chunks = q0 // SEL_CHUNK
    pair = lambda g: slice(LANES * (g // 2), LANES * (g // 2 + 1))
    slot = lambda g: slice(LANES * g, LANES * (g + 1))

    def softmax_once(r0, width, valid, sbuf, pbuf):
        sb = jnp.where(valid, sbuf[pl.ds(r0, rb), :width], NEG_INF)
        m = jnp.max(sb, axis=-1, keepdims=True)
        e = jnp.exp(sb - m)
        den = jnp.sum(e, axis=-1, keepdims=True)
        p = e * jnp.where(m > 0.5 * NEG_INF, 1.0 / den, 0.0)
        pbuf[pl.ds(r0, rb), :width] = p.astype(BF16)
        return p

    n_i = lax.broadcasted_iota(jnp.int32, (1, ncp), 1)
    c_end = jnp.where(n_i < nc, n_i * D_CMP + (L_CMP - 1), jnp.int32(2 ** 30))

    for g in range(N_KV_HEADS):
        for r in range(GROUP):
            h = g * GROUP + r
            qh = q_ref[0, :, LANES * (h // 2):LANES * (h // 2 + 1)] * (HEAD_DIM ** -0.5)
            if h % 2:
                qh = pltpu.roll(qh, HEAD_DIM, 1)
            qa_s[g, r * tq:(r + 1) * tq, :] = jnp.where(lane < AUG_LANE, qh, qaug_ref[h:h + 1, :]).astype(BF16)

        s_s[:, :ncp] = _dot_nt(qa_s[g], ck_ref[0, :, slot(g)])
        imp_s[...] = jnp.zeros(imp_s.shape, F32)

        def cmp_block(r0):
            p = softmax_once(r0, ncp, c_end <= row_pos(r0), s_s, p_s)
            imp_s[pl.ds(r0 % tq, rb), :] += p

        _for_row_blocks(rows, cmp_block)
        ocmp_s[g] = _dot(p_s[:, :ncp], cv_ref[0, :, pair(g)])
        score_s[g * tq:(g + 1) * tq, :] = _selection_scores(_exact_pool(imp_s[...], pool_ref[...]), tpos1)

    selb_s[...] = jnp.where(_topk_mask(score_s[...], TOP_N) > 0.5, 0.0, NEG_INF).astype(BF16)

    for g in range(N_KV_HEADS):
        sel_bias = selb_s[g * tq:(g + 1) * tq, :]
        q_full = jnp.concatenate([qa_s[g], jnp.concatenate([sel_bias] * GROUP, axis=0)], axis=1)

        m_s[...] = jnp.full(m_s.shape, NEG_INF, F32)
        l_s[...] = jnp.zeros(l_s.shape, F32)
        acc_s[...] = jnp.zeros(acc_s.shape, F32)

        def sel_scores(c, buf):
            k0 = pl.multiple_of(c * SEL_CHUNK, SEL_CHUNK)
            k_full = jnp.concatenate([ksel_ref[0, pl.ds(k0, SEL_CHUNK), slot(g)],
                                      onehot_ref[pl.ds(k0, SEL_CHUNK), :]], axis=1)
            buf[:, :SEL_CHUNK] = _dot_nt(q_full, k_full)

        def sel_consume(c, buf, causal):
            pbuf = p_s if buf is s_s else p2_s
            k0 = pl.multiple_of(c * SEL_CHUNK, SEL_CHUNK)
            kpos = k0 + lax.broadcasted_iota(jnp.int32, (1, SEL_CHUNK), 1)

            def block(r0):
                rs = pl.ds(r0, rb)
                sb = buf[rs, :SEL_CHUNK]
                if causal:
                    sb = jnp.where(kpos <= row_pos(r0), sb, NEG_INF)
                m_old = m_s[rs, :]
                m_new = jnp.maximum(m_old, jnp.max(sb, axis=-1, keepdims=True))
                alpha = jnp.exp(m_old - m_new)
                p = jnp.exp(sb - jnp.tile(m_new, (1, SEL_CHUNK // LANES)))
                l_s[rs, :] = alpha * l_s[rs, :] + jnp.sum(p, axis=-1, keepdims=True)
                m_s[rs, :] = m_new
                al_s[rs, :] = alpha
                pbuf[rs, :SEL_CHUNK] = p.astype(BF16)

            _for_row_blocks(rows, block)
            acc_s[...] = al_s[...] * acc_s[...] + _dot(pbuf[:, :SEL_CHUNK], vsel_ref[0, pl.ds(k0, SEL_CHUNK), pair(g)])

        def run_chunks(first, count, ends_causal):
            bufs = (s_s, s2_s)
            for k in range(count):
                closes = ends_causal and k == count - 1
                if not closes:
                    sel_scores(first + k + 1, bufs[(k + 1) % 2])
                sel_consume(first + k, bufs[k % 2], closes)

        sel_scores(0, s_s)

        def past_group(i, carry):
            run_chunks(i * SEL_UNROLL, SEL_UNROLL, False)
            return carry

        n_groups = n_past_chunks // SEL_UNROLL
        lax.fori_loop(0, n_groups, past_group, 0)
        for rem in range(SEL_UNROLL):
            @pl.when(n_past_chunks - n_groups * SEL_UNROLL == rem)
            def _():
                run_chunks(n_groups * SEL_UNROLL, rem + 1, True)

        o_sel = acc_s[...] / l_s[...]

        s_s[:, :win_len] = _dot_nt(qa_s[g], kwin_ref[0, pl.ds(w0, win_len), slot(g)])
        wpos = w0 + lax.broadcasted_iota(jnp.int32, (1, win_len), 1)

        def win_block(r0):
            wd = row_pos(r0) - wpos
            softmax_once(r0, win_len, (wd >= 0) & (wd < WINDOW), s_s, p_s)

        _for_row_blocks(rows, win_block)
        o_win = _dot(p_s[:, :win_len], vwin_ref[0, pl.ds(w0, win_len), pair(g)])
        o_cmp = ocmp_s[g]

        halves = []
        for r in range(GROUP):
            c0 = 3 * (g * GROUP + r)
            rs = slice(r * tq, (r + 1) * tq)
            o_r = (gates[:, c0:c0 + 1] * o_cmp[rs] + gates[:, c0 + 1:c0 + 2] * o_sel[rs]
                   + gates[:, c0 + 2:c0 + 3] * o_win[rs])
            halves.append(pltpu.roll(o_r, HEAD_DIM, 1) if (r - g) % 2 else o_r)
        for j in range(GROUP // 2):
            tile = jnp.where(lane < HEAD_DIM, halves[2 * j], halves[2 * j + 1])
            o_s[:, LANES * (2 * g + j):LANES * (2 * g + j + 1)] = tile.astype(BF16)

    xo_ref[0] = x_ref[0] + _dot(o_s[...], wout_ref[...])


def _nsa_prompt_attend(q, gates, cmp_k, cmp_v, ksel, vsel, kwin, vwin, onehot, pool, qaug, x, w_out):
    bsz, t, d = q.shape
    nc = (t - L_CMP) // D_CMP + 1
    ncp = cmp_k.shape[1]
    rows = GROUP * Q_TILE
    width = max(ncp, SEL_CHUNK, WINDOW + Q_TILE)
    seq = lambda a: pl.BlockSpec((1,) + a.shape[1:], lambda b, i: (b, 0, 0), pipeline_mode=pl.Buffered(1))
    stat = pltpu.VMEM((rows, LANES), F32)
    return pl.pallas_call(
        functools.partial(_nsa_prompt_kernel, nc),
        grid=(bsz, t // Q_TILE),
        in_specs=[pl.BlockSpec((1, Q_TILE, d), lambda b, i: (b, i, 0)),
                  pl.BlockSpec((1, Q_TILE, LANES), lambda b, i: (b, i, 0)),
                  seq(cmp_k), seq(cmp_v), seq(ksel), seq(vsel), seq(kwin), seq(vwin),
                  _const_spec(onehot.shape), _const_spec(pool.shape), _const_spec(qaug.shape),
                  pl.BlockSpec((1, Q_TILE, d), lambda b, i: (b, i, 0)), _const_spec(w_out.shape)],
        out_specs=pl.BlockSpec((1, Q_TILE, d), lambda b, i: (b, i, 0)),
        out_shape=jax.ShapeDtypeStruct((bsz, t, d), F32),
        scratch_shapes=[pltpu.VMEM((N_KV_HEADS, rows, LANES), BF16),
                        pltpu.VMEM((rows, width), F32),
                        pltpu.VMEM((rows, SEL_CHUNK), F32),
                        pltpu.VMEM((rows, width), BF16),
                        pltpu.VMEM((rows, SEL_CHUNK), BF16),
                        stat, stat, stat, stat,
                        pltpu.VMEM((N_KV_HEADS, rows, LANES), F32),
                        pltpu.VMEM((Q_TILE, ncp), F32),
                        pltpu.VMEM((N_KV_HEADS * Q_TILE, LANES), F32),
                        pltpu.VMEM((N_KV_HEADS * Q_TILE, LANES), BF16),
                        pltpu.VMEM((Q_TILE, d), BF16)],
        compiler_params=_cparams(("arbitrary", "arbitrary")),
        name="nsa_prompt_attend",
    )(q, gates, cmp_k, cmp_v, ksel, vsel, kwin, vwin, onehot, pool, qaug, x, w_out)


def _sample_qpad(qf, tq):
    parts = []
    for g in range(N_KV_HEADS):
        piece = qf[:, g * KV_COLS:(g + 1) * KV_COLS]
        for r in range(GROUP):
            parts.append(_place_slot(piece, r, g))
    return jnp.concatenate(parts, axis=0).astype(BF16)


def _sample_rows_to_tokens(o, tq):
    chunks = []
    for g in range(N_KV_HEADS):
        chunk = jnp.zeros((tq, KV_COLS), F32)
        for r in range(GROUP):
            i0 = (g * GROUP + r) * tq
            chunk = chunk + _place_slot(o[i0:i0 + tq], g, r)
        chunks.append(chunk)
    return jnp.concatenate(chunks, axis=1)


def _nsa_sample_cmp_kernel(n_pages, past_len, tq, pt_ref, q_ref, w2_ref, pool_ref, slope_ref, *refs):
    a_refs = refs[:n_pages]
    b_refs = refs[n_pages:2 * n_pages]
    ocmp_ref, sel_ref = refs[2 * n_pages:]
    del pt_ref
    rows = N_HEADS * tq
    a = jnp.concatenate([r[0] for r in a_refs], axis=0)
    b = jnp.concatenate([r[0] for r in b_refs], axis=0)
    nseg = a.shape[0]
    kc = _finish_compress(a[:, :KV_COLS], b[:, :KV_COLS], w2_ref[0]).astype(BF16)
    vc = _finish_compress(a[:, KV_COLS:], b[:, KV_COLS:], w2_ref[1]).astype(BF16)

    qpad = _sample_qpad(q_ref[0] * (HEAD_DIM ** -0.5), tq)
    slope = slope_ref[...]
    rowi = lax.broadcasted_iota(jnp.int32, (rows, 1), 0)
    tpos = past_len + rowi % tq
    tposf = tpos.astype(F32)
    nc = (past_len + tq - L_CMP) // D_CMP + 1
    n_i = lax.broadcasted_iota(jnp.int32, (1, nseg), 1)
    c_end = jnp.where(n_i < nc, n_i * D_CMP + (L_CMP - 1), jnp.int32(2 ** 30))
    center = (n_i * D_CMP).astype(F32) + (L_CMP - 1) / 2
    s = _dot_nt(qpad, kc) - slope * (tposf - center)
    p_cmp = _masked_softmax(s, c_end <= tpos)
    ocmp_ref[0] = _dot(p_cmp.astype(BF16), vc)

    imps = []
    for g in range(N_KV_HEADS):
        i0 = g * GROUP * tq
        imp = p_cmp[i0:i0 + tq]
        for r in range(1, GROUP):
            imp = imp + p_cmp[i0 + r * tq:i0 + (r + 1) * tq]
        imps.append(imp)
    tpos_g = past_len + lax.broadcasted_iota(jnp.int32, (N_KV_HEADS * tq, 1), 0) % tq
    score = _selection_scores(_exact_pool(jnp.concatenate(imps, axis=0), pool_ref[...]), tpos_g)
    n_sel_rows = N_KV_HEADS * tq
    pad = jnp.full((LANES - n_sel_rows, LANES), NEG_INF, F32)
    sel = _topk_mask(jnp.concatenate([score, pad], axis=0), TOP_N)[:n_sel_rows]
    sel_ref[0] = jnp.concatenate([sel[g * tq:(g + 1) * tq] for g in range(N_KV_HEADS) for _ in range(GROUP)],
                                 axis=0).astype(BF16)


def _nsa_sample_cmp(page_table, q, a_all, b_all, w2_bd, pool, slope_rows, page, past_len):
    bs, tq, d = q.shape
    n_pages = page_table.shape[1]
    seg_pp = page // D_CMP
    rows = N_HEADS * tq
    a3 = a_all.reshape(-1, seg_pp, 2 * KV_COLS)
    b3 = b_all.reshape(-1, seg_pp, 2 * KV_COLS)

    def page_spec(p):
        return pl.BlockSpec((1, seg_pp, 2 * KV_COLS), lambda s, pt: (pt[s, p], 0, 0))

    cst = lambda shape: pl.BlockSpec(shape, lambda s, pt: (0,) * len(shape))
    grid_spec = pltpu.PrefetchScalarGridSpec(
        num_scalar_prefetch=1,
        grid=(bs,),
        in_specs=[pl.BlockSpec((1, tq, d), lambda s, pt: (s, 0, 0)),
                  cst(w2_bd.shape), cst(pool.shape), cst(slope_rows.shape)]
                 + [page_spec(p) for p in range(n_pages)] * 2,
        out_specs=[pl.BlockSpec((1, rows, KV_COLS), lambda s, pt: (s, 0, 0)),
                   pl.BlockSpec((1, rows, LANES), lambda s, pt: (s, 0, 0))],
    )
    return pl.pallas_call(
        functools.partial(_nsa_sample_cmp_kernel, n_pages, past_len, tq),
        grid_spec=grid_spec,
        out_shape=[jax.ShapeDtypeStruct((bs, rows, KV_COLS), F32),
                   jax.ShapeDtypeStruct((bs, rows, LANES), BF16)],
        compiler_params=_cparams(("arbitrary",)),
        name="nsa_sample_cmp",
    )(page_table, q, w2_bd, pool, slope_rows, *([a3] * n_pages), *([b3] * n_pages))


def _nsa_sample_attend_kernel(n_pages, page, tq, pt_ref, q_ref, gates_ref, kvn_ref, winn_ref, winp_ref,
                              ocmp_ref, sel_ref, expand_ref, slope_ref, *refs):
    page_refs = refs[:n_pages]
    o_ref, newwin_ref = refs[n_pages:]
    del pt_ref
    rows = N_HEADS * tq
    past_len = n_pages * page
    wbuf = winp_ref.shape[2]
    qpad = _sample_qpad(q_ref[0] * (HEAD_DIM ** -0.5), tq)
    slope = slope_ref[...]
    rowi = lax.broadcasted_iota(jnp.int32, (rows, 1), 0)
    tpos = past_len + rowi % tq
    tposf = tpos.astype(F32)
    newpos = past_len + lax.broadcasted_iota(jnp.int32, (1, LANES), 1)
    pad_rows = lambda a: jnp.concatenate([a, jnp.zeros((LANES - tq, a.shape[1]), F32)], axis=0)

    kvn = pad_rows(kvn_ref[0])
    k_new = kvn[:, 2 * KV_COLS:3 * KV_COLS].astype(BF16)
    v_new = kvn[:, 3 * KV_COLS:4 * KV_COLS].astype(BF16)
    scores = [_dot(qpad, r[0, :KV_COLS, :].astype(BF16)) for r in page_refs]
    scores.append(_dot_nt(qpad, k_new))
    s = jnp.concatenate(scores, axis=1)
    nk = past_len + LANES
    kpos = lax.broadcasted_iota(jnp.int32, (1, nk), 1)
    picked = _dot(sel_ref[0], expand_ref[...]) > 0.5
    s = s - slope * (tposf - kpos.astype(F32))
    p = _masked_softmax(s, picked & (kpos <= tpos)).astype(BF16)
    o_sel = _dot(p[:, past_len:], v_new)
    for i, r in enumerate(page_refs):
        o_sel = o_sel + _dot_nt(p[:, i * page:(i + 1) * page], r[0, KV_COLS:, :].astype(BF16))

    winp = winp_ref[0]
    winn = pad_rows(winn_ref[0])
    winn_t = pltpu.roll(winn.T, LANES - tq, 1)
    lane_w = lax.broadcasted_iota(jnp.int32, (1, wbuf), 1)
    newwin_ref[0] = jnp.where(lane_w >= wbuf - tq, jnp.concatenate([winn_t] * (wbuf // LANES), axis=1),
                              pltpu.roll(winp, wbuf - tq, 1))
    sp = _dot(qpad, winp[:KV_COLS, :].astype(BF16))
    sn = _dot_nt(qpad, winn[:, :KV_COLS].astype(BF16))
    wpos_p = past_len - wbuf + lax.broadcasted_iota(jnp.int32, (1, wbuf), 1)
    wpos = jnp.concatenate([wpos_p, newpos], axis=1)
    wd = tpos - wpos
    s = jnp.concatenate([sp, sn], axis=1) - slope * wd.astype(F32)
    p = _masked_softmax(s, (wd >= 0) & (wd < WINDOW) & (wpos >= 0)).astype(BF16)
    o_win = _dot_nt(p[:, :wbuf], winp[KV_COLS:, :].astype(BF16)) + _dot(p[:, wbuf:], winn[:, KV_COLS:].astype(BF16))

    gates = gates_ref[0]
    o_cmp = ocmp_ref[0]
    merged = []
    for h in range(N_HEADS):
        rs = slice(h * tq, (h + 1) * tq)
        merged.append(gates[:, 3 * h:3 * h + 1] * o_cmp[rs] + gates[:, 3 * h + 1:3 * h + 2] * o_sel[rs]
                      + gates[:, 3 * h + 2:3 * h + 3] * o_win[rs])
    o_ref[0] = _sample_rows_to_tokens(jnp.concatenate(merged, axis=0), tq).astype(o_ref.dtype)


def _nsa_sample_attend(page_table, cache_t, q, gates, kv_new, win_new, win_past_t, o_cmp, sel, expand, slope_rows):
    bs, tq, d = q.shape
    n_pages = page_table.shape[1]
    page = cache_t.shape[2]
    wbuf = win_past_t.shape[2]
    rows = N_HEADS * tq

    def page_spec(p):
        return pl.BlockSpec((1, 2 * KV_COLS, page), lambda s, pt: (pt[s, p], 1, 0))

    per = lambda r, n: pl.BlockSpec((1, r, n), lambda s, pt: (s, 0, 0))
    cst = lambda shape: pl.BlockSpec(shape, lambda s, pt: (0,) * len(shape))
    grid_spec = pltpu.PrefetchScalarGridSpec(
        num_scalar_prefetch=1,
        grid=(bs,),
        in_specs=[per(tq, d), per(tq, LANES), per(tq, 4 * KV_COLS), per(tq, 2 * KV_COLS),
                  per(2 * KV_COLS, wbuf), per(rows, KV_COLS), per(rows, LANES),
                  cst(expand.shape), cst(slope_rows.shape)]
                 + [page_spec(p) for p in range(n_pages)],
        out_specs=[per(tq, d), per(2 * KV_COLS, wbuf)],
    )
    return pl.pallas_call(
        functools.partial(_nsa_sample_attend_kernel, n_pages, page, tq),
        grid_spec=grid_spec,
        out_shape=[jax.ShapeDtypeStruct((bs, tq, d), BF16),
                   jax.ShapeDtypeStruct((bs, 2 * KV_COLS, wbuf), F32)],
        compiler_params=_cparams(("arbitrary",)),
        name="nsa_sample_attend",
    )(page_table, q, gates, kv_new, win_new, win_past_t, o_cmp, sel, expand, slope_rows,
      *([cache_t] * n_pages))


def _cmul_add(ar, ai, hr, hi, xr, xi):
    return ar * hr - ai * hi + xr, ar * hi + ai * hr + xi


def _s5_kernel(mode, tm, x_ref, h0r_ref, h0i_ref, gain_ref, bre_ref, bim_ref, are_ref, aim_ref,
               pwr_ref, pwi_ref, cre_ref, cim_ref, d_ref, wglu_ref, perm_ref, permt_ref,
               xo_ref, hfr_ref, hfi_ref, hr_s, hi_s, cr_s, ci_s):
    d = x_ref.shape[1]
    nstate = cr_s.shape[1]
    nblk = bre_ref.shape[0]
    cb = d // nblk
    sb = nstate // nblk
    tpb = sb // LANES
    x = x_ref[...]
    uf = _rms(x, gain_ref[...])
    ub = uf.astype(BF16)

    if mode == "carry":
        nj = tm // SUBLANES
        ub = _dot(perm_ref[...], ub).astype(BF16)
    for j in range(nblk):
        uj = ub[:, j * cb:(j + 1) * cb]
        br = _dot(uj, bre_ref[j])
        bi = _dot(uj, bim_ref[j])
        for k in range(tpb):
            hr_s[j * tpb + k] = br[:, k * LANES:(k + 1) * LANES]
            hi_s[j * tpb + k] = bi[:, k * LANES:(k + 1) * LANES]

    lc = SSM_LANE_CHUNK
    par = lc // LANES
    if mode == "carry":
        @pl.when(pl.program_id(1) == 0)
        def _():
            cr_s[...] = jnp.broadcast_to(h0r_ref[0], cr_s.shape)
            ci_s[...] = jnp.broadcast_to(h0i_ref[0], ci_s.shape)

        sub = lax.broadcasted_iota(jnp.int32, (SUBLANES, 1), 0)
        for c in range(nstate // lc):
            sl = slice(c * lc, (c + 1) * lc)
            ar = jnp.broadcast_to(are_ref[:, sl], (SUBLANES, lc))
            ai = jnp.broadcast_to(aim_ref[:, sl], (SUBLANES, lc))

            tiles = range(c * par, (c + 1) * par)
            lane = lambda v, k: v[:, k * LANES:(k + 1) * LANES]

            def p1(jj, carry):
                st, mult = carry
                idx = pl.ds(pl.multiple_of(jj * SUBLANES, SUBLANES), SUBLANES)
                out = []
                for k, q in enumerate(tiles):
                    nr, ni = _cmul_add(mult[2 * k], mult[2 * k + 1], st[2 * k], st[2 * k + 1],
                                       hr_s[q, idx, :], hi_s[q, idx, :])
                    hr_s[q, idx, :] = nr
                    hi_s[q, idx, :] = ni
                    out += [nr, ni]
                return tuple(out), mult

            z = jnp.zeros((SUBLANES, LANES), F32)
            mult0 = tuple(lane(v, k) for k in range(par) for v in (ar, ai))
            lf, _ = lax.fori_loop(0, nj, p1, ((z,) * (2 * par), mult0))
            lfr = jnp.concatenate(lf[0::2], axis=1)
            lfi = jnp.concatenate(lf[1::2], axis=1)

            alr = pwr_ref[nj - 1:nj, sl]
            ali = pwi_ref[nj - 1:nj, sl]
            inr = cr_s[0:1, sl]
            ini = ci_s[0:1, sl]
            hin_r = jnp.zeros((SUBLANES, lc), F32)
            hin_i = jnp.zeros((SUBLANES, lc), F32)
            for s in range(SUBLANES):
                hin_r = jnp.where(sub == s, inr, hin_r)
                hin_i = jnp.where(sub == s, ini, hin_i)
                inr, ini = _cmul_add(alr, ali, inr, ini, lfr[s:s + 1], lfi[s:s + 1])
            cr_s[:, sl] = jnp.broadcast_to(inr, (SUBLANES, lc))
            ci_s[:, sl] = jnp.broadcast_to(ini, (SUBLANES, lc))

            def p2(jj, carry):
                idx = pl.ds(pl.multiple_of(jj * SUBLANES, SUBLANES), SUBLANES)
                pr = pwr_ref[pl.ds(jj, 1), sl]
                pi_ = pwi_ref[pl.ds(jj, 1), sl]
                for k, q in enumerate(tiles):
                    nr, ni = _cmul_add(lane(pr, k), lane(pi_, k), lane(hin_r, k), lane(hin_i, k),
                                       hr_s[q, idx, :], hi_s[q, idx, :])
                    hr_s[q, idx, :] = nr
                    hi_s[q, idx, :] = ni
                return carry

            lax.fori_loop(0, nj, p2, 0, unroll=4)

        @pl.when(pl.program_id(1) == pl.num_programs(1) - 1)
        def _():
            hfr_ref[0] = cr_s[0:1, :]
            hfi_ref[0] = ci_s[0:1, :]
    else:
        nseq = tm // mode
        for q in range(nstate // LANES):
            sl = slice(q * LANES, (q + 1) * LANES)
            ar = jnp.broadcast_to(are_ref[:, sl], (nseq, LANES))
            ai = jnp.broadcast_to(aim_ref[:, sl], (nseq, LANES))
            sr = h0r_ref[:, sl]
            si = h0i_ref[:, sl]
            for t in range(mode):
                idx = pl.ds(t, nseq, stride=mode)
                sr, si = _cmul_add(ar, ai, sr, si, hr_s[q, idx, :], hi_s[q, idx, :])
                hr_s[q, idx, :] = sr
                hi_s[q, idx, :] = si
            hfr_ref[:, sl] = sr
            hfi_ref[:, sl] = si

    def block_rows(ref, j):
        return jnp.concatenate([ref[j * tpb + k] for k in range(tpb)], axis=1).astype(BF16)

    ys = []
    for j in range(nblk):
        ys.append(_dot(block_rows(hr_s, j), cre_ref[j]) - _dot(block_rows(hi_s, j), cim_ref[j]))
    y = jnp.concatenate(ys, axis=1)
    if mode == "carry":
        y = _exact_rows(permt_ref[...], y)
    y = y + d_ref[...] * uf
    ab = _dot(_gelu(y).astype(BF16), wglu_ref[...])
    xo_ref[...] = x + ab[:, :d] * _sigmoid(ab[:, d:])


def _s5_layer(x2, h0r, h0i, seq_len, n_seq, gain, prm):
    rows, d = x2.shape
    nstate = h0r.shape[1]
    if seq_len >= LAYER_ROWS:
        mode, tm = "carry", LAYER_ROWS
        nt = seq_len // tm
        grid = (n_seq, nt)
        xmap = lambda b, i: (b * nt + i, 0)
        h0_spec = pl.BlockSpec((1, 1, nstate), lambda b, i: (b, 0, 0))
        hf_spec = pl.BlockSpec((1, 1, nstate), lambda b, i: (b, 0, 0))
        h0r, h0i = h0r[:, None, :], h0i[:, None, :]
        hf_shape = (n_seq, 1, nstate)
        sem = ("arbitrary", "arbitrary")
        cst = lambda shape: pl.BlockSpec(shape, lambda b, i: (0,) * len(shape), pipeline_mode=pl.Buffered(1))
    else:
        mode = seq_len
        tm = min(LAYER_ROWS, rows)
        nseq_t = tm // seq_len
        grid = (rows // tm,)
        xmap = lambda i: (i, 0)
        h0_spec = pl.BlockSpec((nseq_t, nstate), lambda i: (i, 0))
        hf_spec = pl.BlockSpec((nseq_t, nstate), lambda i: (i, 0))
        hf_shape = (n_seq, nstate)
        sem = ("arbitrary",)
        cst = lambda shape: pl.BlockSpec(shape, lambda i: (0,) * len(shape), pipeline_mode=pl.Buffered(1))
    pw_r, pw_i = prm["pw_r"], prm["pw_i"]
    src = np.arange(tm).reshape(tm // SUBLANES, SUBLANES)
    perm = np.zeros((tm, tm), np.float32)
    perm[np.arange(tm), ((src % SUBLANES) * (tm // SUBLANES) + src // SUBLANES).reshape(-1)] = 1.0
    consts = [gain, prm["b_r"], prm["b_i"], prm["a_r"], prm["a_i"], pw_r, pw_i, prm["c_r"], prm["c_i"],
              prm["d"], prm["w_glu"], jnp.asarray(perm, BF16), jnp.asarray(perm.T, BF16)]
    outs = pl.pallas_call(
        functools.partial(_s5_kernel, mode, tm),
        grid=grid,
        in_specs=[pl.BlockSpec((tm, d), xmap), h0_spec, h0_spec] + [cst(c.shape) for c in consts],
        out_specs=[pl.BlockSpec((tm, d), xmap), hf_spec, hf_spec],
        out_shape=[jax.ShapeDtypeStruct((rows, d), F32), jax.ShapeDtypeStruct(hf_shape, F32),
                   jax.ShapeDtypeStruct(hf_shape, F32)],
        scratch_shapes=[pltpu.VMEM((nstate // LANES, tm, LANES), F32),
                        pltpu.VMEM((nstate // LANES, tm, LANES), F32),
                        pltpu.VMEM((SUBLANES, nstate), F32), pltpu.VMEM((SUBLANES, nstate), F32)],
        compiler_params=_cparams(sem),
        name="s5_layer",
    )(x2, h0r, h0i, *consts)
    xo, hfr, hfi = outs
    return xo, hfr.reshape(n_seq, nstate), hfi.reshape(n_seq, nstate)


def _s5_params(a_re, a_im, log_dt, b_re, b_im, c_re, c_im, d_skip, w_glu, n_pow):
    g, p = a_re.shape
    ch = b_re.shape[2]
    lr, li = a_re.astype(F32), a_im.astype(F32)
    dt = jnp.exp(log_dt.astype(F32))[:, None]
    mag = jnp.exp(lr * dt)
    ar, ai = mag * jnp.cos(li * dt), mag * jnp.sin(li * dt)
    den = lr * lr + li * li
    kr = ((ar - 1.0) * lr + ai * li) / den
    ki = (ai * lr - (ar - 1.0) * li) / den
    br, bi = b_re.astype(F32), b_im.astype(F32)
    bbar_r = kr[:, :, None] * br - ki[:, :, None] * bi
    bbar_i = kr[:, :, None] * bi + ki[:, :, None] * br
    nblk = g // SSM_GROUP_BLOCK
    eye = jnp.eye(SSM_GROUP_BLOCK, dtype=F32)

    def b_blocks(m):
        m = m.reshape(nblk, SSM_GROUP_BLOCK, p, ch)
        return jnp.einsum("jgpc,gh->jgchp", m, eye).reshape(nblk, SSM_GROUP_BLOCK * ch, SSM_GROUP_BLOCK * p).astype(BF16)

    def c_blocks(m):
        m = m.reshape(nblk, SSM_GROUP_BLOCK, ch, p)
        return jnp.einsum("jgcp,gh->jgphc", m, eye).reshape(nblk, SSM_GROUP_BLOCK * p, SSM_GROUP_BLOCK * ch).astype(BF16)

    ar, ai = ar.reshape(1, g * p), ai.reshape(1, g * p)
    pw_r, pw_i = [ar], [ai]
    for _ in range(n_pow - 1):
        pw_r, pw_i = pw_r + [pw_r[-1] * ar - pw_i[-1] * ai], pw_i + [pw_r[-1] * ai + pw_i[-1] * ar]
    return dict(b_r=b_blocks(bbar_r), b_i=b_blocks(bbar_i), a_r=ar, a_i=ai,
                pw_r=jnp.concatenate(pw_r, axis=0), pw_i=jnp.concatenate(pw_i, axis=0),
                c_r=c_blocks(c_re.astype(F32)), c_i=c_blocks(c_im.astype(F32)),
                d=d_skip.astype(F32).reshape(1, -1), w_glu=w_glu.astype(BF16))


def _ffn_ple_kernel(mode, final_norm, x_ref, p_ref, b1_ref, b2_ref, gffn_ref, wup_ref, cw_ref, cb_ref,
                    wdown_ref, gple_ref, wproj_ref, wgate_ref, gfin_ref, xo_ref, tail_ref, carry_ref):
    tm = x_ref.shape[0]
    f = cw_ref.shape[1]
    x = x_ref[...]
    u = _rms(x, gffn_ref[...]).astype(BF16)
    rowi = lax.broadcasted_iota(jnp.int32, (tm, 1), 0)
    if mode == "carry":
        @pl.when(pl.program_id(1) == 0)
        def _():
            carry_ref[SUBLANES - 2:SUBLANES, :] = b1_ref[0]

    fc = f // FFN_CHUNKS
    x1 = x
    for c in range(FFN_CHUNKS):
        cs = slice(c * fc, (c + 1) * fc)
        h = _dot(u, wup_ref[:, cs])
        gate_branch = _dot(u, wup_ref[:, f + c * fc:f + (c + 1) * fc])
        r1 = pltpu.roll(h, 1, 0)
        r2 = pltpu.roll(h, 2, 0)
        if mode == "carry":
            c0 = carry_ref[SUBLANES - 2:SUBLANES - 1, cs]
            c1 = carry_ref[SUBLANES - 1:SUBLANES, cs]
            hm1 = jnp.where(rowi == 0, c1, r1)
            hm2 = jnp.where(rowi == 0, c0, jnp.where(rowi == 1, c1, r2))
            carry_ref[:, cs] = h[tm - SUBLANES:tm, :]
            tail_ref[:, cs] = h[tm - SUBLANES:tm, :]
        else:
            t = rowi % mode
            hm1 = jnp.where(t == 0, b1_ref[:, cs], r1)
            hm2 = jnp.where(t < 2, b2_ref[:, cs], r2)
            tail_ref[:, cs] = h
        conv = cb_ref[:, cs] + cw_ref[0:1, cs] * hm2 + cw_ref[1:2, cs] * hm1 + cw_ref[2:3, cs] * h
        act = (_gelu(conv) * gate_branch).astype(BF16)
        x1 = x1 + _dot(act, wdown_ref[cs, :])
    gate = _sigmoid(_dot(_rms(x1, gple_ref[...]).astype(BF16), wgate_ref[...]))
    x2 = x1 + _dot(p_ref[...].astype(BF16), wproj_ref[...]) * gate
    xo_ref[...] = _rms(x2, gfin_ref[...]) if final_norm else x2


def _ffn_ple(x2, p2, buf, seq_len, n_seq, final_norm, gffn, wup, cw, cb, wdown, gple, wproj, wgate, gfin):
    rows, d = x2.shape
    f = cw.shape[1]
    ple = p2.shape[1]
    consts = [gffn, wup, cw, cb, wdown, gple, wproj, wgate, gfin]
    if seq_len >= LAYER_ROWS:
        mode, tm = "carry", LAYER_ROWS
        nt = seq_len // tm
        grid = (n_seq, nt)
        rmap = lambda b, i: (b * nt + i, 0)
        b1, b2 = buf, buf
        bspec = pl.BlockSpec((1, CONV_W - 1, f), lambda b, i: (b, 0, 0))
        tail_rows = SUBLANES
        sem = ("arbitrary", "arbitrary")
        cst = lambda shape: pl.BlockSpec(shape, lambda b, i: (0,) * len(shape), pipeline_mode=pl.Buffered(1))
    else:
        mode = seq_len
        tm = min(LAYER_ROWS, rows)
        grid = (rows // tm,)
        rmap = lambda i: (i, 0)
        b1 = jnp.pad(buf[:, 1:], ((0, 0), (0, seq_len - 1), (0, 0))).reshape(rows, f)
        b2 = jnp.pad(buf, ((0, 0), (0, seq_len - (CONV_W - 1)), (0, 0))).reshape(rows, f)
        bspec = pl.BlockSpec((tm, f), rmap)
        tail_rows = tm
        sem = ("arbitrary",)
        cst = lambda shape: pl.BlockSpec(shape, lambda i: (0,) * len(shape), pipeline_mode=pl.Buffered(1))
    n_tiles = rows // tm
    xo, tail = pl.pallas_call(
        functools.partial(_ffn_ple_kernel, mode, final_norm),
        grid=grid,
        in_specs=[pl.BlockSpec((tm, d), rmap), pl.BlockSpec((tm, ple), rmap), bspec, bspec]
                 + [cst(c.shape) for c in consts],
        out_specs=[pl.BlockSpec((tm, d), rmap), pl.BlockSpec((tail_rows, f), rmap)],
        out_shape=[jax.ShapeDtypeStruct((rows, d), F32), jax.ShapeDtypeStruct((n_tiles * tail_rows, f), F32)],
        scratch_shapes=[pltpu.VMEM((SUBLANES, f), F32)],
        compiler_params=_cparams(sem),
        name="ffn_ple",
    )(x2, p2, b1, b2, *consts)
    if mode == "carry":
        new_buf = tail.reshape(n_seq, n_tiles // n_seq, SUBLANES, f)[:, -1, SUBLANES - (CONV_W - 1):, :]
    else:
        new_buf = tail.reshape(n_seq, seq_len, f)[:, seq_len - (CONV_W - 1):, :]
    return xo, new_buf


def _block_diag_heads(w):
    eye = jnp.eye(N_KV_HEADS, dtype=w.dtype)
    out = jnp.einsum("...de,gh->...gdhe", w, eye)
    return out.reshape(*w.shape[:-2], KV_COLS, KV_COLS)


def _slopes():
    return np.exp2(-8.0 * np.arange(1, N_HEADS + 1, dtype=np.float64) / N_HEADS).astype(np.float32)


def _query_aug(slopes):
    rest = jnp.asarray(slopes, F32)
    aug = jnp.zeros((N_HEADS, LANES), F32)
    for i in range(N_PIECES):
        piece = rest.astype(BF16).astype(F32)
        rest = rest - piece
        aug = aug.at[:, AUG_LANE + i].set(piece).at[:, AUG_LANE + N_PIECES + i].set(piece)
    return aug


def _expand_matrix(n_keys):
    return (np.arange(LANES)[:, None] == (np.arange(n_keys)[None, :] // L_SEL)).astype(np.float32)


def _pool_matrix(n_cmp):
    return (np.arange(n_cmp)[:, None] // SEL_PER_CMP == np.arange(LANES)[None, :]).astype(np.float32)


def _nsa_layer(xp, xs, cache_l, cache_win_l, page_table, gain, w_in, w_out, pe, w1, w2):
    bsz, t, d = xp.shape
    bs, tq, _ = xs.shape
    page = cache_l.shape[1]
    n_pages = page_table.shape[1]
    past_len = n_pages * page
    n_pool = cache_l.shape[0]
    assert t % SEL_CHUNK == 0 and t // L_SEL <= LANES and t >= WINDOW + Q_TILE
    assert (past_len + LANES) // L_SEL <= LANES and page % D_CMP == 0 and tq <= SUBLANES

    qcols = N_HEADS * HEAD_DIM
    nmain = qcols + 6 * KV_COLS
    w_main = w_in[:, :nmain].astype(BF16)
    w_gate = jnp.pad(w_in[:, nmain:], ((0, 0), (0, LANES - (w_in.shape[1] - nmain)))).astype(BF16)
    w_out_b = w_out.astype(BF16)
    pe_t = jnp.tile(pe, (1, 1, N_KV_HEADS))
    w1_bd = _block_diag_heads(w1).astype(BF16)
    w2_bd = _block_diag_heads(w2).astype(BF16)
    slopes = _slopes()

    q, gates, ksel, vsel, kwin, vwin, kv_t, win_t, kv_pages = _attn_in_proj(
        xp.reshape(bsz * t, d), gain, w_main, w_gate, seq_len=t)
    sh = lambda a: a.reshape(bsz, t, a.shape[-1])
    a_p, b_p = _pool_segment_sums(kv_pages, pe_t, w1_bd, math.gcd(kv_pages.shape[0], POOL_PAGES))
    seg3 = lambda a: a.reshape(bsz, t // D_CMP, 2 * KV_COLS)
    cmp_k, cmp_v = _prompt_summaries(seg3(a_p), seg3(b_p), w2_bd)
    onehot = jnp.asarray(_expand_matrix(t).T, BF16)
    pool = jnp.asarray(_pool_matrix(t // D_CMP), BF16)
    xp_new = _nsa_prompt_attend(sh(q), sh(gates), cmp_k, cmp_v, sh(ksel), sh(vsel), sh(kwin), sh(vwin), onehot, pool,
                                jnp.asarray(_query_aug(slopes)), xp, w_out_b)
    keep = min(WINDOW, t)
    from_t = lambda a, n: jnp.transpose(a.reshape(a.shape[0], n, N_KV_HEADS, HEAD_DIM, a.shape[-1]), (0, 4, 1, 2, 3))
    kv_p = from_t(kv_t, 4)
    win_p = from_t(win_t[:, :, t - keep:], 2)

    cache_t = jnp.transpose(cache_l, (0, 2, 3, 4, 1)).reshape(n_pool, 4 * KV_COLS, page)
    wbuf = cache_win_l.shape[1]
    win_past_t = jnp.transpose(cache_win_l, (0, 2, 3, 4, 1)).reshape(bs, 2 * KV_COLS, wbuf)
    qs, gates_s, kvs, wins = _attn_in_proj(xs.reshape(bs * tq, d), gain, w_main, w_gate)
    shs = lambda a: a.reshape(bs, tq, a.shape[-1])
    a_all, b_all = _pool_segment_sums(cache_t, pe_t, w1_bd, math.gcd(n_pool, POOL_PAGES))
    nseg_s = past_len // D_CMP
    pool_s = jnp.asarray(_pool_matrix(nseg_s), BF16)
    slope_s = jnp.asarray(np.repeat(slopes, tq)[:, None])
    o_cmp, sel = _nsa_sample_cmp(page_table, shs(qs), a_all, b_all, w2_bd, pool_s, slope_s, page, past_len)
    expand_s = jnp.asarray(_expand_matrix(past_len + LANES), BF16)
    o_s, new_win_t = _nsa_sample_attend(page_table, cache_t, shs(qs), shs(gates_s), shs(kvs), shs(wins), win_past_t,
                                        o_cmp, sel, expand_s, slope_s)
    xs_new = _matmul_res(o_s.reshape(bs * tq, d), w_out_b, xs.reshape(bs * tq, d)).reshape(bs, tq, d)
    kv_s = shs(kvs).reshape(bs, tq, 4, N_KV_HEADS, HEAD_DIM)
    win_s = from_t(new_win_t, 2)
    return xp_new, xs_new, kv_p, kv_s, win_p, win_s


def kernel(x_prompt, x_sample, cache_kv, cache_win, state_ssm_re, state_ssm_im, state_conv, page_table,
           p_prompt, p_sample, norm_mix, norm_ffn, norm_ple, norm_final, w_attn_in, w_attn_out,
           cmp_pe, cmp_w1, cmp_w2, ssm_a_re, ssm_a_im, ssm_log_dt, ssm_b_re, ssm_b_im, ssm_c_re, ssm_c_im,
           ssm_d, w_glu, w_ffn_up, ffn_conv_w, ffn_conv_b, w_ffn_down, w_ple_proj, w_ple_gate):
    bsz, t, d = x_prompt.shape
    bs, tq, _ = x_sample.shape
    depth = norm_mix.shape[0]
    f = ffn_conv_w.shape[2]
    g, p = ssm_a_re.shape[1:]
    xp, xs = x_prompt, x_sample
    row = lambda v: v.reshape(1, -1).astype(F32)
    kv_p, kv_s, win_p, win_s = [], [], [], []
    sre_p, sim_p, sre_s, sim_s = [], [], [], []
    cb_p, cb_s = [], []
    for i in range(depth):
        j = i // 2
        if i % 2 == 0:
            xp, xs, kvp, kvs, wp, ws = _nsa_layer(xp, xs, cache_kv[j], cache_win[j], page_table, row(norm_mix[i]),
                                                  w_attn_in[j], w_attn_out[j], cmp_pe[j], cmp_w1[j], cmp_w2[j])
            kv_p.append(kvp)
            kv_s.append(kvs)
            win_p.append(wp)
            win_s.append(ws)
        else:
            prm = _s5_params(ssm_a_re[j], ssm_a_im[j], ssm_log_dt[j], ssm_b_re[j], ssm_b_im[j], ssm_c_re[j],
                             ssm_c_im[j], ssm_d[j], w_glu[j], LAYER_ROWS // SUBLANES)
            zero = jnp.zeros((bsz, g * p), F32)
            xp2, hr, hi = _s5_layer(xp.reshape(bsz * t, d), zero, zero, t, bsz, row(norm_mix[i]), prm)
            xs2, hrs, his = _s5_layer(xs.reshape(bs * tq, d), state_ssm_re[j].reshape(bs, g * p).astype(F32),
                                      state_ssm_im[j].reshape(bs, g * p).astype(F32), tq, bs, row(norm_mix[i]), prm)
            xp, xs = xp2.reshape(bsz, t, d), xs2.reshape(bs, tq, d)
            sre_p.append(hr.reshape(bsz, g, p))
            sim_p.append(hi.reshape(bsz, g, p))
            sre_s.append(hrs.reshape(bs, g, p))
            sim_s.append(his.reshape(bs, g, p))
        last = i == depth - 1
        ffn_w = (row(norm_ffn[i]), w_ffn_up[i].astype(BF16), ffn_conv_w[i].astype(F32), row(ffn_conv_b[i]),
                 w_ffn_down[i].astype(BF16), row(norm_ple[i]), w_ple_proj[i].astype(BF16),
                 w_ple_gate[i].astype(BF16), row(norm_final))
        xp2, bp = _ffn_ple(xp.reshape(bsz * t, d), p_prompt[i].reshape(bsz * t, -1),
                           jnp.zeros((bsz, CONV_W - 1, f), F32), t, bsz, last, *ffn_w)
        xs2, bs_new = _ffn_ple(xs.reshape(bs * tq, d), p_sample[i].reshape(bs * tq, -1), state_conv[i].astype(F32),
                               tq, bs, last, *ffn_w)
        xp, xs = xp2.reshape(bsz, t, d), xs2.reshape(bs, tq, d)
        cb_p.append(bp)
        cb_s.append(bs_new)
    return (xp, xs, jnp.stack(kv_p), jnp.stack(kv_s), jnp.stack(win_p), jnp.stack(win_s),
            jnp.stack(sre_p), jnp.stack(sim_p), jnp.stack(sre_s), jnp.stack(sim_s),
            jnp.stack(cb_p), jnp.stack(cb_s))
```
